```python
import jax, jax.numpy as jnp
from jax import lax
import numpy as np

D_MODEL = 1024
BATCH = 32
SEQ = 256
DEPTH = 4
DEC_BATCH = 2
DEC_SEQ = 4096
PAST_LEN = 256

GRID_W = 64
HEAD_DIM = 64
NA_HEADS = 6
NA_KH = 8
NA_KW = 16
MLA_HEADS = 4
MLA_Q_LORA = 256
MLA_KV_LORA = 128
MLA_NOPE = 64
MLA_ROPE = 32
MLA_V = 64
GQA_HEADS = 6
GQA_KV_HEADS = 2
GQA_GROUP = GQA_HEADS // GQA_KV_HEADS
D_FF = 2816
CONV_W = 3
Q_BLOCK = 128
ROPE_THETA = 10000.0
EPS = 1e-6

NA_WIDTH = NA_HEADS * HEAD_DIM
MLA_WIDTH = MLA_HEADS * MLA_V
GQA_WIDTH = GQA_HEADS * HEAD_DIM
MIX_WIDTH = NA_WIDTH + MLA_WIDTH + GQA_WIDTH
IN_SPLITS = (NA_WIDTH, NA_WIDTH, NA_WIDTH,
             MLA_Q_LORA, MLA_KV_LORA, MLA_ROPE,
             GQA_WIDTH, GQA_KV_HEADS * HEAD_DIM, GQA_KV_HEADS * HEAD_DIM)
IN_COLS = sum(IN_SPLITS)

kernel_name = 'hybrid_na_mla_gqa_diffusion_step'


def rms_norm(x, g):
    xf = x.astype(jnp.float32)
    y = xf * lax.rsqrt(jnp.mean(xf * xf, axis=-1, keepdims=True) + EPS)
    return y.astype(x.dtype) * g


def axial_rope(x):
    n, d = x.shape[1], x.shape[-1]
    half = d // 2
    quarter = half // 2
    t = jnp.arange(n, dtype=jnp.int32)
    freqs = ROPE_THETA ** (-jnp.arange(quarter, dtype=jnp.float32) / quarter)
    outs = []
    for pos, xs in ((t // GRID_W, x[..., :half]), (t % GRID_W, x[..., half:])):
        ang = pos.astype(jnp.float32)[:, None] * freqs[None, :]
        cos = jnp.cos(ang)[None, :, None, :].astype(x.dtype)
        sin = jnp.sin(ang)[None, :, None, :].astype(x.dtype)
        x1, x2 = xs[..., :quarter], xs[..., quarter:]
        outs.append(jnp.concatenate([x1 * cos - x2 * sin, x2 * cos + x1 * sin], axis=-1))
    return jnp.concatenate(outs, axis=-1)


def blocked_attention(q, k, v):
    b, lq, hkv, g, dk = q.shape
    dv = v.shape[-1]
    scale = dk ** -0.5
    qb = jnp.moveaxis(q.reshape(b, lq // Q_BLOCK, Q_BLOCK, hkv, g, dk), 1, 0)

    def one_block(q_blk):
        s = jnp.einsum('bqhgd,bkhd->bhgqk', q_blk, k).astype(jnp.float32) * scale
        p = jax.nn.softmax(s, axis=-1).astype(v.dtype)
        return jnp.einsum('bhgqk,bkhd->bqhgd', p, v)

    o = lax.map(one_block, qb)
    return jnp.moveaxis(o, 0, 1).reshape(b, lq, hkv * g * dv)


def neighbourhood_attention(q, k, v, k_ctx, v_ctx, rpb):
    b, n, h, d = q.shape
    rows = n // GRID_W
    kh = min(NA_KH, rows)
    scale = d ** -0.5
    cols = np.arange(GRID_W)
    col_start = np.clip(cols - NA_KW // 2, 0, GRID_W - NA_KW)
    kw_idx = col_start[:, None] + np.arange(NA_KW)[None, :]
    dc_idx = kw_idx - cols[:, None] + (NA_KW - 1)
    col_bias = rpb[:, :, dc_idx]
    qg = jnp.moveaxis(q.reshape(b, rows, GRID_W, h, d), 1, 0)
    kg = k.reshape(b, rows, GRID_W, h, d)
    vg = v.reshape(b, rows, GRID_W, h, d)

    def one_row(args):
        r, q_row = args
        rs = jnp.clip(r - kh // 2, 0, rows - kh)
        k_blk = lax.dynamic_slice_in_dim(kg, rs, kh, axis=1)[:, :, kw_idx]
        v_blk = lax.dynamic_slice_in_dim(vg, rs, kh, axis=1)[:, :, kw_idx]
        bias = jnp.take(col_bias, rs - r + jnp.arange(kh) + (NA_KH - 1), axis=1)
        s_nb = (jnp.einsum('bqhd,bjqkhd->bhqjk', q_row, k_blk).astype(jnp.float32) * scale
                + jnp.transpose(bias, (0, 2, 1, 3))[None].astype(jnp.float32))
        s_cx = jnp.einsum('bqhd,bchd->bhqc', q_row, k_ctx).astype(jnp.float32) * scale
        s = jnp.concatenate([s_nb.reshape(b, h, GRID_W, kh * NA_KW), s_cx], axis=-1)
        p = jax.nn.softmax(s, axis=-1).astype(v.dtype)
        p_nb = p[..., :kh * NA_KW].reshape(b, h, GRID_W, kh, NA_KW)
        return (jnp.einsum('bhqjk,bjqkhd->bqhd', p_nb, v_blk)
                + jnp.einsum('bhqc,bchd->bqhd', p[..., kh * NA_KW:], v_ctx))

    o = lax.map(one_row, (jnp.arange(rows, dtype=jnp.int32), qg))
    return jnp.moveaxis(o, 0, 1).reshape(b, n, h * d)


def mixer_projections(h, w_in, mla_g_q, mla_w_uq, mla_g_kv, gqa_g_q, gqa_g_k):
    b, n, _ = h.shape
    idx = [int(i) for i in np.cumsum(IN_SPLITS)[:-1]]
    q_na, k_na, v_na, cq, ckv, k_rope, q_g, k_g, v_g = jnp.split(h @ w_in, idx, axis=-1)
    q_na = q_na.reshape(b, n, NA_HEADS, HEAD_DIM)
    k_na = k_na.reshape(b, n, NA_HEADS, HEAD_DIM)
    v_na = v_na.reshape(b, n, NA_HEADS, HEAD_DIM)
    q_mla = (rms_norm(cq, mla_g_q) @ mla_w_uq).reshape(b, n, MLA_HEADS, MLA_NOPE + MLA_ROPE)
    ckv = rms_norm(ckv, mla_g_kv)
    q_g = rms_norm(q_g.reshape(b, n, GQA_HEADS, HEAD_DIM), gqa_g_q)
    k_g = rms_norm(k_g.reshape(b, n, GQA_KV_HEADS, HEAD_DIM), gqa_g_k)
    v_g = v_g.reshape(b, n, GQA_KV_HEADS, HEAD_DIM)
    return q_na, k_na, v_na, q_mla, ckv, k_rope, q_g, k_g, v_g


def mla_expand(ckv, k_rope, w_ukv):
    b, l, _ = ckv.shape
    kv = (ckv @ w_ukv).reshape(b, l, MLA_HEADS, MLA_NOPE + MLA_V)
    k_nope, v = kv[..., :MLA_NOPE], kv[..., MLA_NOPE:]
    k = jnp.concatenate([k_nope, jnp.broadcast_to(k_rope[:, :, None, :], (b, l, MLA_HEADS, MLA_ROPE))], axis=-1)
    return k, v


def conv_ffn(h, w_up, conv_w, conv_b, w_down):
    u = h @ w_up
    up = jnp.pad(u, ((0, 0), (1, 1), (0, 0)))
    u = up[:, :-2] * conv_w[0] + up[:, 1:-1] * conv_w[1] + up[:, 2:] * conv_w[2] + conv_b
    gate, val = jnp.split(u, 2, axis=-1)
    return (jax.nn.silu(gate) * val) @ w_down


def layer(x, cond, w, ctx=None):
    (w_ada, b_ada, g_pre_mix, g_post_mix, g_pre_ffn, g_post_ffn, w_in, na_rpb,
     mla_g_q, mla_w_uq, mla_g_kv, mla_w_ukv, gqa_g_q, gqa_g_k, w_out,
     ffn_w_up, ffn_conv_w, ffn_conv_b, ffn_w_down) = w
    b, n, _ = x.shape
    mods = jax.nn.silu(cond) @ w_ada + b_ada
    sh_m, sc_m, gt_m, sh_f, sc_f, gt_f = [m[:, None, :] for m in jnp.split(mods, 6, axis=-1)]
    h = rms_norm(x, g_pre_mix) * (1 + sc_m) + sh_m
    q_na, k_na, v_na, q_mla, ckv, k_rope, q_g, k_g, v_g = mixer_projections(
        h, w_in, mla_g_q, mla_w_uq, mla_g_kv, gqa_g_q, gqa_g_k)
    if ctx is None:
        o_na = blocked_attention(q_na[:, :, :, None], k_na, v_na)
        k_m, v_m = mla_expand(ckv, k_rope, mla_w_ukv)
        o_mla = blocked_attention(q_mla[:, :, :, None], k_m, v_m)
        o_gqa = blocked_attention(q_g.reshape(b, n, GQA_KV_HEADS, GQA_GROUP, HEAD_DIM), k_g, v_g)
        new_ctx = (k_na, v_na, ckv, k_rope, k_g, v_g)
    else:
        ctx_na_k, ctx_na_v, ctx_ckv, ctx_krope, ctx_gqa_k, ctx_gqa_v = ctx
        o_na = neighbourhood_attention(q_na, k_na, v_na, ctx_na_k, ctx_na_v, na_rpb)
        q_mla = jnp.concatenate([q_mla[..., :MLA_NOPE], axial_rope(q_mla[..., MLA_NOPE:])], axis=-1)
        k_lat, v_lat = mla_expand(ckv, axial_rope(k_rope[:, :, None, :])[:, :, 0, :], mla_w_ukv)
        k_cx, v_cx = mla_expand(ctx_ckv, ctx_krope, mla_w_ukv)
        o_mla = blocked_attention(q_mla[:, :, :, None],
                                  jnp.concatenate([k_cx, k_lat], axis=1),
                                  jnp.concatenate([v_cx, v_lat], axis=1))
        q_r = axial_rope(q_g).reshape(b, n, GQA_KV_HEADS, GQA_GROUP, HEAD_DIM)
        o_gqa = blocked_attention(q_r,
                                  jnp.concatenate([ctx_gqa_k, axial_rope(k_g)], axis=1),
                                  jnp.concatenate([ctx_gqa_v, v_g], axis=1))
        new_ctx = None
    o = jnp.concatenate([o_na, o_mla, o_gqa], axis=-1) @ w_out
    x = x + gt_m * rms_norm(o, g_post_mix)
    h = rms_norm(x, g_pre_ffn) * (1 + sc_f) + sh_f
    x = x + gt_f * rms_norm(conv_ffn(h, ffn_w_up, ffn_conv_w, ffn_conv_b, ffn_w_down), g_post_ffn)
    return x, new_ctx


def setup_inputs(seed: int = 0) -> dict:
    key = jax.random.key(seed)
    sub = jax.random.split(key, 40)
    ks = iter([sub[i] for i in range(40)])

    def nrm(shape, s):
        return jax.random.normal(next(ks), shape, jnp.float32) * s

    def gain(shape):
        return 1.0 + nrm(shape, 0.05)

    L, D = DEPTH, D_MODEL
    return {
        'x_prompt': nrm((BATCH, SEQ, D), 1.0),
        'x_sample': nrm((DEC_BATCH, DEC_SEQ, D), 1.0),
        'c': nrm((DEC_BATCH, D), 1.0),
        'cache_na_k': nrm((DEC_BATCH, L, PAST_LEN, NA_HEADS, HEAD_DIM), 1.0),
        'cache_na_v': nrm((DEC_BATCH, L, PAST_LEN, NA_HEADS, HEAD_DIM), 1.0),
        'cache_mla_ckv': nrm((DEC_BATCH, L, PAST_LEN, MLA_KV_LORA), 1.0),
        'cache_mla_krope': nrm((DEC_BATCH, L, PAST_LEN, MLA_ROPE), 1.0),
        'cache_gqa_k': nrm((DEC_BATCH, L, PAST_LEN, GQA_KV_HEADS, HEAD_DIM), 1.0),
        'cache_gqa_v': nrm((DEC_BATCH, L, PAST_LEN, GQA_KV_HEADS, HEAD_DIM), 1.0),
        'c_ctx': nrm((D,), 1.0),
        'w_ada': nrm((L, D, 6 * D), 0.5 * D ** -0.5),
        'b_ada': nrm((L, 6 * D), 0.02),
        'g_pre_mix': gain((L, D)),
        'g_post_mix': gain((L, D)),
        'g_pre_ffn': gain((L, D)),
        'g_post_ffn': gain((L, D)),
        'w_in': nrm((L, D, IN_COLS), D ** -0.5),
        'na_rpb': nrm((L, NA_HEADS, 2 * NA_KH - 1, 2 * NA_KW - 1), 0.1),
        'mla_g_q': gain((L, MLA_Q_LORA)),
        'mla_w_uq': nrm((L, MLA_Q_LORA, MLA_HEADS * (MLA_NOPE + MLA_ROPE)), MLA_Q_LORA ** -0.5),
        'mla_g_kv': gain((L, MLA_KV_LORA)),
        'mla_w_ukv': nrm((L, MLA_KV_LORA, MLA_HEADS * (MLA_NOPE + MLA_V)), MLA_KV_LORA ** -0.5),
        'gqa_g_q': gain((L, HEAD_DIM)),
        'gqa_g_k': gain((L, HEAD_DIM)),
        'w_out': nrm((L, MIX_WIDTH, D), MIX_WIDTH ** -0.5),
        'ffn_w_up': nrm((L, D, 2 * D_FF), D ** -0.5),
        'ffn_conv_w': nrm((L, CONV_W, 2 * D_FF), CONV_W ** -0.5),
        'ffn_conv_b': nrm((L, 2 * D_FF), 0.02),
        'ffn_w_down': nrm((L, D_FF, D), D_FF ** -0.5),
    }


def reference(x_prompt, x_sample, c, cache_na_k, cache_na_v, cache_mla_ckv, cache_mla_krope,
              cache_gqa_k, cache_gqa_v, c_ctx, w_ada, b_ada, g_pre_mix, g_post_mix, g_pre_ffn,
              g_post_ffn, w_in, na_rpb, mla_g_q, mla_w_uq, mla_g_kv, mla_w_ukv, gqa_g_q, gqa_g_k,
              w_out, ffn_w_up, ffn_conv_w, ffn_conv_b, ffn_w_down):
    cond_ctx = c_ctx[None, :]
    xp, xs = x_prompt, x_sample
    na_k, na_v, mla_ckv, mla_krope, gqa_k, gqa_v = [], [], [], [], [], []
    for l in range(DEPTH):
        w = (w_ada[l], b_ada[l], g_pre_mix[l], g_post_mix[l], g_pre_ffn[l], g_post_ffn[l],
             w_in[l], na_rpb[l], mla_g_q[l], mla_w_uq[l], mla_g_kv[l], mla_w_ukv[l],
             gqa_g_q[l], gqa_g_k[l], w_out[l], ffn_w_up[l], ffn_conv_w[l], ffn_conv_b[l],
             ffn_w_down[l])
        xp, (k1, v1, ck, kr, k2, v2) = layer(xp, cond_ctx, w)
        na_k.append(k1)
        na_v.append(v1)
        mla_ckv.append(ck)
        mla_krope.append(kr)
        gqa_k.append(k2)
        gqa_v.append(v2)
        ctx_l = (cache_na_k[:, l], cache_na_v[:, l], cache_mla_ckv[:, l], cache_mla_krope[:, l],
                 cache_gqa_k[:, l], cache_gqa_v[:, l])
        xs, _ = layer(xs, c, w, ctx_l)
    y_prompt, y_sample = xp, xs
    new_na_k = jnp.stack(na_k, axis=1)
    new_na_v = jnp.stack(na_v, axis=1)
    new_mla_ckv = jnp.stack(mla_ckv, axis=1)
    new_mla_krope = jnp.stack(mla_krope, axis=1)
    new_gqa_k = jnp.stack(gqa_k, axis=1)
    new_gqa_v = jnp.stack(gqa_v, axis=1)
    return (y_prompt, y_sample, new_na_k, new_na_v, new_mla_ckv, new_mla_krope, new_gqa_k, new_gqa_v)
```

```python
import functools

import numpy as np
import jax
import jax.numpy as jnp
from jax import lax
from jax.experimental import pallas as pl
from jax.experimental.pallas import tpu as pltpu

GRID_W = 64
HEAD_DIM = 64
NA_HEADS = 6
NA_KH = 8
NA_KW = 16
MLA_HEADS = 4
MLA_Q_LORA = 256
MLA_KV_LORA = 128
MLA_NOPE = 64
MLA_ROPE = 32
MLA_V = 64
GQA_HEADS = 6
GQA_KV_HEADS = 2
GQA_GROUP = GQA_HEADS // GQA_KV_HEADS
ROPE_THETA = 10000.0
EPS = 1e-6

NA_WIDTH = NA_HEADS * HEAD_DIM
MLA_WIDTH = MLA_HEADS * MLA_V
GQA_WIDTH = GQA_HEADS * HEAD_DIM
DENSE_WIDTH = MLA_WIDTH + GQA_WIDTH

LANES = 128
VMEM_LIMIT = 52 * 1024 * 1024
MASK_VALUE = -1e30

PROJ_TILE = 512
POST_TILE = 512
FFN_TILE = 512
FF_CHUNK = 256
DENSE_Q_TILE = 256
DENSE_K_TILE = 512
NA_ROWS = 4
NA_KEY_ROWS = NA_ROWS + NA_KH

C_QNA = 0
C_KNA = C_QNA + NA_WIDTH
C_VNA = C_KNA + NA_WIDTH
C_CQ = C_VNA + NA_WIDTH
C_CKV = C_CQ + MLA_Q_LORA
C_QG = C_CKV + MLA_KV_LORA
C_KG = C_QG + GQA_WIDTH
C_VG = C_KG + LANES
C_KR = C_VG + LANES
C_END_CTX = C_KR + LANES
C_KRS = C_END_CTX
C_QGS = C_KRS + LANES
C_KGS = C_QGS + GQA_WIDTH
C_END_LAT = C_KGS + LANES

GQA_ORDER = (0, 3, 1, 4, 2, 5)


def _dot(a, b):
    return jnp.dot(a, b, preferred_element_type=jnp.float32)


def _dot_nt(a, b):
    return lax.dot_general(a, b, (((1,), (1,)), ((), ())), preferred_element_type=jnp.float32)


def _params(*sem):
    return pltpu.CompilerParams(dimension_semantics=sem, vmem_limit_bytes=VMEM_LIMIT)


def _rms(x, g):
    return x * lax.rsqrt(jnp.mean(x * x, axis=-1, keepdims=True) + EPS) * g


def _head_mean_sq(x, ones_bf16):
    xx = x * x
    hi = xx.astype(jnp.bfloat16)
    lo = (xx - hi.astype(jnp.float32)).astype(jnp.bfloat16)
    return _dot(hi, ones_bf16) + _dot(lo, ones_bf16)


def _ada_kernel(cond_ref, w_ref, b_ref, o_ref):
    cnd = cond_ref[...]
    s = cnd / (1.0 + jnp.exp(-cnd))
    o_ref[0] = jnp.dot(s, w_ref[0], preferred_element_type=jnp.float32,
                       precision=lax.Precision.HIGHEST) + b_ref[0]


def _ada_mods(cond, w_ada, b_ada):
    nl, d, n6 = w_ada.shape
    tn = 1536
    return pl.pallas_call(
        _ada_kernel,
        grid=(nl, n6 // tn),
        in_specs=[pl.BlockSpec((8, d), lambda l, j: (0, 0)),
                  pl.BlockSpec((1, d, tn), lambda l, j: (l, 0, j)),
                  pl.BlockSpec((1, 1, tn), lambda l, j: (l, 0, j))],
        out_specs=pl.BlockSpec((1, 8, tn), lambda l, j: (l, 0, j)),
        out_shape=jax.ShapeDtypeStruct((nl, 8, n6), jnp.float32),
        compiler_params=_params("arbitrary", "arbitrary"),
        name="ada_mods",
    )(cond, w_ada, b_ada.reshape(nl, 1, n6))


def _proj_kernel(*refs, latent):
    if latent:
        (x_ref, mod_ref, gpre_ref, w_ref, gq_ref, wuq1_ref, wuq2_ref, gkv_ref, wuk_ref, wuv_ref,
         ones3_ref, ggq_ref, ggk_ref, ggqs_ref, ggks_ref, cg_ref, sg_ref, cm_ref, sm_ref,
         qna_ref, kna_ref, vna_ref, qm_ref, km_ref, vm_ref, qg_ref, kg_ref, vg_ref) = refs
    else:
        (x_ref, mod_ref, gpre_ref, w_ref, gq_ref, wuq1_ref, gkv_ref, wuk_ref, wuv_ref,
         ones3_ref, ggq_ref, ggk_ref,
         qna_ref, kna_ref, vna_ref, qm_ref, ckv_ref, kr_ref, km_ref, vm_ref, qg_ref, kg_ref, vg_ref) = refs

    x = x_ref[...]
    mod = mod_ref[0]
    sh, sc = mod[0:1, :], mod[1:2, :]
    h = (_rms(x, gpre_ref[...]) * (1.0 + sc) + sh).astype(jnp.bfloat16)

    def proj(c0, width):
        return _dot(h, w_ref[:, c0:c0 + width])

    na_scale = HEAD_DIM ** -0.5
    qna_ref[...] = (proj(C_QNA, NA_WIDTH) * na_scale).astype(qna_ref.dtype)
    kna_ref[...] = proj(C_KNA, NA_WIDTH).astype(kna_ref.dtype)
    vna_ref[...] = proj(C_VNA, NA_WIDTH).astype(vna_ref.dtype)

    mla_scale = (MLA_NOPE + MLA_ROPE) ** -0.5
    cqn = _rms(proj(C_CQ, MLA_Q_LORA), gq_ref[...]).astype(jnp.bfloat16)
    qm = _dot(cqn, wuq1_ref[...])
    kr = proj(C_KR, LANES)
    if latent:
        cm = jnp.concatenate([cm_ref[...]] * MLA_HEADS, axis=1)
        sm = jnp.concatenate([sm_ref[...]] * MLA_HEADS, axis=1)
        qm = qm * cm + _dot(cqn, wuq2_ref[...]) * sm
        kr = kr * cm_ref[...] + proj(C_KRS, LANES) * sm_ref[...]
    qm_ref[...] = (qm * mla_scale).astype(qm_ref.dtype)
    ckv = _rms(proj(C_CKV, MLA_KV_LORA), gkv_ref[...])
    ckv_b = ckv.astype(jnp.bfloat16)
    kn = _dot(ckv_b, wuk_ref[...])
    km_ref[...] = (kn + jnp.concatenate([kr] * MLA_HEADS, axis=1)).astype(km_ref.dtype)
    vm_ref[...] = _dot(ckv_b, wuv_ref[...]).astype(vm_ref.dtype)
    if not latent:
        ckv_ref[...] = ckv
        kr_ref[...] = kr[:, MLA_NOPE:MLA_NOPE + MLA_ROPE]

    ones3 = ones3_ref[...]
    ones1 = ones3[0:LANES, 0:LANES]
    qg = proj(C_QG, GQA_WIDTH)
    rq = lax.rsqrt(_head_mean_sq(qg, ones3) + EPS)
    qg = qg * rq * ggq_ref[...]
    kg = proj(C_KG, LANES)
    rk = lax.rsqrt(_head_mean_sq(kg, ones1) + EPS)
    kg = kg * rk * ggk_ref[...]
    if latent:
        cg, sg = cg_ref[...], sg_ref[...]
        cg3 = jnp.concatenate([cg] * (GQA_WIDTH // LANES), axis=1)
        sg3 = jnp.concatenate([sg] * (GQA_WIDTH // LANES), axis=1)
        qg = qg * cg3 + (proj(C_QGS, GQA_WIDTH) * rq * ggqs_ref[...]) * sg3
        kg = kg * cg + (proj(C_KGS, LANES) * rk * ggks_ref[...]) * sg
    qg_ref[...] = (qg * na_scale).astype(qg_ref.dtype)
    kg_ref[...] = kg.astype(kg_ref.dtype)
    vg_ref[...] = proj(C_VG, LANES).astype(vg_ref.dtype)


def _proj(x, mods, lw, rope, *, latent, seq):
    ntok, d = x.shape
    t = PROJ_TILE
    nt = ntok // t
    tiles_per_seq = seq // t if latent else 1
    tok = lambda w: pl.BlockSpec((t, w), lambda i: (i, 0))
    full = lambda a: pl.BlockSpec(a.shape, lambda i: (0,) * a.ndim)
    if latent:
        mod_spec = pl.BlockSpec((1, 6, d), lambda i: (i // tiles_per_seq, 0, 0))
    else:
        mod_spec = pl.BlockSpec((1, 6, d), lambda i: (0, 0, 0))
    kv_dt = jnp.bfloat16 if latent else jnp.float32
    bf = jnp.bfloat16
    sds = lambda w, dt: jax.ShapeDtypeStruct((ntok, w), dt)
    if latent:
        rope_spec = pl.BlockSpec((t, LANES), lambda i: (i % tiles_per_seq, 0))
        ins = [x, mods, lw["g_pre_mix"], lw["w_in_lat"], lw["mla_g_q"], lw["w_uq1"], lw["w_uq2"], lw["mla_g_kv"],
               lw["w_uk"], lw["w_uv"], lw["ones3"], lw["ggq"], lw["ggk"], lw["ggqs"], lw["ggks"],
               rope["cg"], rope["sg"], rope["cm"], rope["sm"]]
        in_specs = [tok(d), mod_spec] + [full(a) for a in ins[2:15]] + [rope_spec] * 4
        outs = [sds(NA_WIDTH, bf), sds(NA_WIDTH, kv_dt), sds(NA_WIDTH, kv_dt), sds(MLA_HEADS * LANES, bf),
                sds(MLA_HEADS * LANES, bf), sds(MLA_WIDTH, bf), sds(GQA_WIDTH, bf), sds(LANES, kv_dt), sds(LANES, kv_dt)]
    else:
        ins = [x, mods, lw["g_pre_mix"], lw["w_in_ctx"], lw["mla_g_q"], lw["w_uq1"], lw["mla_g_kv"],
               lw["w_uk"], lw["w_uv"], lw["ones3"], lw["ggq"], lw["ggk"]]
        in_specs = [tok(d), mod_spec] + [full(a) for a in ins[2:]]
        outs = [sds(NA_WIDTH, bf), sds(NA_WIDTH, kv_dt), sds(NA_WIDTH, kv_dt), sds(MLA_HEADS * LANES, bf),
                sds(MLA_KV_LORA, jnp.float32), sds(MLA_ROPE, jnp.float32),
                sds(MLA_HEADS * LANES, bf), sds(MLA_WIDTH, bf), sds(GQA_WIDTH, bf), sds(LANES, kv_dt), sds(LANES, kv_dt)]
    out_specs = [tok(o.shape[1]) for o in outs]
    return pl.pallas_call(
        functools.partial(_proj_kernel, latent=latent),
        grid=(nt,),
        in_specs=in_specs,
        out_specs=out_specs,
        out_shape=outs,
        compiler_params=_params("arbitrary"),
        name="proj_lat" if latent else "proj_ctx",
    )(*ins)


def _lane_lo():
    return lax.broadcasted_iota(jnp.int32, (1, LANES), 1) < HEAD_DIM


def _split_heads(q):
    lo = _lane_lo()
    zero = jnp.zeros_like(q)
    return jnp.where(lo, q, zero), jnp.where(lo, zero, q)


def _softmax_pv(scores, values):
    m = scores[0].max(axis=-1, keepdims=True)
    for s in scores[1:]:
        m = jnp.maximum(m, s.max(axis=-1, keepdims=True))
    l = None
    acc = None
    for s, v in zip(scores, values):
        p = jnp.exp(s - m)
        ps = p.sum(axis=-1, keepdims=True)
        pv = _dot(p.astype(jnp.bfloat16), v)
        l = ps if l is None else l + ps
        acc = pv if acc is None else acc + pv
    return acc / l


def _ctx_attn_kernel(qna_ref, kna_ref, vna_ref, qm_ref, km_ref, vm_ref, qg_ref, kg_ref, vg_ref, ona_ref, odn_ref):
    lo = _lane_lo()
    bf = jnp.bfloat16

    def pair(q_lo, q_hi, k_lo, k_hi, v):
        o_lo = _softmax_pv([_dot_nt(q_lo, k_lo)], [v])
        o_hi = _softmax_pv([_dot_nt(q_hi, k_hi)], [v])
        return jnp.where(lo, o_lo, o_hi)

    for p in range(NA_WIDTH // LANES):
        cs = slice(p * LANES, (p + 1) * LANES)
        q_lo, q_hi = _split_heads(qna_ref[0, :, cs])
        k = kna_ref[0, :, cs].astype(bf)
        ona_ref[0, :, cs] = pair(q_lo, q_hi, k, k, vna_ref[0, :, cs].astype(bf)).astype(ona_ref.dtype)
    for p in range(MLA_HEADS // 2):
        c0 = 2 * p * LANES
        o = pair(qm_ref[0, :, c0:c0 + LANES], qm_ref[0, :, c0 + LANES:c0 + 2 * LANES],
                 km_ref[0, :, c0:c0 + LANES], km_ref[0, :, c0 + LANES:c0 + 2 * LANES],
                 vm_ref[0, :, p * LANES:(p + 1) * LANES])
        odn_ref[0, :, p * LANES:(p + 1) * LANES] = o.astype(odn_ref.dtype)
    kg = kg_ref[0].astype(bf)
    vg = vg_ref[0].astype(bf)
    for c in range(GQA_WIDTH // LANES):
        q_lo, q_hi = _split_heads(qg_ref[0, :, c * LANES:(c + 1) * LANES])
        o = pair(q_lo, q_hi, kg, kg, vg)
        odn_ref[0, :, MLA_WIDTH + c * LANES:MLA_WIDTH + (c + 1) * LANES] = o.astype(odn_ref.dtype)


def _ctx_attn(qna, kna, vna, qm, km, vm, qg, kg, vg, *, batch, seq):
    ins = [a.reshape(batch, seq, a.shape[-1]) for a in (qna, kna, vna, qm, km, vm, qg, kg, vg)]
    spec = lambda a: pl.BlockSpec((1, seq, a.shape[-1]), lambda b: (b, 0, 0))
    outs = [jax.ShapeDtypeStruct((batch, seq, NA_WIDTH), jnp.bfloat16),
            jax.ShapeDtypeStruct((batch, seq, DENSE_WIDTH), jnp.bfloat16)]
    ona, odn = pl.pallas_call(
        _ctx_attn_kernel,
        grid=(batch,),
        in_specs=[spec(a) for a in ins],
        out_specs=[spec(o) for o in outs],
        out_shape=outs,
        compiler_params=_params("arbitrary"),
        name="ctx_attn",
    )(*ins)
    return ona.reshape(batch * seq, NA_WIDTH), odn.reshape(batch * seq, DENSE_WIDTH)


def _na_attn_kernel(q_ref, k_ref, v_ref, kc_ref, vc_ref, bias_ref, o_ref, *, rows):
    lo = _lane_lo()
    blk = pl.program_id(1)
    key_row0 = jnp.clip(blk * NA_ROWS - NA_KH // 2, 0, rows - NA_KEY_ROWS)
    start = pl.multiple_of(key_row0 * GRID_W, GRID_W)
    nkeys = NA_KEY_ROWS * GRID_W
    for p in range(NA_WIDTH // LANES):
        cs = slice(p * LANES, (p + 1) * LANES)
        q_lo, q_hi = _split_heads(q_ref[0, :, cs])
        k = k_ref[0, pl.ds(start, nkeys), cs]
        v = v_ref[0, pl.ds(start, nkeys), cs]
        kc = kc_ref[0, 0, :, cs]
        vc = vc_ref[0, 0, :, cs]
        o_lo = _softmax_pv([_dot_nt(q_lo, k) + bias_ref[0, 2 * p], _dot_nt(q_lo, kc)], [v, vc])
        o_hi = _softmax_pv([_dot_nt(q_hi, k) + bias_ref[0, 2 * p + 1], _dot_nt(q_hi, kc)], [v, vc])
        o_ref[0, :, cs] = jnp.where(lo, o_lo, o_hi).astype(o_ref.dtype)


def _na_attn(q, k, v, kc, vc, bias, layer, *, batch, seq):
    rows = seq // GRID_W
    nblk = rows // NA_ROWS
    tq = NA_ROWS * GRID_W
    past = kc.shape[2]
    q3, k3, v3 = (a.reshape(batch, seq, NA_WIDTH) for a in (q, k, v))
    o = pl.pallas_call(
        functools.partial(_na_attn_kernel, rows=rows),
        grid=(batch, nblk),
        in_specs=[pl.BlockSpec((1, tq, NA_WIDTH), lambda b, i: (b, i, 0)),
                  pl.BlockSpec((1, seq, NA_WIDTH), lambda b, i: (b, 0, 0)),
                  pl.BlockSpec((1, seq, NA_WIDTH), lambda b, i: (b, 0, 0)),
                  pl.BlockSpec((1, 1, past, NA_WIDTH), lambda b, i: (b, layer, 0, 0)),
                  pl.BlockSpec((1, 1, past, NA_WIDTH), lambda b, i: (b, layer, 0, 0)),
                  pl.BlockSpec((1,) + bias.shape[1:],
                               lambda b, i: (jnp.minimum(i, 1) + (i == nblk - 1).astype(jnp.int32), 0, 0, 0))],
        out_specs=pl.BlockSpec((1, tq, NA_WIDTH), lambda b, i: (b, i, 0)),
        out_shape=jax.ShapeDtypeStruct((batch, seq, NA_WIDTH), jnp.bfloat16),
        compiler_params=_params("arbitrary", "arbitrary"),
        name="na_attn",
    )(q3, k3, v3, kc, vc, bias)
    return o.reshape(batch * seq, NA_WIDTH)


def _na_bias_index(rows):
    nflat = (2 * NA_KH - 1) * (2 * NA_KW - 1)
    pats = []
    for r0 in (0, NA_ROWS, rows - NA_ROWS):
        key_row0 = int(np.clip(r0 - NA_KH // 2, 0, rows - NA_KEY_ROWS))
        rq = r0 + np.arange(NA_ROWS)[:, None, None, None]
        cq = np.arange(GRID_W)[None, :, None, None]
        rk = key_row0 + np.arange(NA_KEY_ROWS)[None, None, :, None]
        ck = np.arange(GRID_W)[None, None, None, :]
        rs = np.clip(rq - NA_KH // 2, 0, rows - NA_KH)
        cs = np.clip(cq - NA_KW // 2, 0, GRID_W - NA_KW)
        valid = (rk >= rs) & (rk < rs + NA_KH) & (ck >= cs) & (ck < cs + NA_KW)
        idx = (rk - rq + NA_KH - 1) * (2 * NA_KW - 1) + (ck - cq + NA_KW - 1)
        idx = np.where(valid, idx, nflat)
        pats.append(idx.reshape(NA_ROWS * GRID_W, NA_KEY_ROWS * GRID_W))
    return np.stack(pats).astype(np.int32)


def _dense_attn_kernel(qm_ref, qg_ref, kmc_ref, vmc_ref, kgc_ref, vgc_ref, kml_ref, vml_ref, kgl_ref, vgl_ref,
                       o_ref, *, nk, tk):
    lo = _lane_lo()
    bf = jnp.bfloat16

    def head_first(q, kc, vc):
        s = _dot_nt(q, kc)
        m = s.max(axis=-1, keepdims=True)
        p = jnp.exp(s - m)
        return m, p.sum(axis=-1, keepdims=True), _dot(p.astype(bf), vc)

    def head_step(q, k, v, m, l, acc):
        s = _dot_nt(q, k)
        m_new = jnp.maximum(m, s.max(axis=-1, keepdims=True))
        alpha = jnp.exp(m - m_new)
        p = jnp.exp(s - m_new)
        l = alpha * l + p.sum(axis=-1, keepdims=True)
        acc = alpha * acc + _dot(p.astype(bf), v)
        return m_new, l, acc

    def unit(q_lo, q_hi, kc_lo, kc_hi, vc, k_lo_at, k_hi_at, v_at):
        st_lo = head_first(q_lo, kc_lo, vc)
        st_hi = head_first(q_hi, kc_hi, vc)

        def body(j, carry):
            st_lo, st_hi = carry
            off = pl.multiple_of(j * tk, tk)
            v = v_at(off)
            return head_step(q_lo, k_lo_at(off), v, *st_lo), head_step(q_hi, k_hi_at(off), v, *st_hi)

        st_lo, st_hi = lax.fori_loop(0, nk, body, (st_lo, st_hi))
        return jnp.where(lo, st_lo[2] / st_lo[1], st_hi[2] / st_hi[1])

    for p in range(MLA_HEADS // 2):
        c0 = 2 * p * LANES
        c1 = c0 + LANES
        o = unit(qm_ref[0, :, c0:c1], qm_ref[0, :, c1:c1 + LANES],
                 kmc_ref[0, 0, :, c0:c1], kmc_ref[0, 0, :, c1:c1 + LANES], vmc_ref[0, 0, :, p * LANES:(p + 1) * LANES],
                 lambda off, c0=c0, c1=c1: kml_ref[0, pl.ds(off, tk), c0:c1],
                 lambda off, c1=c1: kml_ref[0, pl.ds(off, tk), c1:c1 + LANES],
                 lambda off, p=p: vml_ref[0, pl.ds(off, tk), p * LANES:(p + 1) * LANES])
        o_ref[0, :, p * LANES:(p + 1) * LANES] = o.astype(o_ref.dtype)
    kgc = kgc_ref[0, 0]
    vgc = vgc_ref[0, 0]
    kg_at = lambda off: kgl_ref[0, pl.ds(off, tk), :]
    vg_at = lambda off: vgl_ref[0, pl.ds(off, tk), :]
    for c in range(GQA_WIDTH // LANES):
        q_lo, q_hi = _split_heads(qg_ref[0, :, c * LANES:(c + 1) * LANES])
        o = unit(q_lo, q_hi, kgc, kgc, vgc, kg_at, kg_at, vg_at)
        o_ref[0, :, MLA_WIDTH + c * LANES:MLA_WIDTH + (c + 1) * LANES] = o.astype(o_ref.dtype)


def _dense_attn(qm, qg, kmc, vmc, kgc, vgc, kml, vml, kgl, vgl, layer, *, batch, seq):
    tq, tk = DENSE_Q_TILE, DENSE_K_TILE
    r3 = lambda a: a.reshape(batch, seq, a.shape[-1])
    qm, qg, kml, vml, kgl, vgl = (r3(a) for a in (qm, qg, kml, vml, kgl, vgl))
    qspec = lambda a: pl.BlockSpec((1, tq, a.shape[-1]), lambda b, i: (b, i, 0))
    cspec = lambda a: pl.BlockSpec((1, 1) + a.shape[2:], lambda b, i: (b, layer, 0, 0))
    lspec = lambda a: pl.BlockSpec((1, seq, a.shape[-1]), lambda b, i: (b, 0, 0))
    o = pl.pallas_call(
        functools.partial(_dense_attn_kernel, nk=seq // tk, tk=tk),
        grid=(batch, seq // tq),
        in_specs=[qspec(qm), qspec(qg), cspec(kmc), cspec(vmc), cspec(kgc), cspec(vgc),
                  lspec(kml), lspec(vml), lspec(kgl), lspec(vgl)],
        out_specs=pl.BlockSpec((1, tq, DENSE_WIDTH), lambda b, i: (b, i, 0)),
        out_shape=jax.ShapeDtypeStruct((batch, seq, DENSE_WIDTH), jnp.bfloat16),
        compiler_params=_params("arbitrary", "arbitrary"),
        name="dense_attn",
    )(qm, qg, kmc, vmc, kgc, vgc, kml, vml, kgl, vgl)
    return o.reshape(batch * seq, DENSE_WIDTH)


def _mla_ctx_kernel(ckv_ref, kr_ref, wuk_ref, wuv_ref, k_ref, v_ref):
    ckv = ckv_ref[0, 0]
    kn = _dot(ckv, wuk_ref[0])
    k_ref[0, 0] = (kn + jnp.concatenate([kr_ref[0, 0].astype(jnp.float32)] * MLA_HEADS, axis=1)).astype(k_ref.dtype)
    v_ref[0, 0] = _dot(ckv, wuv_ref[0]).astype(v_ref.dtype)


def _mla_ctx_expand(ckv, kr_pad, wuk, wuv):
    b, nl, past, _ = ckv.shape
    spec = lambda w: pl.BlockSpec((1, 1, past, w), lambda i, l: (i, l, 0, 0))
    wspec = lambda a: pl.BlockSpec((1,) + a.shape[1:], lambda i, l: (l, 0, 0))
    return pl.pallas_call(
        _mla_ctx_kernel,
        grid=(b, nl),
        in_specs=[spec(MLA_KV_LORA), spec(LANES), wspec(wuk), wspec(wuv)],
        out_specs=[spec(MLA_HEADS * LANES), spec(MLA_WIDTH)],
        out_shape=[jax.ShapeDtypeStruct((b, nl, past, MLA_HEADS * LANES), jnp.bfloat16),
                   jax.ShapeDtypeStruct((b, nl, past, MLA_WIDTH), jnp.bfloat16)],
        compiler_params=_params("arbitrary", "arbitrary"),
        name="mla_ctx_expand",
    )(ckv, kr_pad, wuk, wuv)


def _post_kernel(ona_ref, odn_ref, x_ref, mod_ref, wna_ref, wdn_ref, gpost_ref, gpre_ref, x1_ref, h2_ref):
    mod = mod_ref[0]
    gt_m, sh_f, sc_f = mod[2:3, :], mod[3:4, :], mod[4:5, :]
    o = _dot(ona_ref[...], wna_ref[...]) + _dot(odn_ref[...], wdn_ref[...])
    x1 = x_ref[...] + gt_m * _rms(o, gpost_ref[...])
    x1_ref[...] = x1
    h2_ref[...] = (_rms(x1, gpre_ref[...]) * (1.0 + sc_f) + sh_f).astype(h2_ref.dtype)


def _post(ona, odn, x, mods, lw, *, seq_per_mod):
    ntok, d = x.shape
    t = POST_TILE
    tok = lambda w: pl.BlockSpec((t, w), lambda i: (i, 0))
    full = lambda a: pl.BlockSpec(a.shape, lambda i: (0,) * a.ndim)
    if seq_per_mod is None:
        mod_spec = pl.BlockSpec((1, 6, d), lambda i: (0, 0, 0))
    else:
        tps = seq_per_mod // t
        mod_spec = pl.BlockSpec((1, 6, d), lambda i: (i // tps, 0, 0))
    ins = [ona, odn, x, mods, lw["w_out_na"], lw["w_out_dn"], lw["g_post_mix"], lw["g_pre_ffn"]]
    return pl.pallas_call(
        _post_kernel,
        grid=(ntok // t,),
        in_specs=[tok(NA_WIDTH), tok(DENSE_WIDTH), tok(d), mod_spec] + [full(a) for a in ins[4:]],
        out_specs=[tok(d), tok(d)],
        out_shape=[jax.ShapeDtypeStruct((ntok, d), jnp.float32), jax.ShapeDtypeStruct((ntok, d), jnp.bfloat16)],
        compiler_params=_params("arbitrary"),
        name="post",
    )(*ins)


def _ffn_kernel(h_ref, hp_ref, hn_ref, x1_ref, mod_ref, wg_ref, wv_ref, cwg_ref, cwv_ref, cbg_ref, cbv_ref,
                wd_ref, gpost_ref, o_ref, acc_ref, *, seq):
    i = pl.program_id(0)
    j = pl.program_id(1)
    t = h_ref.shape[0]
    halo = hp_ref.shape[0]

    @pl.when(j == 0)
    def _():
        acc_ref[...] = jnp.zeros_like(acc_ref)

    h = h_ref[...]
    row = lax.broadcasted_iota(jnp.int32, (t, 1), 0)
    pos = jnp.bitwise_and(i * t + row, seq - 1)
    first, last = row == 0, row == t - 1
    seq_first, seq_last = pos == 0, pos == seq - 1

    def conv(w_ref, cw_ref, cb_ref):
        w = w_ref[...]
        u = _dot(h, w)
        u_hp = _dot(hp_ref[...], w)[halo - 1:halo, :]
        u_hn = _dot(hn_ref[...], w)[0:1, :]
        prev = jnp.where(first, u_hp, pltpu.roll(u, 1, 0))
        prev = jnp.where(seq_first, 0.0, prev)
        nxt = jnp.where(last, u_hn, pltpu.roll(u, t - 1, 0))
        nxt = jnp.where(seq_last, 0.0, nxt)
        cw = cw_ref[...]
        return prev * cw[0:1, :] + u * cw[1:2, :] + nxt * cw[2:3, :] + cb_ref[...]

    gate = conv(wg_ref, cwg_ref, cbg_ref)
    val = conv(wv_ref, cwv_ref, cbv_ref)
    act = gate / (1.0 + jnp.exp(-gate)) * val
    acc_ref[...] += _dot(act.astype(jnp.bfloat16), wd_ref[...])

    @pl.when(j == pl.num_programs(1) - 1)
    def _():
        gt_f = mod_ref[0][5:6, :]
        o_ref[...] = x1_ref[...] + gt_f * _rms(acc_ref[...], gpost_ref[...])


def _ffn(h2, x1, mods, lw, *, seq, seq_per_mod):
    ntok, d = x1.shape
    t = FFN_TILE
    halo = 16
    c = FF_CHUNK
    dff = lw["w_down"].shape[0]
    nchunk = dff // c
    assert seq & (seq - 1) == 0 and (t % seq == 0 or seq % t == 0)
    nhalo = ntok // halo
    if seq_per_mod is None:
        mod_spec = pl.BlockSpec((1, 6, d), lambda i, j: (0, 0, 0))
    else:
        tps = seq_per_mod // t
        mod_spec = pl.BlockSpec((1, 6, d), lambda i, j: (i // tps, 0, 0))
    tok = pl.BlockSpec((t, d), lambda i, j: (i, 0))
    in_specs = [
        tok,
        pl.BlockSpec((halo, d), lambda i, j: (jnp.maximum(i * (t // halo) - 1, 0), 0)),
        pl.BlockSpec((halo, d), lambda i, j: (jnp.minimum((i + 1) * (t // halo), nhalo - 1), 0)),
        tok,
        mod_spec,
        pl.BlockSpec((d, c), lambda i, j: (0, j)),
        pl.BlockSpec((d, c), lambda i, j: (0, nchunk + j)),
        pl.BlockSpec((3, c), lambda i, j: (0, j)),
        pl.BlockSpec((3, c), lambda i, j: (0, nchunk + j)),
        pl.BlockSpec((1, c), lambda i, j: (0, j)),
        pl.BlockSpec((1, c), lambda i, j: (0, nchunk + j)),
        pl.BlockSpec((c, d), lambda i, j: (j, 0)),
        pl.BlockSpec((1, d), lambda i, j: (0, 0)),
    ]
    return pl.pallas_call(
        functools.partial(_ffn_kernel, seq=seq),
        grid=(ntok // t, nchunk),
        in_specs=in_specs,
        out_specs=tok,
        out_shape=jax.ShapeDtypeStruct((ntok, d), jnp.float32),
        scratch_shapes=[pltpu.VMEM((t, d), jnp.float32)],
        compiler_params=_params("arbitrary", "arbitrary"),
        name="ffn",
    )(h2, h2, h2, x1, mods, lw["w_up"], lw["w_up"], lw["conv_w"], lw["conv_w"], lw["conv_b"], lw["conv_b"],
      lw["w_down"], lw["g_post_ffn"])


def _swap_halves(n, group):
    idx = np.arange(n).reshape(-1, 2, group // 2)
    return idx[:, ::-1, :].reshape(-1)


def _prep_weights(w_in, mla_w_uq, mla_w_ukv, w_out, gqa_g_q, gqa_g_k):
    nl, d, in_cols = w_in.shape
    bf = jnp.bfloat16
    o_qna, o_kna, o_vna = 0, NA_WIDTH, 2 * NA_WIDTH
    o_cq = 3 * NA_WIDTH
    o_ckv = o_cq + MLA_Q_LORA
    o_kr = o_ckv + MLA_KV_LORA
    o_qg = o_kr + MLA_ROPE
    o_kg = o_qg + GQA_WIDTH
    o_vg = o_kg + GQA_KV_HEADS * HEAD_DIM
    zero = in_cols
    z = lambda n: np.full((n,), zero, np.int64)
    ar = np.arange
    sw64 = _swap_halves(HEAD_DIM, HEAD_DIM // 2)
    sw32 = _swap_halves(MLA_ROPE, MLA_ROPE // 2)
    qg_cols = np.concatenate([o_qg + HEAD_DIM * h + ar(HEAD_DIM) for h in GQA_ORDER])
    qgs_cols = np.concatenate([o_qg + HEAD_DIM * h + sw64 for h in GQA_ORDER])
    kg_cols = o_kg + ar(GQA_KV_HEADS * HEAD_DIM)
    kgs_cols = np.concatenate([o_kg + HEAD_DIM * h + sw64 for h in range(GQA_KV_HEADS)])
    pad_r = LANES - MLA_NOPE - MLA_ROPE
    cols = np.concatenate([
        o_qna + ar(NA_WIDTH), o_kna + ar(NA_WIDTH), o_vna + ar(NA_WIDTH),
        o_cq + ar(MLA_Q_LORA), o_ckv + ar(MLA_KV_LORA),
        qg_cols, kg_cols, o_vg + ar(GQA_KV_HEADS * HEAD_DIM),
        z(MLA_NOPE), o_kr + ar(MLA_ROPE), z(pad_r),
        z(MLA_NOPE), o_kr + sw32, z(pad_r),
        qgs_cols, kgs_cols])
    assert cols.shape[0] == C_END_LAT
    w_pad = jnp.concatenate([w_in, jnp.zeros((nl, d, 1), w_in.dtype)], axis=2)
    w_all = jnp.take(w_pad, jnp.asarray(cols, jnp.int32), axis=2).astype(bf)

    qw = MLA_NOPE + MLA_ROPE
    zq = mla_w_uq.shape[2]
    uq1 = np.concatenate([np.concatenate([qw * h + ar(qw), np.full((pad_r,), zq)]) for h in range(MLA_HEADS)])
    uq2 = np.concatenate([np.concatenate([np.full((MLA_NOPE,), zq), qw * h + MLA_NOPE + sw32, np.full((pad_r,), zq)])
                          for h in range(MLA_HEADS)])
    uq_pad = jnp.concatenate([mla_w_uq, jnp.zeros(mla_w_uq.shape[:2] + (1,), mla_w_uq.dtype)], axis=2)
    w_uq1 = jnp.take(uq_pad, jnp.asarray(uq1, jnp.int32), axis=2).astype(bf)
    w_uq2 = jnp.take(uq_pad, jnp.asarray(uq2, jnp.int32), axis=2).astype(bf)
    kvw = MLA_NOPE + MLA_V
    zk = mla_w_ukv.shape[2]
    uk = np.concatenate([np.concatenate([kvw * h + ar(MLA_NOPE), np.full((LANES - MLA_NOPE,), zk)])
                         for h in range(MLA_HEADS)])
    uv = np.concatenate([kvw * h + MLA_NOPE + ar(MLA_V) for h in range(MLA_HEADS)])
    ukv_pad = jnp.concatenate([mla_w_ukv, jnp.zeros(mla_w_ukv.shape[:2] + (1,), mla_w_ukv.dtype)], axis=2)
    w_uk = jnp.take(ukv_pad, jnp.asarray(uk, jnp.int32), axis=2).astype(bf)
    w_uv = jnp.take(ukv_pad, jnp.asarray(uv, jnp.int32), axis=2).astype(bf)

    o_gqa = NA_WIDTH + MLA_WIDTH
    rows_dn = np.concatenate([NA_WIDTH + ar(MLA_WIDTH)] + [o_gqa + HEAD_DIM * h + ar(HEAD_DIM) for h in GQA_ORDER])
    w_out_na = w_out[:, :NA_WIDTH, :].astype(bf)
    w_out_dn = jnp.take(w_out, jnp.asarray(rows_dn, jnp.int32), axis=1).astype(bf)

    ggq = jnp.tile(gqa_g_q, (1, GQA_HEADS))[:, None, :]
    ggk = jnp.tile(gqa_g_k, (1, GQA_KV_HEADS))[:, None, :]
    ggqs = jnp.tile(gqa_g_q[:, sw64], (1, GQA_HEADS))[:, None, :]
    ggks = jnp.tile(gqa_g_k[:, sw64], (1, GQA_KV_HEADS))[:, None, :]
    blk = np.kron(np.eye(GQA_WIDTH // HEAD_DIM), np.full((HEAD_DIM, HEAD_DIM), 1.0 / HEAD_DIM))
    ones3 = jnp.asarray(blk, bf)
    return dict(w_all=w_all, w_uq1=w_uq1, w_uq2=w_uq2, w_uk=w_uk, w_uv=w_uv, w_out_na=w_out_na, w_out_dn=w_out_dn,
                ggq=ggq, ggk=ggk, ggqs=ggqs, ggks=ggks, ones3=ones3)


def _rope_tables(seq):
    t = jnp.arange(seq, dtype=jnp.int32)
    r, c = (t // GRID_W).astype(jnp.float32), (t % GRID_W).astype(jnp.float32)

    def tables(dim):
        quarter = dim // 4
        freqs = ROPE_THETA ** (-jnp.arange(quarter, dtype=jnp.float32) / quarter)
        ar_, ac_ = r[:, None] * freqs[None, :], c[:, None] * freqs[None, :]
        cos = jnp.concatenate([jnp.cos(ar_), jnp.cos(ar_), jnp.cos(ac_), jnp.cos(ac_)], axis=1)
        sin = jnp.concatenate([-jnp.sin(ar_), jnp.sin(ar_), -jnp.sin(ac_), jnp.sin(ac_)], axis=1)
        return cos, sin

    c64, s64 = tables(HEAD_DIM)
    c32, s32 = tables(MLA_ROPE)
    pad = LANES - MLA_NOPE - MLA_ROPE
    cm = jnp.concatenate([jnp.ones((seq, MLA_NOPE)), c32, jnp.zeros((seq, pad))], axis=1)
    sm = jnp.concatenate([jnp.zeros((seq, MLA_NOPE)), s32, jnp.zeros((seq, pad))], axis=1)
    return dict(cg=jnp.tile(c64, (1, 2)), sg=jnp.tile(s64, (1, 2)), cm=cm.astype(jnp.float32), sm=sm.astype(jnp.float32))


def kernel(x_prompt, x_sample, c, cache_na_k, cache_na_v, cache_mla_ckv, cache_mla_krope, cache_gqa_k, cache_gqa_v, c_ctx, w_ada, b_ada, g_pre_mix, g_post_mix, g_pre_ffn, g_post_ffn, w_in, na_rpb, mla_g_q, mla_w_uq, mla_g_kv, mla_w_ukv, gqa_g_q, gqa_g_k, w_out, ffn_w_up, ffn_conv_w, ffn_conv_b, ffn_w_down):
    batch, seq, d = x_prompt.shape
    dbatch, dseq, _ = x_sample.shape
    nl = w_ada.shape[0]
    past = cache_na_k.shape[2]
    bf = jnp.bfloat16
    assert dbatch + 1 <= 8 and dseq % GRID_W == 0

    cond = jnp.zeros((8, d), jnp.float32).at[0].set(c_ctx).at[1:1 + dbatch].set(c)
    mods_all = _ada_mods(cond, w_ada, b_ada).reshape(nl, 8, 6, d)

    pw = _prep_weights(w_in, mla_w_uq, mla_w_ukv, w_out, gqa_g_q, gqa_g_k)
    rope = _rope_tables(dseq)
    w_up = ffn_w_up.astype(bf)
    w_down = ffn_w_down.astype(bf)

    kc_na = cache_na_k.reshape(dbatch, nl, past, NA_WIDTH).astype(bf)
    vc_na = cache_na_v.reshape(dbatch, nl, past, NA_WIDTH).astype(bf)
    kc_g = cache_gqa_k.reshape(dbatch, nl, past, GQA_KV_HEADS * HEAD_DIM).astype(bf)
    vc_g = cache_gqa_v.reshape(dbatch, nl, past, GQA_KV_HEADS * HEAD_DIM).astype(bf)
    kr_pad = jnp.pad(cache_mla_krope, ((0, 0), (0, 0), (0, 0), (MLA_NOPE, LANES - MLA_NOPE - MLA_ROPE))).astype(bf)
    kc_m, vc_m = _mla_ctx_expand(cache_mla_ckv.astype(bf), kr_pad, pw["w_uk"], pw["w_uv"])

    rows = dseq // GRID_W
    bias_idx = jnp.asarray(_na_bias_index(rows))
    rpb_flat = na_rpb.reshape(nl, NA_HEADS, -1)
    rpb_flat = jnp.concatenate([rpb_flat, jnp.full((nl, NA_HEADS, 1), MASK_VALUE, rpb_flat.dtype)], axis=2)

    xp = x_prompt.reshape(batch * seq, d)
    xs = x_sample.reshape(dbatch * dseq, d)
    caches = [[] for _ in range(6)]
    for l in range(nl):
        lw = dict(
            g_pre_mix=g_pre_mix[l][None], g_post_mix=g_post_mix[l][None], g_pre_ffn=g_pre_ffn[l][None],
            g_post_ffn=g_post_ffn[l][None], mla_g_q=mla_g_q[l][None], mla_g_kv=mla_g_kv[l][None],
            w_in_ctx=pw["w_all"][l, :, :C_END_CTX], w_in_lat=pw["w_all"][l],
            w_uq1=pw["w_uq1"][l], w_uq2=pw["w_uq2"][l], w_uk=pw["w_uk"][l], w_uv=pw["w_uv"][l],
            ones3=pw["ones3"], ggq=pw["ggq"][l], ggk=pw["ggk"][l], ggqs=pw["ggqs"][l], ggks=pw["ggks"][l],
            w_out_na=pw["w_out_na"][l], w_out_dn=pw["w_out_dn"][l],
            w_up=w_up[l], w_down=w_down[l], conv_w=ffn_conv_w[l], conv_b=ffn_conv_b[l][None])
        mods_ctx = mods_all[l, 0:1]
        mods_lat = mods_all[l, 1:1 + dbatch]

        qna, kna, vna, qm, ckv, kr, km, vm, qg, kg, vg = _proj(xp, mods_ctx, lw, None, latent=False, seq=seq)
        for lst, a in zip(caches, (kna, vna, ckv, kr, kg, vg)):
            lst.append(a)
        ona, odn = _ctx_attn(qna, kna, vna, qm, km, vm, qg, kg, vg, batch=batch, seq=seq)
        x1, h2 = _post(ona, odn, xp, mods_ctx, lw, seq_per_mod=None)
        xp = _ffn(h2, x1, mods_ctx, lw, seq=seq, seq_per_mod=None)

        qna, kna, vna, qm, km, vm, qg, kg, vg = _proj(xs, mods_lat, lw, rope, latent=True, seq=dseq)
        bias = jnp.take(rpb_flat[l], bias_idx, axis=1).transpose(1, 0, 2, 3)
        ona = _na_attn(qna, kna, vna, kc_na, vc_na, bias, l, batch=dbatch, seq=dseq)
        odn = _dense_attn(qm, qg, kc_m, vc_m, kc_g, vc_g, km, vm, kg, vg, l, batch=dbatch, seq=dseq)
        x1, h2 = _post(ona, odn, xs, mods_lat, lw, seq_per_mod=dseq)
        xs = _ffn(h2, x1, mods_lat, lw, seq=dseq, seq_per_mod=dseq)

    def stack(lst, tail):
        return jnp.stack([a.reshape((batch, seq) + tail) for a in lst], axis=1)

    return (xp.reshape(batch, seq, d), xs.reshape(dbatch, dseq, d),
            stack(caches[0], (NA_HEADS, HEAD_DIM)), stack(caches[1], (NA_HEADS, HEAD_DIM)),
            stack(caches[2], (MLA_KV_LORA,)), stack(caches[3], (MLA_ROPE,)),
            stack(caches[4], (GQA_KV_HEADS, HEAD_DIM)), stack(caches[5], (GQA_KV_HEADS, HEAD_DIM)))
```

```python
import functools

import numpy as np
import jax
import jax.numpy as jnp
from jax import lax
from jax.experimental import pallas as pl
from jax.experimental.pallas import tpu as pltpu

GRID_W = 64
HEAD_DIM = 64
NA_HEADS = 6
NA_KH = 8
NA_KW = 16
MLA_HEADS = 4
MLA_Q_LORA = 256
MLA_KV_LORA = 128
MLA_NOPE = 64
MLA_ROPE = 32
MLA_V = 64
GQA_HEADS = 6
GQA_KV_HEADS = 2
GQA_GROUP = GQA_HEADS // GQA_KV_HEADS
ROPE_THETA = 10000.0
EPS = 1e-6

NA_WIDTH = NA_HEADS * HEAD_DIM
MLA_WIDTH = MLA_HEADS * MLA_V
GQA_WIDTH = GQA_HEADS * HEAD_DIM
DENSE_WIDTH = MLA_WIDTH + GQA_WIDTH

LANES = 128
VMEM_LIMIT = 52 * 1024 * 1024
MASK_VALUE = -1e30
LOG2_E = 1.4426950408889634

PROJ_TILE = 512
POST_TILE = 512
FFN_TILE = 512
FF_CHUNK = 256
DENSE_Q_TILE = 256
DENSE_K_TILE = 512
NA_ROWS = 4
NA_KEY_ROWS = NA_ROWS + NA_KH
NA_PAD = NA_KEY_ROWS - NA_KH

C_QNA = 0
C_KNA = C_QNA + NA_WIDTH
C_VNA = C_KNA + NA_WIDTH
C_CQ = C_VNA + NA_WIDTH
C_CKV = C_CQ + MLA_Q_LORA
C_QG = C_CKV + MLA_KV_LORA
C_KG = C_QG + GQA_WIDTH
C_VG = C_KG + LANES
C_KR = C_VG + LANES
C_END_CTX = C_KR + LANES
C_KRS = C_END_CTX
C_QGS = C_KRS + LANES
C_KGS = C_QGS + GQA_WIDTH
C_END_LAT = C_KGS + LANES

GQA_ORDER = (0, 3, 1, 4, 2, 5)


def _dot(a, b):
    return jnp.dot(a, b, preferred_element_type=jnp.float32)


def _dot_nt(a, b):
    return lax.dot_general(a, b, (((1,), (1,)), ((), ())), preferred_element_type=jnp.float32)


def _params(*sem):
    return pltpu.CompilerParams(dimension_semantics=sem, vmem_limit_bytes=VMEM_LIMIT)


def _rms(x, g):
    return x * lax.rsqrt(jnp.mean(x * x, axis=-1, keepdims=True) + EPS) * g


def _head_mean_sq(x, ones_bf16):
    xx = x * x
    hi = xx.astype(jnp.bfloat16)
    lo = (xx - hi.astype(jnp.float32)).astype(jnp.bfloat16)
    return _dot(hi, ones_bf16) + _dot(lo, ones_bf16)


def _ada_kernel(cond_ref, w_ref, b_ref, o_ref):
    cnd = cond_ref[...]
    s = cnd / (1.0 + jnp.exp(-cnd))
    o_ref[0] = jnp.dot(s, w_ref[0], preferred_element_type=jnp.float32,
                       precision=lax.Precision.HIGHEST) + b_ref[0]


def _ada_mods(cond, w_ada, b_ada):
    nl, d, n6 = w_ada.shape
    tn = 1536
    return pl.pallas_call(
        _ada_kernel,
        grid=(nl, n6 // tn),
        in_specs=[pl.BlockSpec((8, d), lambda l, j: (0, 0)),
                  pl.BlockSpec((1, d, tn), lambda l, j: (l, 0, j)),
                  pl.BlockSpec((1, 1, tn), lambda l, j: (l, 0, j))],
        out_specs=pl.BlockSpec((1, 8, tn), lambda l, j: (l, 0, j)),
        out_shape=jax.ShapeDtypeStruct((nl, 8, n6), jnp.float32),
        compiler_params=_params("arbitrary", "arbitrary"),
        name="ada_mods",
    )(cond, w_ada, b_ada.reshape(nl, 1, n6))


def _proj_kernel(*refs, latent):
    if latent:
        (x_ref, mod_ref, gpre_ref, w_ref, gq_ref, wuq1_ref, wuq2_ref, gkv_ref, wuk_ref, wuv_ref,
         ones3_ref, ggq_ref, ggk_ref, ggqs_ref, ggks_ref, cg_ref, sg_ref, cm_ref, sm_ref,
         qna_ref, kna_ref, vna_ref, qm_ref, km_ref, vm_ref, qg_ref, kg_ref, vg_ref) = refs
    else:
        (x_ref, mod_ref, gpre_ref, w_ref, gq_ref, wuq1_ref, gkv_ref, wuk_ref, wuv_ref,
         ones3_ref, ggq_ref, ggk_ref,
         qna_ref, kna_ref, vna_ref, qm_ref, ckv_ref, kr_ref, km_ref, vm_ref, qg_ref, kg_ref, vg_ref) = refs

    x = x_ref[...]
    mod = mod_ref[0]
    sh, sc = mod[0:1, :], mod[1:2, :]
    h = (_rms(x, gpre_ref[...]) * (1.0 + sc) + sh).astype(jnp.bfloat16)

    def proj(c0, width):
        return _dot(h, w_ref[:, c0:c0 + width])

    na_scale = HEAD_DIM ** -0.5
    dense_unit = LOG2_E if latent else 1.0
    qna_ref[...] = (proj(C_QNA, NA_WIDTH) * na_scale).astype(qna_ref.dtype)
    kna_ref[...] = proj(C_KNA, NA_WIDTH).astype(kna_ref.dtype)
    vna_ref[...] = proj(C_VNA, NA_WIDTH).astype(vna_ref.dtype)

    mla_scale = (MLA_NOPE + MLA_ROPE) ** -0.5
    cqn = _rms(proj(C_CQ, MLA_Q_LORA), gq_ref[...]).astype(jnp.bfloat16)
    qm = _dot(cqn, wuq1_ref[...])
    kr = proj(C_KR, LANES)
    if latent:
        cm = jnp.concatenate([cm_ref[...]] * MLA_HEADS, axis=1)
        sm = jnp.concatenate([sm_ref[...]] * MLA_HEADS, axis=1)
        qm = qm * cm + _dot(cqn, wuq2_ref[...]) * sm
        kr = kr * cm_ref[...] + proj(C_KRS, LANES) * sm_ref[...]
    qm_ref[...] = (qm * (mla_scale * dense_unit)).astype(qm_ref.dtype)
    ckv = _rms(proj(C_CKV, MLA_KV_LORA), gkv_ref[...])
    ckv_b = ckv.astype(jnp.bfloat16)
    kn = _dot(ckv_b, wuk_ref[...])
    km_ref[...] = (kn + jnp.concatenate([kr] * MLA_HEADS, axis=1)).astype(km_ref.dtype)
    vm_ref[...] = _dot(ckv_b, wuv_ref[...]).astype(vm_ref.dtype)
    if not latent:
        ckv_ref[...] = ckv
        kr_ref[...] = kr[:, MLA_NOPE:MLA_NOPE + MLA_ROPE]

    ones3 = ones3_ref[...]
    ones1 = ones3[0:LANES, 0:LANES]
    qg = proj(C_QG, GQA_WIDTH)
    rq = lax.rsqrt(_head_mean_sq(qg, ones3) + EPS)
    qg = qg * rq * ggq_ref[...]
    kg = proj(C_KG, LANES)
    rk = lax.rsqrt(_head_mean_sq(kg, ones1) + EPS)
    kg = kg * rk * ggk_ref[...]
    if latent:
        cg, sg = cg_ref[...], sg_ref[...]
        cg3 = jnp.concatenate([cg] * (GQA_WIDTH // LANES), axis=1)
        sg3 = jnp.concatenate([sg] * (GQA_WIDTH // LANES), axis=1)
        qg = qg * cg3 + (proj(C_QGS, GQA_WIDTH) * rq * ggqs_ref[...]) * sg3
        kg = kg * cg + (proj(C_KGS, LANES) * rk * ggks_ref[...]) * sg
    qg_ref[...] = (qg * (na_scale * dense_unit)).astype(qg_ref.dtype)
    kg_ref[...] = kg.astype(kg_ref.dtype)
    vg_ref[...] = proj(C_VG, LANES).astype(vg_ref.dtype)


def _proj(x, mods, lw, rope, *, latent, seq):
    ntok, d = x.shape
    t = PROJ_TILE
    nt = ntok // t
    tiles_per_seq = seq // t if latent else 1
    tok = lambda w: pl.BlockSpec((t, w), lambda i: (i, 0))
    full = lambda a: pl.BlockSpec(a.shape, lambda i: (0,) * a.ndim)
    if latent:
        mod_spec = pl.BlockSpec((1, 6, d), lambda i: (i // tiles_per_seq, 0, 0))
    else:
        mod_spec = pl.BlockSpec((1, 6, d), lambda i: (0, 0, 0))
    kv_dt = jnp.bfloat16 if latent else jnp.float32
    bf = jnp.bfloat16
    sds = lambda w, dt: jax.ShapeDtypeStruct((ntok, w), dt)
    if latent:
        rope_spec = pl.BlockSpec((t, LANES), lambda i: (i % tiles_per_seq, 0))
        ins = [x, mods, lw["g_pre_mix"], lw["w_in_lat"], lw["mla_g_q"], lw["w_uq1"], lw["w_uq2"], lw["mla_g_kv"],
               lw["w_uk"], lw["w_uv"], lw["ones3"], lw["ggq"], lw["ggk"], lw["ggqs"], lw["ggks"],
               rope["cg"], rope["sg"], rope["cm"], rope["sm"]]
        in_specs = [tok(d), mod_spec] + [full(a) for a in ins[2:15]] + [rope_spec] * 4
        outs = [sds(NA_WIDTH, bf), sds(NA_WIDTH, kv_dt), sds(NA_WIDTH, kv_dt), sds(MLA_HEADS * LANES, bf),
                sds(MLA_HEADS * LANES, bf), sds(MLA_WIDTH, bf), sds(GQA_WIDTH, bf), sds(LANES, kv_dt), sds(LANES, kv_dt)]
    else:
        ins = [x, mods, lw["g_pre_mix"], lw["w_in_ctx"], lw["mla_g_q"], lw["w_uq1"], lw["mla_g_kv"],
               lw["w_uk"], lw["w_uv"], lw["ones3"], lw["ggq"], lw["ggk"]]
        in_specs = [tok(d), mod_spec] + [full(a) for a in ins[2:]]
        outs = [sds(NA_WIDTH, bf), sds(NA_WIDTH, kv_dt), sds(NA_WIDTH, kv_dt), sds(MLA_HEADS * LANES, bf),
                sds(MLA_KV_LORA, jnp.float32), sds(MLA_ROPE, jnp.float32),
                sds(MLA_HEADS * LANES, bf), sds(MLA_WIDTH, bf), sds(GQA_WIDTH, bf), sds(LANES, kv_dt), sds(LANES, kv_dt)]
    out_specs = [tok(o.shape[1]) for o in outs]
    return pl.pallas_call(
        functools.partial(_proj_kernel, latent=latent),
        grid=(nt,),
        in_specs=in_specs,
        out_specs=out_specs,
        out_shape=outs,
        compiler_params=_params("arbitrary"),
        name="proj_lat" if latent else "proj_ctx",
    )(*ins)


def _lane_lo():
    return lax.broadcasted_iota(jnp.int32, (1, LANES), 1) < HEAD_DIM


def _split_heads(q):
    lo = _lane_lo()
    zero = jnp.zeros_like(q)
    return jnp.where(lo, q, zero), jnp.where(lo, zero, q)


def _softmax_pv(scores, values):
    m = scores[0].max(axis=-1, keepdims=True)
    for s in scores[1:]:
        m = jnp.maximum(m, s.max(axis=-1, keepdims=True))
    l = None
    acc = None
    for s, v in zip(scores, values):
        p = jnp.exp(s - m)
        ps = p.sum(axis=-1, keepdims=True)
        pv = _dot(p.astype(jnp.bfloat16), v)
        l = ps if l is None else l + ps
        acc = pv if acc is None else acc + pv
    return acc / l


def _ctx_attn_kernel(qna_ref, kna_ref, vna_ref, qm_ref, km_ref, vm_ref, qg_ref, kg_ref, vg_ref, ona_ref, odn_ref):
    lo = _lane_lo()
    bf = jnp.bfloat16

    def pair(q_lo, q_hi, k_lo, k_hi, v):
        o_lo = _softmax_pv([_dot_nt(q_lo, k_lo)], [v])
        o_hi = _softmax_pv([_dot_nt(q_hi, k_hi)], [v])
        return jnp.where(lo, o_lo, o_hi)

    for p in range(NA_WIDTH // LANES):
        cs = slice(p * LANES, (p + 1) * LANES)
        q_lo, q_hi = _split_heads(qna_ref[0, :, cs])
        k = kna_ref[0, :, cs].astype(bf)
        ona_ref[0, :, cs] = pair(q_lo, q_hi, k, k, vna_ref[0, :, cs].astype(bf)).astype(ona_ref.dtype)
    for p in range(MLA_HEADS // 2):
        c0 = 2 * p * LANES
        o = pair(qm_ref[0, :, c0:c0 + LANES], qm_ref[0, :, c0 + LANES:c0 + 2 * LANES],
                 km_ref[0, :, c0:c0 + LANES], km_ref[0, :, c0 + LANES:c0 + 2 * LANES],
                 vm_ref[0, :, p * LANES:(p + 1) * LANES])
        odn_ref[0, :, p * LANES:(p + 1) * LANES] = o.astype(odn_ref.dtype)
    kg = kg_ref[0].astype(bf)
    vg = vg_ref[0].astype(bf)
    for c in range(GQA_WIDTH // LANES):
        q_lo, q_hi = _split_heads(qg_ref[0, :, c * LANES:(c + 1) * LANES])
        o = pair(q_lo, q_hi, kg, kg, vg)
        odn_ref[0, :, MLA_WIDTH + c * LANES:MLA_WIDTH + (c + 1) * LANES] = o.astype(odn_ref.dtype)


def _ctx_attn(qna, kna, vna, qm, km, vm, qg, kg, vg, *, batch, seq):
    ins = [a.reshape(batch, seq, a.shape[-1]) for a in (qna, kna, vna, qm, km, vm, qg, kg, vg)]
    spec = lambda a: pl.BlockSpec((1, seq, a.shape[-1]), lambda b: (b, 0, 0))
    outs = [jax.ShapeDtypeStruct((batch, seq, NA_WIDTH), jnp.bfloat16),
            jax.ShapeDtypeStruct((batch, seq, DENSE_WIDTH), jnp.bfloat16)]
    ona, odn = pl.pallas_call(
        _ctx_attn_kernel,
        grid=(batch,),
        in_specs=[spec(a) for a in ins],
        out_specs=[spec(o) for o in outs],
        out_shape=outs,
        compiler_params=_params("arbitrary"),
        name="ctx_attn",
    )(*ins)
    return ona.reshape(batch * seq, NA_WIDTH), odn.reshape(batch * seq, DENSE_WIDTH)


def _na_attn_kernel(q_ref, k_ref, v_ref, kc_ref, vc_ref, pair_ref, o_ref, *, rows):
    lo = _lane_lo()
    blk = pl.program_id(1)
    row0 = blk * NA_ROWS
    key_row0 = jnp.clip(row0 - NA_KH // 2, 0, rows - NA_KEY_ROWS)
    start = pl.multiple_of(key_row0 * GRID_W, GRID_W)
    tq = NA_ROWS * GRID_W
    nkeys = NA_KEY_ROWS * GRID_W

    rq = row0 + lax.shift_right_logical(lax.broadcasted_iota(jnp.int32, (tq, 1), 0), 6)
    rk = key_row0 + lax.shift_right_logical(lax.broadcasted_iota(jnp.int32, (1, nkeys), 1), 6)
    rs = jnp.clip(rq - NA_KH // 2, 0, rows - NA_KH)
    row_mask = jnp.where((rk >= rs) & (rk < rs + NA_KH), 0.0, MASK_VALUE)

    def bias(h):
        base = key_row0 - row0 + (NA_KH - 1) + NA_PAD
        blocks = [jnp.concatenate([pair_ref[0, h, base + 2 * m - dq] for m in range(NA_KEY_ROWS // 2)], axis=1)
                  for dq in range(NA_ROWS)]
        return jnp.concatenate(blocks, axis=0) + row_mask

    for p in range(NA_WIDTH // LANES):
        cs = slice(p * LANES, (p + 1) * LANES)
        q_lo, q_hi = _split_heads(q_ref[0, :, cs])
        k = k_ref[0, pl.ds(start, nkeys), cs]
        v = v_ref[0, pl.ds(start, nkeys), cs]
        kc = kc_ref[0, 0, :, cs]
        vc = vc_ref[0, 0, :, cs]
        o_lo = _softmax_pv([_dot_nt(q_lo, k) + bias(2 * p), _dot_nt(q_lo, kc)], [v, vc])
        o_hi = _softmax_pv([_dot_nt(q_hi, k) + bias(2 * p + 1), _dot_nt(q_hi, kc)], [v, vc])
        o_ref[0, :, cs] = jnp.where(lo, o_lo, o_hi).astype(o_ref.dtype)


def _na_attn(q, k, v, kc, vc, pairs, layer, *, batch, seq):
    rows = seq // GRID_W
    nblk = rows // NA_ROWS
    tq = NA_ROWS * GRID_W
    past = kc.shape[2]
    assert GRID_W == 64 and rows >= NA_KEY_ROWS and rows % NA_ROWS == 0
    q3, k3, v3 = (a.reshape(batch, seq, NA_WIDTH) for a in (q, k, v))
    o = pl.pallas_call(
        functools.partial(_na_attn_kernel, rows=rows),
        grid=(batch, nblk),
        in_specs=[pl.BlockSpec((1, tq, NA_WIDTH), lambda b, i: (b, i, 0)),
                  pl.BlockSpec((1, seq, NA_WIDTH), lambda b, i: (b, 0, 0)),
                  pl.BlockSpec((1, seq, NA_WIDTH), lambda b, i: (b, 0, 0)),
                  pl.BlockSpec((1, 1, past, NA_WIDTH), lambda b, i: (b, layer, 0, 0)),
                  pl.BlockSpec((1, 1, past, NA_WIDTH), lambda b, i: (b, layer, 0, 0)),
                  pl.BlockSpec((1,) + pairs.shape[1:], lambda b, i: (layer, 0, 0, 0, 0))],
        out_specs=pl.BlockSpec((1, tq, NA_WIDTH), lambda b, i: (b, i, 0)),
        out_shape=jax.ShapeDtypeStruct((batch, seq, NA_WIDTH), jnp.bfloat16),
        compiler_params=_params("arbitrary", "arbitrary"),
        name="na_attn",
    )(q3, k3, v3, kc, vc, pairs)
    return o.reshape(batch * seq, NA_WIDTH)


def _na_pair_tables(na_rpb):
    cq = np.arange(GRID_W)[:, None]
    ck = np.arange(GRID_W)[None, :]
    cs = np.clip(cq - NA_KW // 2, 0, GRID_W - NA_KW)
    valid = (ck >= cs) & (ck < cs + NA_KW)
    onehot = ((ck - cq + NA_KW - 1)[None] == np.arange(2 * NA_KW - 1)[:, None, None]) & valid[None]
    t = jnp.einsum("lhdo,oqk->lhdqk", na_rpb, jnp.asarray(onehot, na_rpb.dtype), precision=lax.Precision.HIGHEST)
    t = jnp.where(jnp.asarray(valid), t, MASK_VALUE)
    t = jnp.pad(t, ((0, 0), (0, 0), (NA_PAD, NA_PAD), (0, 0), (0, 0)), constant_values=MASK_VALUE)
    return jnp.concatenate([t[:, :, :-1], t[:, :, 1:]], axis=-1)


def _dense_attn_kernel(qm_ref, qg_ref, kmc_ref, vmc_ref, kgc_ref, vgc_ref, kml_ref, vml_ref, kgl_ref, vgl_ref,
                       o_ref, s_lo_ref, s_hi_ref, *, nk, tk):
    lo = _lane_lo()
    bf = jnp.bfloat16
    past = kmc_ref.shape[2]

    def lane_fold(x, op):
        parts = [x[:, c * LANES:(c + 1) * LANES] for c in range(x.shape[1] // LANES)]
        while len(parts) > 1:
            parts = [op(parts[i], parts[i + 1]) if i + 1 < len(parts) else parts[i] for i in range(0, len(parts), 2)]
        return parts[0]

    def head(q, kc, vc, k_at, v_at, s_ref):
        spans = [(0, past)] + [(past + j * tk, tk) for j in range(nk)]
        mx = None
        for idx, (c0, width) in enumerate(spans):
            s = _dot_nt(q, kc if idx == 0 else k_at(idx - 1))
            s_ref[:, c0:c0 + width] = s
            part = lane_fold(s, jnp.maximum)
            mx = part if mx is None else jnp.maximum(mx, part)
        m = mx.max(axis=-1, keepdims=True)
        lsum = None
        acc = None
        for idx, (c0, width) in enumerate(spans):
            p = jnp.exp2(s_ref[:, c0:c0 + width] - m)
            part = lane_fold(p, jnp.add)
            pv = _dot(p.astype(bf), vc if idx == 0 else v_at(idx - 1))
            lsum = part if lsum is None else lsum + part
            acc = pv if acc is None else acc + pv
        return acc / lsum.sum(axis=-1, keepdims=True)

    def unit(q_lo, q_hi, kc_lo, kc_hi, vc, k_lo_at, k_hi_at, v_at):
        o_lo = head(q_lo, kc_lo, vc, k_lo_at, v_at, s_lo_ref)
        o_hi = head(q_hi, kc_hi, vc, k_hi_at, v_at, s_hi_ref)
        return jnp.where(lo, o_lo, o_hi)

    for p in range(MLA_HEADS // 2):
        c0 = 2 * p * LANES
        c1 = c0 + LANES
        o = unit(qm_ref[0, :, c0:c1], qm_ref[0, :, c1:c1 + LANES],
                 kmc_ref[0, 0, :, c0:c1], kmc_ref[0, 0, :, c1:c1 + LANES], vmc_ref[0, 0, :, p * LANES:(p + 1) * LANES],
                 lambda j, c0=c0, c1=c1: kml_ref[0, j * tk:(j + 1) * tk, c0:c1],
                 lambda j, c1=c1: kml_ref[0, j * tk:(j + 1) * tk, c1:c1 + LANES],
                 lambda j, p=p: vml_ref[0, j * tk:(j + 1) * tk, p * LANES:(p + 1) * LANES])
        o_ref[0, :, p * LANES:(p + 1) * LANES] = o.astype(o_ref.dtype)
    kgc = kgc_ref[0, 0]
    vgc = vgc_ref[0, 0]
    kg_at = lambda j: kgl_ref[0, j * tk:(j + 1) * tk, :]
    vg_at = lambda j: vgl_ref[0, j * tk:(j + 1) * tk, :]
    for c in range(GQA_WIDTH // LANES):
        q_lo, q_hi = _split_heads(qg_ref[0, :, c * LANES:(c + 1) * LANES])
        o = unit(q_lo, q_hi, kgc, kgc, vgc, kg_at, kg_at, vg_at)
        o_ref[0, :, MLA_WIDTH + c * LANES:MLA_WIDTH + (c + 1) * LANES] = o.astype(o_ref.dtype)


def _dense_attn(qm, qg, kmc, vmc, kgc, vgc, kml, vml, kgl, vgl, layer, *, batch, seq):
    tq, tk = DENSE_Q_TILE, DENSE_K_TILE
    r3 = lambda a: a.reshape(batch, seq, a.shape[-1])
    qm, qg, kml, vml, kgl, vgl = (r3(a) for a in (qm, qg, kml, vml, kgl, vgl))
    qspec = lambda a: pl.BlockSpec((1, tq, a.shape[-1]), lambda b, i: (b, i, 0))
    cspec = lambda a: pl.BlockSpec((1, 1) + a.shape[2:], lambda b, i: (b, layer, 0, 0))
    lspec = lambda a: pl.BlockSpec((1, seq, a.shape[-1]), lambda b, i: (b, 0, 0), pipeline_mode=pl.Buffered(1))
    o = pl.pallas_call(
        functools.partial(_dense_attn_kernel, nk=seq // tk, tk=tk),
        grid=(batch, seq // tq),
        in_specs=[qspec(qm), qspec(qg), cspec(kmc), cspec(vmc), cspec(kgc), cspec(vgc),
                  lspec(kml), lspec(vml), lspec(kgl), lspec(vgl)],
        out_specs=pl.BlockSpec((1, tq, DENSE_WIDTH), lambda b, i: (b, i, 0)),
        out_shape=jax.ShapeDtypeStruct((batch, seq, DENSE_WIDTH), jnp.bfloat16),
        scratch_shapes=[pltpu.VMEM((tq, kmc.shape[2] + seq), jnp.float32)] * 2,
        compiler_params=_params("arbitrary", "arbitrary"),
        name="dense_attn",
    )(qm, qg, kmc, vmc, kgc, vgc, kml, vml, kgl, vgl)
    return o.reshape(batch * seq, DENSE_WIDTH)


def _mla_ctx_kernel(ckv_ref, kr_ref, wuk_ref, wuv_ref, k_ref, v_ref):
    ckv = ckv_ref[0, 0]
    kn = _dot(ckv, wuk_ref[0])
    k_ref[0, 0] = (kn + jnp.concatenate([kr_ref[0, 0].astype(jnp.float32)] * MLA_HEADS, axis=1)).astype(k_ref.dtype)
    v_ref[0, 0] = _dot(ckv, wuv_ref[0]).astype(v_ref.dtype)


def _mla_ctx_expand(ckv, kr_pad, wuk, wuv):
    b, nl, past, _ = ckv.shape
    spec = lambda w: pl.BlockSpec((1, 1, past, w), lambda i, l: (i, l, 0, 0))
    wspec = lambda a: pl.BlockSpec((1,) + a.shape[1:], lambda i, l: (l, 0, 0))
    return pl.pallas_call(
        _mla_ctx_kernel,
        grid=(b, nl),
        in_specs=[spec(MLA_KV_LORA), spec(LANES), wspec(wuk), wspec(wuv)],
        out_specs=[spec(MLA_HEADS * LANES), spec(MLA_WIDTH)],
        out_shape=[jax.ShapeDtypeStruct((b, nl, past, MLA_HEADS * LANES), jnp.bfloat16),
                   jax.ShapeDtypeStruct((b, nl, past, MLA_WIDTH), jnp.bfloat16)],
        compiler_params=_params("arbitrary", "arbitrary"),
        name="mla_ctx_expand",
    )(ckv, kr_pad, wuk, wuv)


def _post_kernel(ona_ref, odn_ref, x_ref, mod_ref, wna_ref, wdn_ref, gpost_ref, gpre_ref, x1_ref, h2_ref):
    mod = mod_ref[0]
    gt_m, sh_f, sc_f = mod[2:3, :], mod[3:4, :], mod[4:5, :]
    o = _dot(ona_ref[...], wna_ref[...]) + _dot(odn_ref[...], wdn_ref[...])
    x1 = x_ref[...] + gt_m * _rms(o, gpost_ref[...])
    x1_ref[...] = x1
    h2_ref[...] = (_rms(x1, gpre_ref[...]) * (1.0 + sc_f) + sh_f).astype(h2_ref.dtype)


def _post(ona, odn, x, mods, lw, *, seq_per_mod):
    ntok, d = x.shape
    t = POST_TILE
    tok = lambda w: pl.BlockSpec((t, w), lambda i: (i, 0))
    full = lambda a: pl.BlockSpec(a.shape, lambda i: (0,) * a.ndim)
    if seq_per_mod is None:
        mod_spec = pl.BlockSpec((1, 6, d), lambda i: (0, 0, 0))
    else:
        tps = seq_per_mod // t
        mod_spec = pl.BlockSpec((1, 6, d), lambda i: (i // tps, 0, 0))
    ins = [ona, odn, x, mods, lw["w_out_na"], lw["w_out_dn"], lw["g_post_mix"], lw["g_pre_ffn"]]
    return pl.pallas_call(
        _post_kernel,
        grid=(ntok // t,),
        in_specs=[tok(NA_WIDTH), tok(DENSE_WIDTH), tok(d), mod_spec] + [full(a) for a in ins[4:]],
        out_specs=[tok(d), tok(d)],
        out_shape=[jax.ShapeDtypeStruct((ntok, d), jnp.float32), jax.ShapeDtypeStruct((ntok, d), jnp.bfloat16)],
        compiler_params=_params("arbitrary"),
        name="post",
    )(*ins)


def _ffn_kernel(h_ref, hp_ref, hn_ref, x1_ref, mod_ref, wg_ref, wv_ref, cwg_ref, cwv_ref, cbg_ref, cbv_ref,
                wd_ref, gpost_ref, o_ref, acc_ref, *, seq):
    i = pl.program_id(0)
    j = pl.program_id(1)
    t = h_ref.shape[0]
    halo = hp_ref.shape[0]

    @pl.when(j == 0)
    def _():
        acc_ref[...] = jnp.zeros_like(acc_ref)

    h = h_ref[...]
    row = lax.broadcasted_iota(jnp.int32, (t, 1), 0)
    pos = jnp.bitwise_and(i * t + row, seq - 1)
    first, last = row == 0, row == t - 1
    seq_first, seq_last = pos == 0, pos == seq - 1

    def conv(w_ref, cw_ref, cb_ref):
        w = w_ref[...]
        u = _dot(h, w)
        u_hp = _dot(hp_ref[...], w)[halo - 1:halo, :]
        u_hn = _dot(hn_ref[...], w)[0:1, :]
        prev = jnp.where(first, u_hp, pltpu.roll(u, 1, 0))
        prev = jnp.where(seq_first, 0.0, prev)
        nxt = jnp.where(last, u_hn, pltpu.roll(u, t - 1, 0))
        nxt = jnp.where(seq_last, 0.0, nxt)
        cw = cw_ref[...]
        return prev * cw[0:1, :] + u * cw[1:2, :] + nxt * cw[2:3, :] + cb_ref[...]

    gate = conv(wg_ref, cwg_ref, cbg_ref)
    val = conv(wv_ref, cwv_ref, cbv_ref)
    act = gate / (1.0 + jnp.exp(-gate)) * val
    acc_ref[...] += _dot(act.astype(jnp.bfloat16), wd_ref[...])

    @pl.when(j == pl.num_programs(1) - 1)
    def _():
        gt_f = mod_ref[0][5:6, :]
        o_ref[...] = x1_ref[...] + gt_f * _rms(acc_ref[...], gpost_ref[...])


def _ffn(h2, x1, mods, lw, *, seq, seq_per_mod):
    ntok, d = x1.shape
    t = FFN_TILE
    halo = 16
    c = FF_CHUNK
    dff = lw["w_down"].shape[0]
    nchunk = dff // c
    assert seq & (seq - 1) == 0 and (t % seq == 0 or seq % t == 0)
    nhalo = ntok // halo
    if seq_per_mod is None:
        mod_spec = pl.BlockSpec((1, 6, d), lambda i, j: (0, 0, 0))
    else:
        tps = seq_per_mod // t
        mod_spec = pl.BlockSpec((1, 6, d), lambda i, j: (i // tps, 0, 0))
    tok = pl.BlockSpec((t, d), lambda i, j: (i, 0))
    in_specs = [
        tok,
        pl.BlockSpec((halo, d), lambda i, j: (jnp.maximum(i * (t // halo) - 1, 0), 0)),
        pl.BlockSpec((halo, d), lambda i, j: (jnp.minimum((i + 1) * (t // halo), nhalo - 1), 0)),
        tok,
        mod_spec,
        pl.BlockSpec((d, c), lambda i, j: (0, j)),
        pl.BlockSpec((d, c), lambda i, j: (0, nchunk + j)),
        pl.BlockSpec((3, c), lambda i, j: (0, j)),
        pl.BlockSpec((3, c), lambda i, j: (0, nchunk + j)),
        pl.BlockSpec((1, c), lambda i, j: (0, j)),
        pl.BlockSpec((1, c), lambda i, j: (0, nchunk + j)),
        pl.BlockSpec((c, d), lambda i, j: (j, 0)),
        pl.BlockSpec((1, d), lambda i, j: (0, 0)),
    ]
    return pl.pallas_call(
        functools.partial(_ffn_kernel, seq=seq),
        grid=(ntok // t, nchunk),
        in_specs=in_specs,
        out_specs=tok,
        out_shape=jax.ShapeDtypeStruct((ntok, d), jnp.float32),
        scratch_shapes=[pltpu.VMEM((t, d), jnp.float32)],
        compiler_params=_params("arbitrary", "arbitrary"),
        name="ffn",
    )(h2, h2, h2, x1, mods, lw["w_up"], lw["w_up"], lw["conv_w"], lw["conv_w"], lw["conv_b"], lw["conv_b"],
      lw["w_down"], lw["g_post_ffn"])


def _swap_halves(n, group):
    idx = np.arange(n).reshape(-1, 2, group // 2)
    return idx[:, ::-1, :].reshape(-1)


def _prep_weights(w_in, mla_w_uq, mla_w_ukv, w_out, gqa_g_q, gqa_g_k):
    nl, d, in_cols = w_in.shape
    bf = jnp.bfloat16
    o_qna, o_kna, o_vna = 0, NA_WIDTH, 2 * NA_WIDTH
    o_cq = 3 * NA_WIDTH
    o_ckv = o_cq + MLA_Q_LORA
    o_kr = o_ckv + MLA_KV_LORA
    o_qg = o_kr + MLA_ROPE
    o_kg = o_qg + GQA_WIDTH
    o_vg = o_kg + GQA_KV_HEADS * HEAD_DIM
    zero = in_cols
    z = lambda n: np.full((n,), zero, np.int64)
    ar = np.arange
    sw64 = _swap_halves(HEAD_DIM, HEAD_DIM // 2)
    sw32 = _swap_halves(MLA_ROPE, MLA_ROPE // 2)
    qg_cols = np.concatenate([o_qg + HEAD_DIM * h + ar(HEAD_DIM) for h in GQA_ORDER])
    qgs_cols = np.concatenate([o_qg + HEAD_DIM * h + sw64 for h in GQA_ORDER])
    kg_cols = o_kg + ar(GQA_KV_HEADS * HEAD_DIM)
    kgs_cols = np.concatenate([o_kg + HEAD_DIM * h + sw64 for h in range(GQA_KV_HEADS)])
    pad_r = LANES - MLA_NOPE - MLA_ROPE
    cols = np.concatenate([
        o_qna + ar(NA_WIDTH), o_kna + ar(NA_WIDTH), o_vna + ar(NA_WIDTH),
        o_cq + ar(MLA_Q_LORA), o_ckv + ar(MLA_KV_LORA),
        qg_cols, kg_cols, o_vg + ar(GQA_KV_HEADS * HEAD_DIM),
        z(MLA_NOPE), o_kr + ar(MLA_ROPE), z(pad_r),
        z(MLA_NOPE), o_kr + sw32, z(pad_r),
        qgs_cols, kgs_cols])
    assert cols.shape[0] == C_END_LAT
    w_pad = jnp.concatenate([w_in, jnp.zeros((nl, d, 1), w_in.dtype)], axis=2)
    w_all = jnp.take(w_pad, jnp.asarray(cols, jnp.int32), axis=2).astype(bf)

    qw = MLA_NOPE + MLA_ROPE
    zq = mla_w_uq.shape[2]
    uq1 = np.concatenate([np.concatenate([qw * h + ar(qw), np.full((pad_r,), zq)]) for h in range(MLA_HEADS)])
    uq2 = np.concatenate([np.concatenate([np.full((MLA_NOPE,), zq), qw * h + MLA_NOPE + sw32, np.full((pad_r,), zq)])
                          for h in range(MLA_HEADS)])
    uq_pad = jnp.concatenate([mla_w_uq, jnp.zeros(mla_w_uq.shape[:2] + (1,), mla_w_uq.dtype)], axis=2)
    w_uq1 = jnp.take(uq_pad, jnp.asarray(uq1, jnp.int32), axis=2).astype(bf)
    w_uq2 = jnp.take(uq_pad, jnp.asarray(uq2, jnp.int32), axis=2).astype(bf)
    kvw = MLA_NOPE + MLA_V
    zk = mla_w_ukv.shape[2]
    uk = np.concatenate([np.concatenate([kvw * h + ar(MLA_NOPE), np.full((LANES - MLA_NOPE,), zk)])
                         for h in range(MLA_HEADS)])
    uv = np.concatenate([kvw * h + MLA_NOPE + ar(MLA_V) for h in range(MLA_HEADS)])
    ukv_pad = jnp.concatenate([mla_w_ukv, jnp.zeros(mla_w_ukv.shape[:2] + (1,), mla_w_ukv.dtype)], axis=2)
    w_uk = jnp.take(ukv_pad, jnp.asarray(uk, jnp.int32), axis=2).astype(bf)
    w_uv = jnp.take(ukv_pad, jnp.asarray(uv, jnp.int32), axis=2).astype(bf)

    o_gqa = NA_WIDTH + MLA_WIDTH
    rows_dn = np.concatenate([NA_WIDTH + ar(MLA_WIDTH)] + [o_gqa + HEAD_DIM * h + ar(HEAD_DIM) for h in GQA_ORDER])
    w_out_na = w_out[:, :NA_WIDTH, :].astype(bf)
    w_out_dn = jnp.take(w_out, jnp.asarray(rows_dn, jnp.int32), axis=1).astype(bf)

    ggq = jnp.tile(gqa_g_q, (1, GQA_HEADS))[:, None, :]
    ggk = jnp.tile(gqa_g_k, (1, GQA_KV_HEADS))[:, None, :]
    ggqs = jnp.tile(gqa_g_q[:, sw64], (1, GQA_HEADS))[:, None, :]
    ggks = jnp.tile(gqa_g_k[:, sw64], (1, GQA_KV_HEADS))[:, None, :]
    blk = np.kron(np.eye(GQA_WIDTH // HEAD_DIM), np.full((HEAD_DIM, HEAD_DIM), 1.0 / HEAD_DIM))
    ones3 = jnp.asarray(blk, bf)
    return dict(w_all=w_all, w_uq1=w_uq1, w_uq2=w_uq2, w_uk=w_uk, w_uv=w_uv, w_out_na=w_out_na, w_out_dn=w_out_dn,
                ggq=ggq, ggk=ggk, ggqs=ggqs, ggks=ggks, ones3=ones3)


def _rope_tables(seq):
    t = jnp.arange(seq, dtype=jnp.int32)
    r, c = (t // GRID_W).astype(jnp.float32), (t % GRID_W).astype(jnp.float32)

    def tables(dim):
        quarter = dim // 4
        freqs = ROPE_THETA ** (-jnp.arange(quarter, dtype=jnp.float32) / quarter)
        ar_, ac_ = r[:, None] * freqs[None, :], c[:, None] * freqs[None, :]
        cos = jnp.concatenate([jnp.cos(ar_), jnp.cos(ar_), jnp.cos(ac_), jnp.cos(ac_)], axis=1)
        sin = jnp.concatenate([-jnp.sin(ar_), jnp.sin(ar_), -jnp.sin(ac_), jnp.sin(ac_)], axis=1)
        return cos, sin

    c64, s64 = tables(HEAD_DIM)
    c32, s32 = tables(MLA_ROPE)
    pad = LANES - MLA_NOPE - MLA_ROPE
    cm = jnp.concatenate([jnp.ones((seq, MLA_NOPE)), c32, jnp.zeros((seq, pad))], axis=1)
    sm = jnp.concatenate([jnp.zeros((seq, MLA_NOPE)), s32, jnp.zeros((seq, pad))], axis=1)
    return dict(cg=jnp.tile(c64, (1, 2)), sg=jnp.tile(s64, (1, 2)), cm=cm.astype(jnp.float32), sm=sm.astype(jnp.float32))


def kernel(x_prompt, x_sample, c, cache_na_k, cache_na_v, cache_mla_ckv, cache_mla_krope, cache_gqa_k, cache_gqa_v, c_ctx, w_ada, b_ada, g_pre_mix, g_post_mix, g_pre_ffn, g_post_ffn, w_in, na_rpb, mla_g_q, mla_w_uq, mla_g_kv, mla_w_ukv, gqa_g_q, gqa_g_k, w_out, ffn_w_up, ffn_conv_w, ffn_conv_b, ffn_w_down):
    batch, seq, d = x_prompt.shape
    dbatch, dseq, _ = x_sample.shape
    nl = w_ada.shape[0]
    past = cache_na_k.shape[2]
    bf = jnp.bfloat16
    assert dbatch + 1 <= 8 and dseq % GRID_W == 0

    cond = jnp.zeros((8, d), jnp.float32).at[0].set(c_ctx).at[1:1 + dbatch].set(c)
    mods_all = _ada_mods(cond, w_ada, b_ada).reshape(nl, 8, 6, d)

    pw = _prep_weights(w_in, mla_w_uq, mla_w_ukv, w_out, gqa_g_q, gqa_g_k)
    rope = _rope_tables(dseq)
    w_up = ffn_w_up.astype(bf)
    w_down = ffn_w_down.astype(bf)

    kc_na = cache_na_k.reshape(dbatch, nl, past, NA_WIDTH).astype(bf)
    vc_na = cache_na_v.reshape(dbatch, nl, past, NA_WIDTH).astype(bf)
    kc_g = cache_gqa_k.reshape(dbatch, nl, past, GQA_KV_HEADS * HEAD_DIM).astype(bf)
    vc_g = cache_gqa_v.reshape(dbatch, nl, past, GQA_KV_HEADS * HEAD_DIM).astype(bf)
    kr_pad = jnp.pad(cache_mla_krope, ((0, 0), (0, 0), (0, 0), (MLA_NOPE, LANES - MLA_NOPE - MLA_ROPE))).astype(bf)
    kc_m, vc_m = _mla_ctx_expand(cache_mla_ckv.astype(bf), kr_pad, pw["w_uk"], pw["w_uv"])

    na_pairs = _na_pair_tables(na_rpb)

    xp = x_prompt.reshape(batch * seq, d)
    xs = x_sample.reshape(dbatch * dseq, d)
    caches = [[] for _ in range(6)]
    for l in range(nl):
        lw = dict(
            g_pre_mix=g_pre_mix[l][None], g_post_mix=g_post_mix[l][None], g_pre_ffn=g_pre_ffn[l][None],
            g_post_ffn=g_post_ffn[l][None], mla_g_q=mla_g_q[l][None], mla_g_kv=mla_g_kv[l][None],
            w_in_ctx=pw["w_all"][l, :, :C_END_CTX], w_in_lat=pw["w_all"][l],
            w_uq1=pw["w_uq1"][l], w_uq2=pw["w_uq2"][l], w_uk=pw["w_uk"][l], w_uv=pw["w_uv"][l],
            ones3=pw["ones3"], ggq=pw["ggq"][l], ggk=pw["ggk"][l], ggqs=pw["ggqs"][l], ggks=pw["ggks"][l],
            w_out_na=pw["w_out_na"][l], w_out_dn=pw["w_out_dn"][l],
            w_up=w_up[l], w_down=w_down[l], conv_w=ffn_conv_w[l], conv_b=ffn_conv_b[l][None])
        mods_ctx = mods_all[l, 0:1]
        mods_lat = mods_all[l, 1:1 + dbatch]

        qna, kna, vna, qm, ckv, kr, km, vm, qg, kg, vg = _proj(xp, mods_ctx, lw, None, latent=False, seq=seq)
        for lst, a in zip(caches, (kna, vna, ckv, kr, kg, vg)):
            lst.append(a)
        ona, odn = _ctx_attn(qna, kna, vna, qm, km, vm, qg, kg, vg, batch=batch, seq=seq)
        x1, h2 = _post(ona, odn, xp, mods_ctx, lw, seq_per_mod=None)
        xp = _ffn(h2, x1, mods_ctx, lw, seq=seq, seq_per_mod=None)

        qna, kna, vna, qm, km, vm, qg, kg, vg = _proj(xs, mods_lat, lw, rope, latent=True, seq=dseq)
        ona = _na_attn(qna, kna, vna, kc_na, vc_na, na_pairs, l, batch=dbatch, seq=dseq)
        odn = _dense_attn(qm, qg, kc_m, vc_m, kc_g, vc_g, km, vm, kg, vg, l, batch=dbatch, seq=dseq)
        x1, h2 = _post(ona, odn, xs, mods_lat, lw, seq_per_mod=dseq)
        xs = _ffn(h2, x1, mods_lat, lw, seq=dseq, seq_per_mod=dseq)

    def stack(lst, tail):
        return jnp.stack([a.reshape((batch, seq) + tail) for a in lst], axis=1)

    return (xp.reshape(batch, seq, d), xs.reshape(dbatch, dseq, d),
            stack(caches[0], (NA_HEADS, HEAD_DIM)), stack(caches[1], (NA_HEADS, HEAD_DIM)),
            stack(caches[2], (MLA_KV_LORA,)), stack(caches[3], (MLA_ROPE,)),
            stack(caches[4], (GQA_KV_HEADS, HEAD_DIM)), stack(caches[5], (GQA_KV_HEADS, HEAD_DIM)))
```

```python
import functools

import numpy as np
import jax
import jax.numpy as jnp
from jax import lax
from jax.experimental import pallas as pl
from jax.experimental.pallas import tpu as pltpu

GRID_W = 64
HEAD_DIM = 64
NA_HEADS = 6
NA_KH = 8
NA_KW = 16
MLA_HEADS = 4
MLA_Q_LORA = 256
MLA_KV_LORA = 128
MLA_NOPE = 64
MLA_ROPE = 32
MLA_V = 64
GQA_HEADS = 6
GQA_KV_HEADS = 2
GQA_GROUP = GQA_HEADS // GQA_KV_HEADS
ROPE_THETA = 10000.0
EPS = 1e-6

NA_WIDTH = NA_HEADS * HEAD_DIM
MLA_WIDTH = MLA_HEADS * MLA_V
GQA_WIDTH = GQA_HEADS * HEAD_DIM
DENSE_WIDTH = MLA_WIDTH + GQA_WIDTH

LANES = 128
VMEM_LIMIT = 52 * 1024 * 1024
MASK_VALUE = -1e30
LOG2_E = 1.4426950408889634

PROJ_TILE = 512
FFN_TILE = 512
FF_CHUNK = 256
DENSE_Q_TILE = 256
DENSE_K_TILE = 512
NA_ROWS = 4
NA_KEY_ROWS = NA_ROWS + NA_KH
NA_PAD = NA_KEY_ROWS - NA_KH

C_QNA = 0
C_KNA = C_QNA + NA_WIDTH
C_VNA = C_KNA + NA_WIDTH
C_CQ = C_VNA + NA_WIDTH
C_CKV = C_CQ + MLA_Q_LORA
C_QG = C_CKV + MLA_KV_LORA
C_KG = C_QG + GQA_WIDTH
C_VG = C_KG + LANES
C_KR = C_VG + LANES
C_END_CTX = C_KR + LANES
C_KRS = C_END_CTX
C_QGS = C_KRS + LANES
C_KGS = C_QGS + GQA_WIDTH
C_END_LAT = C_KGS + LANES

GQA_ORDER = (0, 3, 1, 4, 2, 5)


def _dot(a, b):
    return jnp.dot(a, b, preferred_element_type=jnp.float32)


def _dot_nt(a, b):
    return lax.dot_general(a, b, (((1,), (1,)), ((), ())), preferred_element_type=jnp.float32)


def _params(*sem):
    return pltpu.CompilerParams(dimension_semantics=sem, vmem_limit_bytes=VMEM_LIMIT)


def _rms(x, g):
    return x * lax.rsqrt(jnp.mean(x * x, axis=-1, keepdims=True) + EPS) * g


def _head_mean_sq(x, ones_bf16):
    xx = x * x
    hi = xx.astype(jnp.bfloat16)
    lo = (xx - hi.astype(jnp.float32)).astype(jnp.bfloat16)
    return _dot(hi, ones_bf16) + _dot(lo, ones_bf16)


def _ada_kernel(cond_ref, w_ref, b_ref, o_ref):
    cnd = cond_ref[...]
    s = cnd / (1.0 + jnp.exp(-cnd))
    o_ref[0] = jnp.dot(s, w_ref[0], preferred_element_type=jnp.float32,
                       precision=lax.Precision.HIGHEST) + b_ref[0]


def _ada_mods(cond, w_ada, b_ada):
    nl, d, n6 = w_ada.shape
    tn = 1536
    return pl.pallas_call(
        _ada_kernel,
        grid=(nl, n6 // tn),
        in_specs=[pl.BlockSpec((8, d), lambda l, j: (0, 0)),
                  pl.BlockSpec((1, d, tn), lambda l, j: (l, 0, j)),
                  pl.BlockSpec((1, 1, tn), lambda l, j: (l, 0, j))],
        out_specs=pl.BlockSpec((1, 8, tn), lambda l, j: (l, 0, j)),
        out_shape=jax.ShapeDtypeStruct((nl, 8, n6), jnp.float32),
        compiler_params=_params("arbitrary", "arbitrary"),
        name="ada_mods",
    )(cond, w_ada, b_ada.reshape(nl, 1, n6))


def _proj_kernel(*refs, latent):
    if latent:
        (x_ref, mod_ref, gpre_ref, w_ref, gq_ref, wuq1_ref, wuq2_ref, gkv_ref, wuk_ref, wuv_ref,
         ones3_ref, ggq_ref, ggk_ref, ggqs_ref, ggks_ref, cg_ref, sg_ref, cm_ref, sm_ref,
         qna_ref, kna_ref, vna_ref, qm_ref, km_ref, vm_ref, qg_ref, kg_ref, vg_ref) = refs
    else:
        (x_ref, mod_ref, gpre_ref, w_ref, gq_ref, wuq1_ref, gkv_ref, wuk_ref, wuv_ref,
         ones3_ref, ggq_ref, ggk_ref,
         qna_ref, kna_ref, vna_ref, qm_ref, ckv_ref, kr_ref, km_ref, vm_ref, qg_ref, kg_ref, vg_ref) = refs

    x = x_ref[...]
    mod = mod_ref[0]
    sh, sc = mod[0:1, :], mod[1:2, :]
    h = (_rms(x, gpre_ref[...]) * (1.0 + sc) + sh).astype(jnp.bfloat16)

    def proj(c0, width):
        return _dot(h, w_ref[:, c0:c0 + width])

    na_scale = HEAD_DIM ** -0.5
    dense_unit = LOG2_E if latent else 1.0
    qna_ref[...] = (proj(C_QNA, NA_WIDTH) * na_scale).astype(qna_ref.dtype)
    kna_ref[...] = proj(C_KNA, NA_WIDTH).astype(kna_ref.dtype)
    vna_ref[...] = proj(C_VNA, NA_WIDTH).astype(vna_ref.dtype)

    mla_scale = (MLA_NOPE + MLA_ROPE) ** -0.5
    cqn = _rms(proj(C_CQ, MLA_Q_LORA), gq_ref[...]).astype(jnp.bfloat16)
    qm = _dot(cqn, wuq1_ref[...])
    kr = proj(C_KR, LANES)
    if latent:
        cm = jnp.concatenate([cm_ref[...]] * MLA_HEADS, axis=1)
        sm = jnp.concatenate([sm_ref[...]] * MLA_HEADS, axis=1)
        qm = qm * cm + _dot(cqn, wuq2_ref[...]) * sm
        kr = kr * cm_ref[...] + proj(C_KRS, LANES) * sm_ref[...]
    qm_ref[...] = (qm * (mla_scale * dense_unit)).astype(qm_ref.dtype)
    ckv = _rms(proj(C_CKV, MLA_KV_LORA), gkv_ref[...])
    ckv_b = ckv.astype(jnp.bfloat16)
    kn = _dot(ckv_b, wuk_ref[...])
    km_ref[...] = (kn + jnp.concatenate([kr] * MLA_HEADS, axis=1)).astype(km_ref.dtype)
    vm_ref[...] = _dot(ckv_b, wuv_ref[...]).astype(vm_ref.dtype)
    if not latent:
        ckv_ref[...] = ckv
        kr_ref[...] = kr[:, MLA_NOPE:MLA_NOPE + MLA_ROPE]

    ones3 = ones3_ref[...]
    ones1 = ones3[0:LANES, 0:LANES]
    qg = proj(C_QG, GQA_WIDTH)
    rq = lax.rsqrt(_head_mean_sq(qg, ones3) + EPS)
    qg = qg * rq * ggq_ref[...]
    kg = proj(C_KG, LANES)
    rk = lax.rsqrt(_head_mean_sq(kg, ones1) + EPS)
    kg = kg * rk * ggk_ref[...]
    if latent:
        cg, sg = cg_ref[...], sg_ref[...]
        cg3 = jnp.concatenate([cg] * (GQA_WIDTH // LANES), axis=1)
        sg3 = jnp.concatenate([sg] * (GQA_WIDTH // LANES), axis=1)
        qg = qg * cg3 + (proj(C_QGS, GQA_WIDTH) * rq * ggqs_ref[...]) * sg3
        kg = kg * cg + (proj(C_KGS, LANES) * rk * ggks_ref[...]) * sg
    qg_ref[...] = (qg * (na_scale * dense_unit)).astype(qg_ref.dtype)
    kg_ref[...] = kg.astype(kg_ref.dtype)
    vg_ref[...] = proj(C_VG, LANES).astype(vg_ref.dtype)


def _layer_spec(entry, cols=None, buffered=False):
    a, layer = entry
    kw = dict(pipeline_mode=pl.Buffered(1)) if buffered else {}
    if layer is None:
        return pl.BlockSpec(a.shape, lambda *_: (0,) * a.ndim, **kw)
    shape = a.shape[1:] if cols is None else a.shape[1:-1] + (cols,)
    return pl.BlockSpec((None,) + shape, lambda *_: (layer,) + (0,) * (a.ndim - 1), **kw)


def _proj(x, mods, lw, rope, *, latent, seq):
    ntok, d = x.shape
    t = PROJ_TILE
    nt = ntok // t
    tiles_per_seq = seq // t if latent else 1
    tok = lambda w: pl.BlockSpec((t, w), lambda i: (i, 0))
    full = _layer_spec
    if latent:
        mod_spec = pl.BlockSpec((1, 6, d), lambda i: (i // tiles_per_seq, 0, 0))
    else:
        mod_spec = pl.BlockSpec((1, 6, d), lambda i: (0, 0, 0))
    kv_dt = jnp.bfloat16 if latent else jnp.float32
    bf = jnp.bfloat16
    sds = lambda w, dt: jax.ShapeDtypeStruct((ntok, w), dt)
    if latent:
        rope_spec = pl.BlockSpec((t, LANES), lambda i: (i % tiles_per_seq, 0))
        ins = [x, mods, lw["g_pre_mix"], lw["w_in_lat"], lw["mla_g_q"], lw["w_uq1"], lw["w_uq2"], lw["mla_g_kv"],
               lw["w_uk"], lw["w_uv"], lw["ones3"], lw["ggq"], lw["ggk"], lw["ggqs"], lw["ggks"],
               rope["cg"], rope["sg"], rope["cm"], rope["sm"]]
        in_specs = [tok(d), mod_spec] + [full(a) for a in ins[2:15]] + [rope_spec] * 4
        in_specs[3] = _layer_spec(lw["w_in_lat"], cols=C_END_LAT)
        outs = [sds(NA_WIDTH, bf), sds(NA_WIDTH, kv_dt), sds(NA_WIDTH, kv_dt), sds(MLA_HEADS * LANES, bf),
                sds(MLA_HEADS * LANES, bf), sds(MLA_WIDTH, bf), sds(GQA_WIDTH, bf), sds(LANES, kv_dt), sds(LANES, kv_dt)]
    else:
        ins = [x, mods, lw["g_pre_mix"], lw["w_in_ctx"], lw["mla_g_q"], lw["w_uq1"], lw["mla_g_kv"],
               lw["w_uk"], lw["w_uv"], lw["ones3"], lw["ggq"], lw["ggk"]]
        in_specs = [tok(d), mod_spec] + [full(a) for a in ins[2:]]
        in_specs[3] = _layer_spec(lw["w_in_ctx"], cols=C_END_CTX)
        outs = [sds(NA_WIDTH, bf), sds(NA_WIDTH, kv_dt), sds(NA_WIDTH, kv_dt), sds(MLA_HEADS * LANES, bf),
                sds(MLA_KV_LORA, jnp.float32), sds(MLA_ROPE, jnp.float32),
                sds(MLA_HEADS * LANES, bf), sds(MLA_WIDTH, bf), sds(GQA_WIDTH, bf), sds(LANES, kv_dt), sds(LANES, kv_dt)]
    out_specs = [tok(o.shape[1]) for o in outs]
    return pl.pallas_call(
        functools.partial(_proj_kernel, latent=latent),
        grid=(nt,),
        in_specs=in_specs,
        out_specs=out_specs,
        out_shape=outs,
        compiler_params=_params("arbitrary"),
        name="proj_lat" if latent else "proj_ctx",
    )(*[a[0] if isinstance(a, tuple) else a for a in ins])


def _lane_lo():
    return lax.broadcasted_iota(jnp.int32, (1, LANES), 1) < HEAD_DIM


def _split_heads(q):
    lo = _lane_lo()
    zero = jnp.zeros_like(q)
    return jnp.where(lo, q, zero), jnp.where(lo, zero, q)


def _softmax_pv(scores, values):
    m = scores[0].max(axis=-1, keepdims=True)
    for s in scores[1:]:
        m = jnp.maximum(m, s.max(axis=-1, keepdims=True))
    l = None
    acc = None
    for s, v in zip(scores, values):
        p = jnp.exp(s - m)
        ps = p.sum(axis=-1, keepdims=True)
        pv = _dot(p.astype(jnp.bfloat16), v)
        l = ps if l is None else l + ps
        acc = pv if acc is None else acc + pv
    return acc / l


def _ctx_attn_kernel(qna_ref, kna_ref, vna_ref, qm_ref, km_ref, vm_ref, qg_ref, kg_ref, vg_ref, ona_ref, odn_ref):
    lo = _lane_lo()
    bf = jnp.bfloat16

    def pair(q_lo, q_hi, k_lo, k_hi, v):
        o_lo = _softmax_pv([_dot_nt(q_lo, k_lo)], [v])
        o_hi = _softmax_pv([_dot_nt(q_hi, k_hi)], [v])
        return jnp.where(lo, o_lo, o_hi)

    for p in range(NA_WIDTH // LANES):
        cs = slice(p * LANES, (p + 1) * LANES)
        q_lo, q_hi = _split_heads(qna_ref[0, :, cs])
        k = kna_ref[0, :, cs].astype(bf)
        ona_ref[0, :, cs] = pair(q_lo, q_hi, k, k, vna_ref[0, :, cs].astype(bf)).astype(ona_ref.dtype)
    for p in range(MLA_HEADS // 2):
        c0 = 2 * p * LANES
        o = pair(qm_ref[0, :, c0:c0 + LANES], qm_ref[0, :, c0 + LANES:c0 + 2 * LANES],
                 km_ref[0, :, c0:c0 + LANES], km_ref[0, :, c0 + LANES:c0 + 2 * LANES],
                 vm_ref[0, :, p * LANES:(p + 1) * LANES])
        odn_ref[0, :, p * LANES:(p + 1) * LANES] = o.astype(odn_ref.dtype)
    kg = kg_ref[0].astype(bf)
    vg = vg_ref[0].astype(bf)
    for c in range(GQA_WIDTH // LANES):
        q_lo, q_hi = _split_heads(qg_ref[0, :, c * LANES:(c + 1) * LANES])
        o = pair(q_lo, q_hi, kg, kg, vg)
        odn_ref[0, :, MLA_WIDTH + c * LANES:MLA_WIDTH + (c + 1) * LANES] = o.astype(odn_ref.dtype)


def _ctx_attn(qna, kna, vna, qm, km, vm, qg, kg, vg, *, batch, seq):
    ins = [a.reshape(batch, seq, a.shape[-1]) for a in (qna, kna, vna, qm, km, vm, qg, kg, vg)]
    spec = lambda a: pl.BlockSpec((1, seq, a.shape[-1]), lambda b: (b, 0, 0))
    outs = [jax.ShapeDtypeStruct((batch, seq, NA_WIDTH), jnp.bfloat16),
            jax.ShapeDtypeStruct((batch, seq, DENSE_WIDTH), jnp.bfloat16)]
    ona, odn = pl.pallas_call(
        _ctx_attn_kernel,
        grid=(batch,),
        in_specs=[spec(a) for a in ins],
        out_specs=[spec(o) for o in outs],
        out_shape=outs,
        compiler_params=_params("arbitrary"),
        name="ctx_attn",
    )(*ins)
    return ona.reshape(batch * seq, NA_WIDTH), odn.reshape(batch * seq, DENSE_WIDTH)


def _na_attn_kernel(q_ref, k_ref, v_ref, kc_ref, vc_ref, pair_ref, o_ref, *, rows):
    lo = _lane_lo()
    blk = pl.program_id(1)
    row0 = blk * NA_ROWS
    key_row0 = jnp.clip(row0 - NA_KH // 2, 0, rows - NA_KEY_ROWS)
    start = pl.multiple_of(key_row0 * GRID_W, GRID_W)
    tq = NA_ROWS * GRID_W
    nkeys = NA_KEY_ROWS * GRID_W

    rq = row0 + lax.shift_right_logical(lax.broadcasted_iota(jnp.int32, (tq, 1), 0), 6)
    rk = key_row0 + lax.shift_right_logical(lax.broadcasted_iota(jnp.int32, (1, nkeys), 1), 6)
    rs = jnp.clip(rq - NA_KH // 2, 0, rows - NA_KH)
    row_mask = jnp.where((rk >= rs) & (rk < rs + NA_KH), 0.0, MASK_VALUE)

    def bias(h):
        base = key_row0 - row0 + (NA_KH - 1) + NA_PAD
        blocks = [jnp.concatenate([pair_ref[0, h, base + 2 * m - dq] for m in range(NA_KEY_ROWS // 2)], axis=1)
                  for dq in range(NA_ROWS)]
        return jnp.concatenate(blocks, axis=0) + row_mask

    for p in range(NA_WIDTH // LANES):
        cs = slice(p * LANES, (p + 1) * LANES)
        q_lo, q_hi = _split_heads(q_ref[0, :, cs])
        k = k_ref[0, pl.ds(start, nkeys), cs]
        v = v_ref[0, pl.ds(start, nkeys), cs]
        kc = kc_ref[0, 0, :, cs]
        vc = vc_ref[0, 0, :, cs]
        o_lo = _softmax_pv([_dot_nt(q_lo, k) + bias(2 * p), _dot_nt(q_lo, kc)], [v, vc])
        o_hi = _softmax_pv([_dot_nt(q_hi, k) + bias(2 * p + 1), _dot_nt(q_hi, kc)], [v, vc])
        o_ref[0, :, cs] = jnp.where(lo, o_lo, o_hi).astype(o_ref.dtype)


def _na_attn(q, k, v, kc, vc, pairs, layer, *, batch, seq):
    rows = seq // GRID_W
    nblk = rows // NA_ROWS
    tq = NA_ROWS * GRID_W
    past = kc.shape[2]
    assert GRID_W == 64 and rows >= NA_KEY_ROWS and rows % NA_ROWS == 0
    q3, k3, v3 = (a.reshape(batch, seq, NA_WIDTH) for a in (q, k, v))
    o = pl.pallas_call(
        functools.partial(_na_attn_kernel, rows=rows),
        grid=(batch, nblk),
        in_specs=[pl.BlockSpec((1, tq, NA_WIDTH), lambda b, i: (b, i, 0)),
                  pl.BlockSpec((1, seq, NA_WIDTH), lambda b, i: (b, 0, 0)),
                  pl.BlockSpec((1, seq, NA_WIDTH), lambda b, i: (b, 0, 0)),
                  pl.BlockSpec((1, 1, past, NA_WIDTH), lambda b, i: (b, layer, 0, 0)),
                  pl.BlockSpec((1, 1, past, NA_WIDTH), lambda b, i: (b, layer, 0, 0)),
                  pl.BlockSpec((1,) + pairs.shape[1:], lambda b, i: (layer, 0, 0, 0, 0))],
        out_specs=pl.BlockSpec((1, tq, NA_WIDTH), lambda b, i: (b, i, 0)),
        out_shape=jax.ShapeDtypeStruct((batch, seq, NA_WIDTH), jnp.bfloat16),
        compiler_params=_params("arbitrary", "arbitrary"),
        name="na_attn",
    )(q3, k3, v3, kc, vc, pairs)
    return o.reshape(batch * seq, NA_WIDTH)


def _na_pair_tables(na_rpb):
    cq = np.arange(GRID_W)[:, None]
    ck = np.arange(GRID_W)[None, :]
    cs = np.clip(cq - NA_KW // 2, 0, GRID_W - NA_KW)
    valid = (ck >= cs) & (ck < cs + NA_KW)
    onehot = ((ck - cq + NA_KW - 1)[None] == np.arange(2 * NA_KW - 1)[:, None, None]) & valid[None]
    t = jnp.einsum("lhdo,oqk->lhdqk", na_rpb, jnp.asarray(onehot, na_rpb.dtype), precision=lax.Precision.HIGHEST)
    t = jnp.where(jnp.asarray(valid), t, MASK_VALUE)
    t = jnp.pad(t, ((0, 0), (0, 0), (NA_PAD, NA_PAD), (0, 0), (0, 0)), constant_values=MASK_VALUE)
    return jnp.concatenate([t[:, :, :-1], t[:, :, 1:]], axis=-1)


def _dense_attn_kernel(qm_ref, qg_ref, kmc_ref, vmc_ref, kgc_ref, vgc_ref, kml_ref, vml_ref, kgl_ref, vgl_ref,
                       o_ref, s_lo_ref, s_hi_ref, *, nk, tk):
    lo = _lane_lo()
    bf = jnp.bfloat16
    past = kmc_ref.shape[2]

    def lane_fold(x, op):
        parts = [x[:, c * LANES:(c + 1) * LANES] for c in range(x.shape[1] // LANES)]
        while len(parts) > 1:
            parts = [op(parts[i], parts[i + 1]) if i + 1 < len(parts) else parts[i] for i in range(0, len(parts), 2)]
        return parts[0]

    def head(q, kc, vc, k_at, v_at, s_ref):
        spans = [(0, past)] + [(past + j * tk, tk) for j in range(nk)]
        mx = None
        for idx, (c0, width) in enumerate(spans):
            s = _dot_nt(q, kc if idx == 0 else k_at(idx - 1))
            s_ref[:, c0:c0 + width] = s
            part = lane_fold(s, jnp.maximum)
            mx = part if mx is None else jnp.maximum(mx, part)
        m = mx.max(axis=-1, keepdims=True)
        lsum = None
        acc = None
        for idx, (c0, width) in enumerate(spans):
            p = jnp.exp2(s_ref[:, c0:c0 + width] - m)
            part = lane_fold(p, jnp.add)
            pv = _dot(p.astype(bf), vc if idx == 0 else v_at(idx - 1))
            lsum = part if lsum is None else lsum + part
            acc = pv if acc is None else acc + pv
        return acc / lsum.sum(axis=-1, keepdims=True)

    def unit(q_lo, q_hi, kc_lo, kc_hi, vc, k_lo_at, k_hi_at, v_at):
        o_lo = head(q_lo, kc_lo, vc, k_lo_at, v_at, s_lo_ref)
        o_hi = head(q_hi, kc_hi, vc, k_hi_at, v_at, s_hi_ref)
        return jnp.where(lo, o_lo, o_hi)

    for p in range(MLA_HEADS // 2):
        c0 = 2 * p * LANES
        c1 = c0 + LANES
        o = unit(qm_ref[0, :, c0:c1], qm_ref[0, :, c1:c1 + LANES],
                 kmc_ref[0, 0, :, c0:c1], kmc_ref[0, 0, :, c1:c1 + LANES], vmc_ref[0, 0, :, p * LANES:(p + 1) * LANES],
                 lambda j, c0=c0, c1=c1: kml_ref[0, j * tk:(j + 1) * tk, c0:c1],
                 lambda j, c1=c1: kml_ref[0, j * tk:(j + 1) * tk, c1:c1 + LANES],
                 lambda j, p=p: vml_ref[0, j * tk:(j + 1) * tk, p * LANES:(p + 1) * LANES])
        o_ref[0, :, p * LANES:(p + 1) * LANES] = o.astype(o_ref.dtype)
    kgc = kgc_ref[0, 0]
    vgc = vgc_ref[0, 0]
    kg_at = lambda j: kgl_ref[0, j * tk:(j + 1) * tk, :]
    vg_at = lambda j: vgl_ref[0, j * tk:(j + 1) * tk, :]
    for c in range(GQA_WIDTH // LANES):
        q_lo, q_hi = _split_heads(qg_ref[0, :, c * LANES:(c + 1) * LANES])
        o = unit(q_lo, q_hi, kgc, kgc, vgc, kg_at, kg_at, vg_at)
        o_ref[0, :, MLA_WIDTH + c * LANES:MLA_WIDTH + (c + 1) * LANES] = o.astype(o_ref.dtype)


def _dense_attn(qm, qg, kmc, vmc, kgc, vgc, kml, vml, kgl, vgl, layer, *, batch, seq):
    tq, tk = DENSE_Q_TILE, DENSE_K_TILE
    r3 = lambda a: a.reshape(batch, seq, a.shape[-1])
    qm, qg, kml, vml, kgl, vgl = (r3(a) for a in (qm, qg, kml, vml, kgl, vgl))
    qspec = lambda a: pl.BlockSpec((1, tq, a.shape[-1]), lambda b, i: (b, i, 0))
    cspec = lambda a: pl.BlockSpec((1, 1) + a.shape[2:], lambda b, i: (b, layer, 0, 0))
    lspec = lambda a: pl.BlockSpec((1, seq, a.shape[-1]), lambda b, i: (b, 0, 0), pipeline_mode=pl.Buffered(1))
    o = pl.pallas_call(
        functools.partial(_dense_attn_kernel, nk=seq // tk, tk=tk),
        grid=(batch, seq // tq),
        in_specs=[qspec(qm), qspec(qg), cspec(kmc), cspec(vmc), cspec(kgc), cspec(vgc),
                  lspec(kml), lspec(vml), lspec(kgl), lspec(vgl)],
        out_specs=pl.BlockSpec((1, tq, DENSE_WIDTH), lambda b, i: (b, i, 0)),
        out_shape=jax.ShapeDtypeStruct((batch, seq, DENSE_WIDTH), jnp.bfloat16),
        scratch_shapes=[pltpu.VMEM((tq, kmc.shape[2] + seq), jnp.float32)] * 2,
        compiler_params=_params("arbitrary", "arbitrary"),
        name="dense_attn",
    )(qm, qg, kmc, vmc, kgc, vgc, kml, vml, kgl, vgl)
    return o.reshape(batch * seq, DENSE_WIDTH)


def _mla_ctx_kernel(ckv_ref, kr_ref, wuk_ref, wuv_ref, k_ref, v_ref):
    ckv = ckv_ref[0, 0]
    kn = _dot(ckv, wuk_ref[0])
    k_ref[0, 0] = (kn + jnp.concatenate([kr_ref[0, 0].astype(jnp.float32)] * MLA_HEADS, axis=1)).astype(k_ref.dtype)
    v_ref[0, 0] = _dot(ckv, wuv_ref[0]).astype(v_ref.dtype)


def _mla_ctx_expand(ckv, kr_pad, wuk, wuv):
    b, nl, past, _ = ckv.shape
    spec = lambda w: pl.BlockSpec((1, 1, past, w), lambda i, l: (i, l, 0, 0))
    wspec = lambda a: pl.BlockSpec((1,) + a.shape[1:], lambda i, l: (l, 0, 0))
    return pl.pallas_call(
        _mla_ctx_kernel,
        grid=(b, nl),
        in_specs=[spec(MLA_KV_LORA), spec(LANES), wspec(wuk), wspec(wuv)],
        out_specs=[spec(MLA_HEADS * LANES), spec(MLA_WIDTH)],
        out_shape=[jax.ShapeDtypeStruct((b, nl, past, MLA_HEADS * LANES), jnp.bfloat16),
                   jax.ShapeDtypeStruct((b, nl, past, MLA_WIDTH), jnp.bfloat16)],
        compiler_params=_params("arbitrary", "arbitrary"),
        name="mla_ctx_expand",
    )(ckv, kr_pad, wuk, wuv)


def _mix_ffn_kernel(ona_ref, odn_ref, x_ref, onap_ref, odnp_ref, xp_ref, onan_ref, odnn_ref, xn_ref, mod_ref,
                    wna_ref, wdn_ref, gpost_ref, gpre_ref, wup_ref, cw_ref, cb_ref, wd_ref, gffn_ref,
                    o_ref, hperm_ref, yperm_ref, *, seq, chunk):
    i = pl.program_id(0)
    t, d = x_ref.shape
    halo = xp_ref.shape[0]
    dff = wd_ref.shape[0]
    groups = t // 8
    mod = mod_ref[0]
    gt_m, sh_f, sc_f, gt_f = mod[2:3, :], mod[3:4, :], mod[4:5, :], mod[5:6, :]

    def mix(ona, odn, x):
        o = _dot(ona, wna_ref[...]) + _dot(odn, wdn_ref[...])
        x1 = x + gt_m * _rms(o, gpost_ref[...])
        return x1, _rms(x1, gpre_ref[...]) * (1.0 + sc_f) + sh_f

    x1, h2 = mix(ona_ref[...], odn_ref[...], x_ref[...])
    o_ref[...] = x1
    _, h2p = mix(onap_ref[...], odnp_ref[...], xp_ref[...])
    _, h2n = mix(onan_ref[...], odnn_ref[...], xn_ref[...])
    hp = h2p.astype(jnp.bfloat16)
    hn = h2n.astype(jnp.bfloat16)

    nlb = d // LANES
    for j in range(nlb):
        for k in range(8):
            hperm_ref[j, pl.ds(k, groups, stride=8), :] = h2[k * groups:(k + 1) * groups, j * LANES:(j + 1) * LANES]
    h = jnp.concatenate([hperm_ref[j] for j in range(nlb)], axis=1).astype(jnp.bfloat16)

    kk = lax.broadcasted_iota(jnp.int32, (8, 1), 0)
    tok = i * t + groups * kk
    prev_zero = jnp.bitwise_and(tok, seq - 1) == 0
    next_zero = jnp.bitwise_and(tok + groups, seq - 1) == 0

    def conv(c0):
        w = wup_ref[:, c0:c0 + chunk]
        u = _dot(h, w)
        u_hp = _dot(hp, w)[halo - 1:halo, :]
        u_hn = _dot(hn, w)[0:1, :]
        b_prev = jnp.where(kk == 0, u_hp, pltpu.roll(u[t - 8:t, :], 1, 0))
        b_prev = jnp.where(prev_zero, 0.0, b_prev)
        b_next = jnp.where(kk == 7, u_hn, pltpu.roll(u[0:8, :], 7, 0))
        b_next = jnp.where(next_zero, 0.0, b_next)
        prev = jnp.concatenate([b_prev, u[0:t - 8, :]], axis=0)
        nxt = jnp.concatenate([u[8:t, :], b_next], axis=0)
        cw = cw_ref[:, c0:c0 + chunk]
        return prev * cw[0:1, :] + u * cw[1:2, :] + nxt * cw[2:3, :] + cb_ref[:, c0:c0 + chunk]

    acc = None
    for c in range(dff // chunk):
        gate = conv(c * chunk)
        val = conv(dff + c * chunk)
        act = (gate / (1.0 + jnp.exp(-gate)) * val).astype(jnp.bfloat16)
        part = _dot(act, wd_ref[c * chunk:(c + 1) * chunk, :])
        acc = part if acc is None else acc + part

    y = gt_f * _rms(acc, gffn_ref[...])
    for j in range(nlb):
        yperm_ref[j] = y[:, j * LANES:(j + 1) * LANES]
    for k in range(8):
        rows = slice(k * groups, (k + 1) * groups)
        yk = jnp.concatenate([yperm_ref[j, pl.ds(k, groups, stride=8), :] for j in range(nlb)], axis=1)
        o_ref[rows, :] = o_ref[rows, :] + yk


def _mix_ffn(ona, odn, x, mods, lw, *, seq, seq_per_mod):
    ntok, d = x.shape
    t = FFN_TILE
    halo = 16
    assert seq & (seq - 1) == 0 and (t % seq == 0 or seq % t == 0) and ntok % t == 0 and t % 64 == 0
    nhalo = ntok // halo
    if seq_per_mod is None:
        mod_spec = pl.BlockSpec((1, 6, d), lambda i: (0, 0, 0))
    else:
        tps = seq_per_mod // t
        mod_spec = pl.BlockSpec((1, 6, d), lambda i: (i // tps, 0, 0))
    tok = lambda w: pl.BlockSpec((t, w), lambda i: (i, 0))
    prv = lambda w: pl.BlockSpec((halo, w), lambda i: (jnp.maximum(i * (t // halo) - 1, 0), 0))
    nxt = lambda w: pl.BlockSpec((halo, w), lambda i: (jnp.minimum((i + 1) * (t // halo), nhalo - 1), 0))
    full = lambda e: _layer_spec(e, buffered=True)
    weights = [lw["w_out_na"], lw["w_out_dn"], lw["g_post_mix"], lw["g_pre_ffn"], lw["w_up"], lw["conv_w"],
               lw["conv_b"], lw["w_down"], lw["g_post_ffn"]]
    widths = (NA_WIDTH, DENSE_WIDTH, d)
    return pl.pallas_call(
        functools.partial(_mix_ffn_kernel, seq=seq, chunk=FF_CHUNK),
        grid=(ntok // t,),
        in_specs=[tok(w) for w in widths] + [prv(w) for w in widths] + [nxt(w) for w in widths] + [mod_spec]
                 + [full(a) for a in weights],
        out_specs=tok(d),
        out_shape=jax.ShapeDtypeStruct((ntok, d), jnp.float32),
        scratch_shapes=[pltpu.VMEM((d // LANES, t, LANES), jnp.float32)] * 2,
        compiler_params=_params("arbitrary"),
        name="mix_ffn",
    )(ona, odn, x, ona, odn, x, ona, odn, x, mods, *[w[0] for w in weights])


def _swap_cols(w, group):
    s = w.shape
    return jnp.flip(w.reshape(s[:-1] + (s[-1] // group, 2, group // 2)), axis=-2).reshape(s)


def _prep_weights(w_in, mla_w_uq, mla_w_ukv, w_out, gqa_g_q, gqa_g_k):
    nl, d, _ = w_in.shape
    bf = jnp.bfloat16
    cat = lambda parts: jnp.concatenate(parts, axis=-1)
    o_kr = 3 * NA_WIDTH + MLA_Q_LORA + MLA_KV_LORA
    o_qg = o_kr + MLA_ROPE
    o_kg = o_qg + GQA_WIDTH
    o_vg = o_kg + GQA_KV_HEADS * HEAD_DIM
    pad_r = LANES - MLA_NOPE - MLA_ROPE
    wb = w_in.astype(bf)
    z = lambda rows, n: jnp.zeros((nl, rows, n), bf)
    kr = wb[..., o_kr:o_qg]
    qg = cat([wb[..., o_qg + HEAD_DIM * h:o_qg + HEAD_DIM * (h + 1)] for h in GQA_ORDER])
    kg = wb[..., o_kg:o_vg]
    w_all = cat([wb[..., :o_kr], qg, kg, wb[..., o_vg:],
                 z(d, MLA_NOPE), kr, z(d, pad_r),
                 z(d, MLA_NOPE), _swap_cols(kr, MLA_ROPE // 2), z(d, pad_r),
                 _swap_cols(qg, HEAD_DIM // 2), _swap_cols(kg, HEAD_DIM // 2)])
    assert w_all.shape[-1] == C_END_LAT

    uq = mla_w_uq.astype(bf)
    ukv = mla_w_ukv.astype(bf)
    ql, kl = uq.shape[1], ukv.shape[1]
    qw = MLA_NOPE + MLA_ROPE
    kvw = MLA_NOPE + MLA_V
    w_uq1 = cat([p for h in range(MLA_HEADS) for p in (uq[..., qw * h:qw * (h + 1)], z(ql, pad_r))])
    w_uq2 = cat([p for h in range(MLA_HEADS) for p in
                 (z(ql, MLA_NOPE), _swap_cols(uq[..., qw * h + MLA_NOPE:qw * (h + 1)], MLA_ROPE // 2), z(ql, pad_r))])
    w_uk = cat([p for h in range(MLA_HEADS) for p in (ukv[..., kvw * h:kvw * h + MLA_NOPE], z(kl, LANES - MLA_NOPE))])
    w_uv = cat([ukv[..., kvw * h + MLA_NOPE:kvw * (h + 1)] for h in range(MLA_HEADS)])

    wo = w_out.astype(bf)
    o_gqa = NA_WIDTH + MLA_WIDTH
    w_out_dn = jnp.concatenate([wo[:, NA_WIDTH:o_gqa]]
                               + [wo[:, o_gqa + HEAD_DIM * h:o_gqa + HEAD_DIM * (h + 1)] for h in GQA_ORDER], axis=1)

    ggq = jnp.tile(gqa_g_q, (1, GQA_HEADS))[:, None, :]
    ggk = jnp.tile(gqa_g_k, (1, GQA_KV_HEADS))[:, None, :]
    ggqs = jnp.tile(_swap_cols(gqa_g_q, HEAD_DIM // 2), (1, GQA_HEADS))[:, None, :]
    ggks = jnp.tile(_swap_cols(gqa_g_k, HEAD_DIM // 2), (1, GQA_KV_HEADS))[:, None, :]
    blk = np.kron(np.eye(GQA_WIDTH // HEAD_DIM), np.full((HEAD_DIM, HEAD_DIM), 1.0 / HEAD_DIM))
    ones3 = jnp.asarray(blk, bf)
    return dict(w_all=w_all, w_uq1=w_uq1, w_uq2=w_uq2, w_uk=w_uk, w_uv=w_uv, w_out_na=wo[:, :NA_WIDTH], w_out_dn=w_out_dn,
                ggq=ggq, ggk=ggk, ggqs=ggqs, ggks=ggks, ones3=ones3)


def _rope_tables(seq):
    t = jnp.arange(seq, dtype=jnp.int32)
    r, c = (t // GRID_W).astype(jnp.float32), (t % GRID_W).astype(jnp.float32)

    def tables(dim):
        quarter = dim // 4
        freqs = ROPE_THETA ** (-jnp.arange(quarter, dtype=jnp.float32) / quarter)
        ar_, ac_ = r[:, None] * freqs[None, :], c[:, None] * freqs[None, :]
        cos = jnp.concatenate([jnp.cos(ar_), jnp.cos(ar_), jnp.cos(ac_), jnp.cos(ac_)], axis=1)
        sin = jnp.concatenate([-jnp.sin(ar_), jnp.sin(ar_), -jnp.sin(ac_), jnp.sin(ac_)], axis=1)
        return cos, sin

    c64, s64 = tables(HEAD_DIM)
    c32, s32 = tables(MLA_ROPE)
    pad = LANES - MLA_NOPE - MLA_ROPE
    cm = jnp.concatenate([jnp.ones((seq, MLA_NOPE)), c32, jnp.zeros((seq, pad))], axis=1)
    sm = jnp.concatenate([jnp.zeros((seq, MLA_NOPE)), s32, jnp.zeros((seq, pad))], axis=1)
    return dict(cg=jnp.tile(c64, (1, 2)), sg=jnp.tile(s64, (1, 2)), cm=cm.astype(jnp.float32), sm=sm.astype(jnp.float32))


def kernel(x_prompt, x_sample, c, cache_na_k, cache_na_v, cache_mla_ckv, cache_mla_krope, cache_gqa_k, cache_gqa_v, c_ctx, w_ada, b_ada, g_pre_mix, g_post_mix, g_pre_ffn, g_post_ffn, w_in, na_rpb, mla_g_q, mla_w_uq, mla_g_kv, mla_w_ukv, gqa_g_q, gqa_g_k, w_out, ffn_w_up, ffn_conv_w, ffn_conv_b, ffn_w_down):
    batch, seq, d = x_prompt.shape
    dbatch, dseq, _ = x_sample.shape
    nl = w_ada.shape[0]
    past = cache_na_k.shape[2]
    bf = jnp.bfloat16
    assert dbatch + 1 <= 8 and dseq % GRID_W == 0

    cond = jnp.zeros((8, d), jnp.float32).at[0].set(c_ctx).at[1:1 + dbatch].set(c)
    mods_all = _ada_mods(cond, w_ada, b_ada).reshape(nl, 8, 6, d)

    pw = _prep_weights(w_in, mla_w_uq, mla_w_ukv, w_out, gqa_g_q, gqa_g_k)
    rope = _rope_tables(dseq)
    vec = lambda g: g[:, None, :]
    stacked = dict(
        g_pre_mix=vec(g_pre_mix), g_post_mix=vec(g_post_mix), g_pre_ffn=vec(g_pre_ffn), g_post_ffn=vec(g_post_ffn),
        mla_g_q=vec(mla_g_q), mla_g_kv=vec(mla_g_kv), w_in_ctx=pw["w_all"], w_in_lat=pw["w_all"],
        w_uq1=pw["w_uq1"], w_uq2=pw["w_uq2"], w_uk=pw["w_uk"], w_uv=pw["w_uv"],
        ggq=pw["ggq"], ggk=pw["ggk"], ggqs=pw["ggqs"], ggks=pw["ggks"],
        w_out_na=pw["w_out_na"], w_out_dn=pw["w_out_dn"],
        w_up=ffn_w_up.astype(bf), w_down=ffn_w_down.astype(bf), conv_w=ffn_conv_w, conv_b=vec(ffn_conv_b))

    kc_na = cache_na_k.reshape(dbatch, nl, past, NA_WIDTH).astype(bf)
    vc_na = cache_na_v.reshape(dbatch, nl, past, NA_WIDTH).astype(bf)
    kc_g = cache_gqa_k.reshape(dbatch, nl, past, GQA_KV_HEADS * HEAD_DIM).astype(bf)
    vc_g = cache_gqa_v.reshape(dbatch, nl, past, GQA_KV_HEADS * HEAD_DIM).astype(bf)
    kr_pad = jnp.pad(cache_mla_krope, ((0, 0), (0, 0), (0, 0), (MLA_NOPE, LANES - MLA_NOPE - MLA_ROPE))).astype(bf)
    kc_m, vc_m = _mla_ctx_expand(cache_mla_ckv.astype(bf), kr_pad, pw["w_uk"], pw["w_uv"])

    na_pairs = _na_pair_tables(na_rpb)

    xp = x_prompt.reshape(batch * seq, d)
    xs = x_sample.reshape(dbatch * dseq, d)
    caches = [[] for _ in range(6)]
    for l in range(nl):
        lw = {k: (v, l) for k, v in stacked.items()}
        lw["ones3"] = (pw["ones3"], None)
        mods_ctx = mods_all[l, 0:1]
        mods_lat = mods_all[l, 1:1 + dbatch]

        qna, kna, vna, qm, ckv, kr, km, vm, qg, kg, vg = _proj(xp, mods_ctx, lw, None, latent=False, seq=seq)
        for lst, a in zip(caches, (kna, vna, ckv, kr, kg, vg)):
            lst.append(a)
        ona, odn = _ctx_attn(qna, kna, vna, qm, km, vm, qg, kg, vg, batch=batch, seq=seq)
        xp = _mix_ffn(ona, odn, xp, mods_ctx, lw, seq=seq, seq_per_mod=None)

        qna, kna, vna, qm, km, vm, qg, kg, vg = _proj(xs, mods_lat, lw, rope, latent=True, seq=dseq)
        ona = _na_attn(qna, kna, vna, kc_na, vc_na, na_pairs, l, batch=dbatch, seq=dseq)
        odn = _dense_attn(qm, qg, kc_m, vc_m, kc_g, vc_g, km, vm, kg, vg, l, batch=dbatch, seq=dseq)
        xs = _mix_ffn(ona, odn, xs, mods_lat, lw, seq=dseq, seq_per_mod=dseq)

    def stack(lst, tail):
        return jnp.stack([a.reshape((batch, seq) + tail) for a in lst], axis=1)

    return (xp.reshape(batch, seq, d), xs.reshape(dbatch, dseq, d),
            stack(caches[0], (NA_HEADS, HEAD_DIM)), stack(caches[1], (NA_HEADS, HEAD_DIM)),
            stack(caches[2], (MLA_KV_LORA,)), stack(caches[3], (MLA_ROPE,)),
            stack(caches[4], (GQA_KV_HEADS, HEAD_DIM)), stack(caches[5], (GQA_KV_HEADS, HEAD_DIM)))
```

```python
import functools

import numpy as np
import jax
import jax.numpy as jnp
from jax import lax
from jax.experimental import pallas as pl
from jax.experimental.pallas import tpu as pltpu

GRID_W = 64
HEAD_DIM = 64
NA_HEADS = 6
NA_KH = 8
NA_KW = 16
MLA_HEADS = 4
MLA_Q_LORA = 256
MLA_KV_LORA = 128
MLA_NOPE = 64
MLA_ROPE = 32
MLA_V = 64
GQA_HEADS = 6
GQA_KV_HEADS = 2
GQA_GROUP = GQA_HEADS // GQA_KV_HEADS
ROPE_THETA = 10000.0
EPS = 1e-6

NA_WIDTH = NA_HEADS * HEAD_DIM
MLA_WIDTH = MLA_HEADS * MLA_V
GQA_WIDTH = GQA_HEADS * HEAD_DIM
DENSE_WIDTH = MLA_WIDTH + GQA_WIDTH

LANES = 128
VMEM_LIMIT = 52 * 1024 * 1024
MASK_VALUE = -1e30
LOG2_E = 1.4426950408889634

PROJ_TILE = 512
FFN_TILE = 512
FF_CHUNK = 256
DENSE_Q_TILE = 256
DENSE_K_TILE = 512
NA_ROWS = 4
NA_KEY_ROWS = NA_ROWS + NA_KH
NA_PAD = NA_KEY_ROWS - NA_KH

C_QNA = 0
C_KNA = C_QNA + NA_WIDTH
C_VNA = C_KNA + NA_WIDTH
C_CQ = C_VNA + NA_WIDTH
C_CKV = C_CQ + MLA_Q_LORA
C_QG = C_CKV + MLA_KV_LORA
C_KG = C_QG + GQA_WIDTH
C_VG = C_KG + LANES
C_KR = C_VG + LANES
C_END = C_KR + LANES
PROJ_GROUP = 512

GQA_ORDER = (0, 3, 1, 4, 2, 5)


def _dot(a, b):
    return jnp.dot(a, b, preferred_element_type=jnp.float32)


def _dot_nt(a, b):
    return lax.dot_general(a, b, (((1,), (1,)), ((), ())), preferred_element_type=jnp.float32)


def _params(*sem):
    return pltpu.CompilerParams(dimension_semantics=sem, vmem_limit_bytes=VMEM_LIMIT)


def _rms(x, g):
    return x * lax.rsqrt(jnp.mean(x * x, axis=-1, keepdims=True) + EPS) * g


def _head_mean_sq(x, ones_bf16):
    xx = x * x
    hi = xx.astype(jnp.bfloat16)
    lo = (xx - hi.astype(jnp.float32)).astype(jnp.bfloat16)
    return _dot(hi, ones_bf16) + _dot(lo, ones_bf16)


def _ada_kernel(cond_ref, w_ref, b_ref, o_ref):
    cnd = cond_ref[...]
    s = cnd / (1.0 + jnp.exp(-cnd))
    o_ref[0] = jnp.dot(s, w_ref[0], preferred_element_type=jnp.float32,
                       precision=lax.Precision.HIGHEST) + b_ref[0]


def _ada_mods(cond, w_ada, b_ada):
    nl, d, n6 = w_ada.shape
    tn = 1536
    return pl.pallas_call(
        _ada_kernel,
        grid=(nl, n6 // tn),
        in_specs=[pl.BlockSpec((8, d), lambda l, j: (0, 0)),
                  pl.BlockSpec((1, d, tn), lambda l, j: (l, 0, j)),
                  pl.BlockSpec((1, 1, tn), lambda l, j: (l, 0, j))],
        out_specs=pl.BlockSpec((1, 8, tn), lambda l, j: (l, 0, j)),
        out_shape=jax.ShapeDtypeStruct((nl, 8, n6), jnp.float32),
        compiler_params=_params("arbitrary", "arbitrary"),
        name="ada_mods",
    )(cond, w_ada, b_ada.reshape(nl, 1, n6))


def _swap_halves(x, half):
    lane = lax.broadcasted_iota(jnp.int32, (1, LANES), 1)
    first = jnp.bitwise_and(lane, 2 * half - 1) < half
    blocks = []
    for b in range(x.shape[1] // LANES):
        xb = x[:, b * LANES:(b + 1) * LANES]
        blocks.append(jnp.where(first, pltpu.roll(xb, LANES - half, 1), pltpu.roll(xb, half, 1)))
    return jnp.concatenate(blocks, axis=1)


def _proj_kernel(*refs, latent):
    if latent:
        (x_ref, mod_ref, gpre_ref, w_ref, gq_ref, wuq_ref, gkv_ref, wuk_ref, wuv_ref,
         ones_ref, ggq_ref, ggk_ref, cg_ref, sg_ref, cm_ref, sm_ref,
         qna_ref, kna_ref, vna_ref, qm_ref, km_ref, vm_ref, qg_ref, kg_ref, vg_ref) = refs
    else:
        (x_ref, mod_ref, gpre_ref, w_ref, gq_ref, wuq_ref, gkv_ref, wuk_ref, wuv_ref,
         ones_ref, ggq_ref, ggk_ref,
         qna_ref, kna_ref, vna_ref, qm_ref, ckv_ref, kr_ref, km_ref, vm_ref, qg_ref, kg_ref, vg_ref) = refs

    x = x_ref[...]
    mod = mod_ref[0]
    sh, sc = mod[0:1, :], mod[1:2, :]
    h = (_rms(x, gpre_ref[...]) * (1.0 + sc) + sh).astype(jnp.bfloat16)

    groups = [(c0, _dot(h, w_ref[:, c0:min(c0 + PROJ_GROUP, C_END)])) for c0 in range(0, C_END, PROJ_GROUP)]

    def piece(c0, width):
        parts = []
        for b0 in range(c0, c0 + width, LANES):
            g0, y = groups[b0 // PROJ_GROUP]
            parts.append(y[:, b0 - g0:b0 - g0 + LANES])
        return parts[0] if len(parts) == 1 else jnp.concatenate(parts, axis=1)

    na_scale = HEAD_DIM ** -0.5
    dense_unit = LOG2_E if latent else 1.0
    qna_ref[...] = (piece(C_QNA, NA_WIDTH) * na_scale).astype(qna_ref.dtype)
    kna_ref[...] = piece(C_KNA, NA_WIDTH).astype(kna_ref.dtype)
    vna_ref[...] = piece(C_VNA, NA_WIDTH).astype(vna_ref.dtype)

    mla_scale = (MLA_NOPE + MLA_ROPE) ** -0.5
    cqn = _rms(piece(C_CQ, MLA_Q_LORA), gq_ref[...]).astype(jnp.bfloat16)
    qm = _dot(cqn, wuq_ref[...])
    kr = piece(C_KR, LANES)
    if latent:
        cm = jnp.concatenate([cm_ref[...]] * MLA_HEADS, axis=1)
        sm = jnp.concatenate([sm_ref[...]] * MLA_HEADS, axis=1)
        qm = qm * cm + _swap_halves(qm, MLA_ROPE // 4) * sm
        kr = kr * cm_ref[...] + _swap_halves(kr, MLA_ROPE // 4) * sm_ref[...]
    qm_ref[...] = (qm * (mla_scale * dense_unit)).astype(qm_ref.dtype)
    ckv = _rms(piece(C_CKV, MLA_KV_LORA), gkv_ref[...])
    ckv_b = ckv.astype(jnp.bfloat16)
    kn = _dot(ckv_b, wuk_ref[...])
    km_ref[...] = (kn + jnp.concatenate([kr] * MLA_HEADS, axis=1)).astype(km_ref.dtype)
    vm_ref[...] = _dot(ckv_b, wuv_ref[...]).astype(vm_ref.dtype)
    if not latent:
        ckv_ref[...] = ckv
        kr_ref[...] = kr[:, MLA_NOPE:MLA_NOPE + MLA_ROPE]

    ones = ones_ref[...]
    qg = piece(C_QG, GQA_WIDTH)
    qq = (qg * qg).astype(jnp.bfloat16)
    ms = jnp.concatenate([_dot(qq[:, b * LANES:(b + 1) * LANES], ones) for b in range(GQA_WIDTH // LANES)], axis=1)
    qg = qg * lax.rsqrt(ms + EPS) * ggq_ref[...]
    kg = piece(C_KG, LANES)
    kg = kg * lax.rsqrt(_head_mean_sq(kg, ones) + EPS) * ggk_ref[...]
    if latent:
        cg, sg = cg_ref[...], sg_ref[...]
        cg3 = jnp.concatenate([cg] * (GQA_WIDTH // LANES), axis=1)
        sg3 = jnp.concatenate([sg] * (GQA_WIDTH // LANES), axis=1)
        qg = qg * cg3 + _swap_halves(qg, HEAD_DIM // 4) * sg3
        kg = kg * cg + _swap_halves(kg, HEAD_DIM // 4) * sg
    qg_ref[...] = (qg * (na_scale * dense_unit)).astype(qg_ref.dtype)
    kg_ref[...] = kg.astype(kg_ref.dtype)
    vg_ref[...] = piece(C_VG, LANES).astype(vg_ref.dtype)


def _layer_spec(entry, buffered=False):
    a, layer = entry
    kw = dict(pipeline_mode=pl.Buffered(1)) if buffered else {}
    if layer is None:
        return pl.BlockSpec(a.shape, lambda *_: (0,) * a.ndim, **kw)
    return pl.BlockSpec((None,) + a.shape[1:], lambda *_: (layer,) + (0,) * (a.ndim - 1), **kw)


def _proj(x, mods, lw, rope, *, latent, seq):
    ntok, d = x.shape
    t = PROJ_TILE
    nt = ntok // t
    tiles_per_seq = seq // t if latent else 1
    tok = lambda w: pl.BlockSpec((t, w), lambda i: (i, 0))
    full = lambda e: _layer_spec(e, buffered=True)
    if latent:
        mod_spec = pl.BlockSpec((1, 6, d), lambda i: (i // tiles_per_seq, 0, 0))
    else:
        mod_spec = pl.BlockSpec((1, 6, d), lambda i: (0, 0, 0))
    kv_dt = jnp.bfloat16 if latent else jnp.float32
    bf = jnp.bfloat16
    sds = lambda w, dt: jax.ShapeDtypeStruct((ntok, w), dt)
    common = [x, mods, lw["g_pre_mix"], lw["w_in"], lw["mla_g_q"], lw["w_uq"], lw["mla_g_kv"],
              lw["w_uk"], lw["w_uv"], lw["ones"], lw["ggq"], lw["ggk"]]
    in_specs = [tok(d), mod_spec] + [full(a) for a in common[2:]]
    if latent:
        rope_spec = pl.BlockSpec((t, LANES), lambda i: (i % tiles_per_seq, 0))
        ins = common + [rope["cg"], rope["sg"], rope["cm"], rope["sm"]]
        in_specs = in_specs + [rope_spec] * 4
        outs = [sds(NA_WIDTH, bf), sds(NA_WIDTH, kv_dt), sds(NA_WIDTH, kv_dt), sds(MLA_HEADS * LANES, bf),
                sds(MLA_HEADS * LANES, bf), sds(MLA_WIDTH, bf), sds(GQA_WIDTH, bf), sds(LANES, kv_dt), sds(LANES, kv_dt)]
    else:
        ins = common
        outs = [sds(NA_WIDTH, bf), sds(NA_WIDTH, kv_dt), sds(NA_WIDTH, kv_dt), sds(MLA_HEADS * LANES, bf),
                sds(MLA_KV_LORA, jnp.float32), sds(MLA_ROPE, jnp.float32),
                sds(MLA_HEADS * LANES, bf), sds(MLA_WIDTH, bf), sds(GQA_WIDTH, bf), sds(LANES, kv_dt), sds(LANES, kv_dt)]
    out_specs = [tok(o.shape[1]) for o in outs]
    return pl.pallas_call(
        functools.partial(_proj_kernel, latent=latent),
        grid=(nt,),
        in_specs=in_specs,
        out_specs=out_specs,
        out_shape=outs,
        compiler_params=_params("arbitrary"),
        name="proj_lat" if latent else "proj_ctx",
    )(*[a[0] if isinstance(a, tuple) else a for a in ins])


def _lane_lo():
    return lax.broadcasted_iota(jnp.int32, (1, LANES), 1) < HEAD_DIM


def _split_heads(q):
    lo = _lane_lo()
    zero = jnp.zeros_like(q)
    return jnp.where(lo, q, zero), jnp.where(lo, zero, q)


def _softmax_pv(scores, values):
    m = scores[0].max(axis=-1, keepdims=True)
    for s in scores[1:]:
        m = jnp.maximum(m, s.max(axis=-1, keepdims=True))
    l = None
    acc = None
    for s, v in zip(scores, values):
        p = jnp.exp(s - m)
        ps = p.sum(axis=-1, keepdims=True)
        pv = _dot(p.astype(jnp.bfloat16), v)
        l = ps if l is None else l + ps
        acc = pv if acc is None else acc + pv
    return acc / l


def _ctx_attn_kernel(qna_ref, kna_ref, vna_ref, qm_ref, km_ref, vm_ref, qg_ref, kg_ref, vg_ref, ona_ref, odn_ref):
    lo = _lane_lo()
    bf = jnp.bfloat16

    def pair(q_lo, q_hi, k_lo, k_hi, v):
        o_lo = _softmax_pv([_dot_nt(q_lo, k_lo)], [v])
        o_hi = _softmax_pv([_dot_nt(q_hi, k_hi)], [v])
        return jnp.where(lo, o_lo, o_hi)

    for p in range(NA_WIDTH // LANES):
        cs = slice(p * LANES, (p + 1) * LANES)
        q_lo, q_hi = _split_heads(qna_ref[0, :, cs])
        k = kna_ref[0, :, cs].astype(bf)
        ona_ref[0, :, cs] = pair(q_lo, q_hi, k, k, vna_ref[0, :, cs].astype(bf)).astype(ona_ref.dtype)
    for p in range(MLA_HEADS // 2):
        c0 = 2 * p * LANES
        o = pair(qm_ref[0, :, c0:c0 + LANES], qm_ref[0, :, c0 + LANES:c0 + 2 * LANES],
                 km_ref[0, :, c0:c0 + LANES], km_ref[0, :, c0 + LANES:c0 + 2 * LANES],
                 vm_ref[0, :, p * LANES:(p + 1) * LANES])
        odn_ref[0, :, p * LANES:(p + 1) * LANES] = o.astype(odn_ref.dtype)
    kg = kg_ref[0].astype(bf)
    vg = vg_ref[0].astype(bf)
    for c in range(GQA_WIDTH // LANES):
        q_lo, q_hi = _split_heads(qg_ref[0, :, c * LANES:(c + 1) * LANES])
        o = pair(q_lo, q_hi, kg, kg, vg)
        odn_ref[0, :, MLA_WIDTH + c * LANES:MLA_WIDTH + (c + 1) * LANES] = o.astype(odn_ref.dtype)


def _ctx_attn(qna, kna, vna, qm, km, vm, qg, kg, vg, *, batch, seq):
    ins = [a.reshape(batch, seq, a.shape[-1]) for a in (qna, kna, vna, qm, km, vm, qg, kg, vg)]
    spec = lambda a: pl.BlockSpec((1, seq, a.shape[-1]), lambda b: (b, 0, 0))
    outs = [jax.ShapeDtypeStruct((batch, seq, NA_WIDTH), jnp.bfloat16),
            jax.ShapeDtypeStruct((batch, seq, DENSE_WIDTH), jnp.bfloat16)]
    ona, odn = pl.pallas_call(
        _ctx_attn_kernel,
        grid=(batch,),
        in_specs=[spec(a) for a in ins],
        out_specs=[spec(o) for o in outs],
        out_shape=outs,
        compiler_params=_params("arbitrary"),
        name="ctx_attn",
    )(*ins)
    return ona.reshape(batch * seq, NA_WIDTH), odn.reshape(batch * seq, DENSE_WIDTH)


def _na_attn_kernel(q_ref, k_ref, v_ref, kc_ref, vc_ref, pair_ref, o_ref, *, rows):
    lo = _lane_lo()
    blk = pl.program_id(1)
    row0 = blk * NA_ROWS
    key_row0 = jnp.clip(row0 - NA_KH // 2, 0, rows - NA_KEY_ROWS)
    start = pl.multiple_of(key_row0 * GRID_W, GRID_W)
    tq = NA_ROWS * GRID_W
    nkeys = NA_KEY_ROWS * GRID_W

    rq = row0 + lax.shift_right_logical(lax.broadcasted_iota(jnp.int32, (tq, 1), 0), 6)
    rk = key_row0 + lax.shift_right_logical(lax.broadcasted_iota(jnp.int32, (1, nkeys), 1), 6)
    rs = jnp.clip(rq - NA_KH // 2, 0, rows - NA_KH)
    row_mask = jnp.where((rk >= rs) & (rk < rs + NA_KH), 0.0, MASK_VALUE)

    def bias(h):
        base = key_row0 - row0 + (NA_KH - 1) + NA_PAD
        blocks = [jnp.concatenate([pair_ref[0, h, base + 2 * m - dq] for m in range(NA_KEY_ROWS // 2)], axis=1)
                  for dq in range(NA_ROWS)]
        return jnp.concatenate(blocks, axis=0) + row_mask

    for p in range(NA_WIDTH // LANES):
        cs = slice(p * LANES, (p + 1) * LANES)
        q_lo, q_hi = _split_heads(q_ref[0, :, cs])
        k = k_ref[0, pl.ds(start, nkeys), cs]
        v = v_ref[0, pl.ds(start, nkeys), cs]
        kc = kc_ref[0, 0, :, cs]
        vc = vc_ref[0, 0, :, cs]
        o_lo = _softmax_pv([_dot_nt(q_lo, k) + bias(2 * p), _dot_nt(q_lo, kc)], [v, vc])
        o_hi = _softmax_pv([_dot_nt(q_hi, k) + bias(2 * p + 1), _dot_nt(q_hi, kc)], [v, vc])
        o_ref[0, :, cs] = jnp.where(lo, o_lo, o_hi).astype(o_ref.dtype)


def _na_attn(q, k, v, kc, vc, pairs, layer, *, batch, seq):
    rows = seq // GRID_W
    nblk = rows // NA_ROWS
    tq = NA_ROWS * GRID_W
    past = kc.shape[2]
    assert GRID_W == 64 and rows >= NA_KEY_ROWS and rows % NA_ROWS == 0
    q3, k3, v3 = (a.reshape(batch, seq, NA_WIDTH) for a in (q, k, v))
    o = pl.pallas_call(
        functools.partial(_na_attn_kernel, rows=rows),
        grid=(batch, nblk),
        in_specs=[pl.BlockSpec((1, tq, NA_WIDTH), lambda b, i: (b, i, 0)),
                  pl.BlockSpec((1, seq, NA_WIDTH), lambda b, i: (b, 0, 0)),
                  pl.BlockSpec((1, seq, NA_WIDTH), lambda b, i: (b, 0, 0)),
                  pl.BlockSpec((1, 1, past, NA_WIDTH), lambda b, i: (b, layer, 0, 0)),
                  pl.BlockSpec((1, 1, past, NA_WIDTH), lambda b, i: (b, layer, 0, 0)),
                  pl.BlockSpec((1,) + pairs.shape[1:], lambda b, i: (layer, 0, 0, 0, 0))],
        out_specs=pl.BlockSpec((1, tq, NA_WIDTH), lambda b, i: (b, i, 0)),
        out_shape=jax.ShapeDtypeStruct((batch, seq, NA_WIDTH), jnp.bfloat16),
        compiler_params=_params("arbitrary", "arbitrary"),
        name="na_attn",
    )(q3, k3, v3, kc, vc, pairs)
    return o.reshape(batch * seq, NA_WIDTH)


def _na_pair_tables(na_rpb):
    cq = np.arange(GRID_W)[:, None]
    ck = np.arange(GRID_W)[None, :]
    cs = np.clip(cq - NA_KW // 2, 0, GRID_W - NA_KW)
    valid = (ck >= cs) & (ck < cs + NA_KW)
    onehot = ((ck - cq + NA_KW - 1)[None] == np.arange(2 * NA_KW - 1)[:, None, None]) & valid[None]
    t = jnp.einsum("lhdo,oqk->lhdqk", na_rpb, jnp.asarray(onehot, na_rpb.dtype), precision=lax.Precision.HIGHEST)
    t = jnp.where(jnp.asarray(valid), t, MASK_VALUE)
    t = jnp.pad(t, ((0, 0), (0, 0), (NA_PAD, NA_PAD), (0, 0), (0, 0)), constant_values=MASK_VALUE)
    return jnp.concatenate([t[:, :, :-1], t[:, :, 1:]], axis=-1)


def _dense_attn_kernel(qm_ref, qg_ref, kmc_ref, vmc_ref, kgc_ref, vgc_ref, kml_ref, vml_ref, kgl_ref, vgl_ref,
                       o_ref, s_lo_ref, s_hi_ref, *, nk, tk):
    lo = _lane_lo()
    bf = jnp.bfloat16
    past = kmc_ref.shape[2]

    def lane_fold(x, op):
        parts = [x[:, c * LANES:(c + 1) * LANES] for c in range(x.shape[1] // LANES)]
        while len(parts) > 1:
            parts = [op(parts[i], parts[i + 1]) if i + 1 < len(parts) else parts[i] for i in range(0, len(parts), 2)]
        return parts[0]

    def head(q, kc, vc, k_at, v_at, s_ref):
        spans = [(0, past)] + [(past + j * tk, tk) for j in range(nk)]
        mx = None
        for idx, (c0, width) in enumerate(spans):
            s = _dot_nt(q, kc if idx == 0 else k_at(idx - 1))
            s_ref[:, c0:c0 + width] = s
            part = lane_fold(s, jnp.maximum)
            mx = part if mx is None else jnp.maximum(mx, part)
        m = mx.max(axis=-1, keepdims=True)
        lsum = None
        acc = None
        for idx, (c0, width) in enumerate(spans):
            p = jnp.exp2(s_ref[:, c0:c0 + width] - m)
            part = lane_fold(p, jnp.add)
            pv = _dot(p.astype(bf), vc if idx == 0 else v_at(idx - 1))
            lsum = part if lsum is None else lsum + part
            acc = pv if acc is None else acc + pv
        return acc / lsum.sum(axis=-1, keepdims=True)

    def unit(q_lo, q_hi, kc_lo, kc_hi, vc, k_lo_at, k_hi_at, v_at):
        o_lo = head(q_lo, kc_lo, vc, k_lo_at, v_at, s_lo_ref)
        o_hi = head(q_hi, kc_hi, vc, k_hi_at, v_at, s_hi_ref)
        return jnp.where(lo, o_lo, o_hi)

    for p in range(MLA_HEADS // 2):
        c0 = 2 * p * LANES
        c1 = c0 + LANES
        o = unit(qm_ref[0, :, c0:c1], qm_ref[0, :, c1:c1 + LANES],
                 kmc_ref[0, 0, :, c0:c1], kmc_ref[0, 0, :, c1:c1 + LANES], vmc_ref[0, 0, :, p * LANES:(p + 1) * LANES],
                 lambda j, c0=c0, c1=c1: kml_ref[0, j * tk:(j + 1) * tk, c0:c1],
                 lambda j, c1=c1: kml_ref[0, j * tk:(j + 1) * tk, c1:c1 + LANES],
                 lambda j, p=p: vml_ref[0, j * tk:(j + 1) * tk, p * LANES:(p + 1) * LANES])
        o_ref[0, :, p * LANES:(p + 1) * LANES] = o.astype(o_ref.dtype)
    kgc = kgc_ref[0, 0]
    vgc = vgc_ref[0, 0]
    kg_at = lambda j: kgl_ref[0, j * tk:(j + 1) * tk, :]
    vg_at = lambda j: vgl_ref[0, j * tk:(j + 1) * tk, :]
    for c in range(GQA_WIDTH // LANES):
        q_lo, q_hi = _split_heads(qg_ref[0, :, c * LANES:(c + 1) * LANES])
        o = unit(q_lo, q_hi, kgc, kgc, vgc, kg_at, kg_at, vg_at)
        o_ref[0, :, MLA_WIDTH + c * LANES:MLA_WIDTH + (c + 1) * LANES] = o.astype(o_ref.dtype)


def _dense_attn(qm, qg, kmc, vmc, kgc, vgc, kml, vml, kgl, vgl, layer, *, batch, seq):
    tq, tk = DENSE_Q_TILE, DENSE_K_TILE
    r3 = lambda a: a.reshape(batch, seq, a.shape[-1])
    qm, qg, kml, vml, kgl, vgl = (r3(a) for a in (qm, qg, kml, vml, kgl, vgl))
    qspec = lambda a: pl.BlockSpec((1, tq, a.shape[-1]), lambda b, i: (b, i, 0))
    cspec = lambda a: pl.BlockSpec((1, 1) + a.shape[2:], lambda b, i: (b, layer, 0, 0))
    lspec = lambda a: pl.BlockSpec((1, seq, a.shape[-1]), lambda b, i: (b, 0, 0), pipeline_mode=pl.Buffered(1))
    o = pl.pallas_call(
        functools.partial(_dense_attn_kernel, nk=seq // tk, tk=tk),
        grid=(batch, seq // tq),
        in_specs=[qspec(qm), qspec(qg), cspec(kmc), cspec(vmc), cspec(kgc), cspec(vgc),
                  lspec(kml), lspec(vml), lspec(kgl), lspec(vgl)],
        out_specs=pl.BlockSpec((1, tq, DENSE_WIDTH), lambda b, i: (b, i, 0)),
        out_shape=jax.ShapeDtypeStruct((batch, seq, DENSE_WIDTH), jnp.bfloat16),
        scratch_shapes=[pltpu.VMEM((tq, kmc.shape[2] + seq), jnp.float32)] * 2,
        compiler_params=_params("arbitrary", "arbitrary"),
        name="dense_attn",
    )(qm, qg, kmc, vmc, kgc, vgc, kml, vml, kgl, vgl)
    return o.reshape(batch * seq, DENSE_WIDTH)


def _mla_ctx_kernel(ckv_ref, kr_ref, wuk_ref, wuv_ref, k_ref, v_ref):
    ckv = ckv_ref[0, 0]
    kn = _dot(ckv, wuk_ref[0])
    k_ref[0, 0] = (kn + jnp.concatenate([kr_ref[0, 0].astype(jnp.float32)] * MLA_HEADS, axis=1)).astype(k_ref.dtype)
    v_ref[0, 0] = _dot(ckv, wuv_ref[0]).astype(v_ref.dtype)


def _mla_ctx_expand(ckv, kr_pad, wuk, wuv):
    b, nl, past, _ = ckv.shape
    spec = lambda w: pl.BlockSpec((1, 1, past, w), lambda i, l: (i, l, 0, 0))
    wspec = lambda a: pl.BlockSpec((1,) + a.shape[1:], lambda i, l: (l, 0, 0))
    return pl.pallas_call(
        _mla_ctx_kernel,
        grid=(b, nl),
        in_specs=[spec(MLA_KV_LORA), spec(LANES), wspec(wuk), wspec(wuv)],
        out_specs=[spec(MLA_HEADS * LANES), spec(MLA_WIDTH)],
        out_shape=[jax.ShapeDtypeStruct((b, nl, past, MLA_HEADS * LANES), jnp.bfloat16),
                   jax.ShapeDtypeStruct((b, nl, past, MLA_WIDTH), jnp.bfloat16)],
        compiler_params=_params("arbitrary", "arbitrary"),
        name="mla_ctx_expand",
    )(ckv, kr_pad, wuk, wuv)


def _mix_ffn_kernel(ona_ref, odn_ref, x_ref, onap_ref, odnp_ref, xp_ref, onan_ref, odnn_ref, xn_ref, mod_ref,
                    wna_ref, wdn_ref, gpost_ref, gpre_ref, wup_ref, cw_ref, cb_ref, wd_ref, gffn_ref,
                    o_ref, hperm_ref, yperm_ref, *, seq, chunk):
    i = pl.program_id(0)
    t, d = x_ref.shape
    halo = xp_ref.shape[0]
    dff = wd_ref.shape[0]
    groups = t // 8
    mod = mod_ref[0]
    gt_m, sh_f, sc_f, gt_f = mod[2:3, :], mod[3:4, :], mod[4:5, :], mod[5:6, :]

    cat0 = lambda parts: jnp.concatenate(parts, axis=0)
    ona = cat0([ona_ref[...], onap_ref[...], onan_ref[...]])
    odn = cat0([odn_ref[...], odnp_ref[...], odnn_ref[...]])
    x = cat0([x_ref[...], xp_ref[...], xn_ref[...]])
    o = _dot(ona, wna_ref[...]) + _dot(odn, wdn_ref[...])
    x1 = x + gt_m * _rms(o, gpost_ref[...])
    h2 = _rms(x1, gpre_ref[...]) * (1.0 + sc_f) + sh_f
    o_ref[...] = x1[0:t, :]

    nlb = d // LANES
    for j in range(nlb):
        for k in range(8):
            hperm_ref[j, pl.ds(k, groups, stride=8), :] = h2[k * groups:(k + 1) * groups, j * LANES:(j + 1) * LANES]
    h = cat0([jnp.concatenate([hperm_ref[j] for j in range(nlb)], axis=1), h2[t:, :]]).astype(jnp.bfloat16)

    kk = lax.broadcasted_iota(jnp.int32, (8, 1), 0)
    tok = i * t + groups * kk
    prev_zero = jnp.bitwise_and(tok, seq - 1) == 0
    next_zero = jnp.bitwise_and(tok + groups, seq - 1) == 0

    def conv(c0):
        w = wup_ref[:, c0:c0 + chunk]
        u_all = _dot(h, w)
        u = u_all[0:t, :]
        u_hp = u_all[t + halo - 1:t + halo, :]
        u_hn = u_all[t + halo:t + halo + 1, :]
        b_prev = jnp.where(kk == 0, u_hp, pltpu.roll(u[t - 8:t, :], 1, 0))
        b_prev = jnp.where(prev_zero, 0.0, b_prev)
        b_next = jnp.where(kk == 7, u_hn, pltpu.roll(u[0:8, :], 7, 0))
        b_next = jnp.where(next_zero, 0.0, b_next)
        prev = jnp.concatenate([b_prev, u[0:t - 8, :]], axis=0)
        nxt = jnp.concatenate([u[8:t, :], b_next], axis=0)
        cw = cw_ref[:, c0:c0 + chunk]
        return prev * cw[0:1, :] + u * cw[1:2, :] + nxt * cw[2:3, :] + cb_ref[:, c0:c0 + chunk]

    acc = None
    for c in range(dff // chunk):
        gate = conv(c * chunk)
        val = conv(dff + c * chunk)
        act = (gate / (1.0 + jnp.exp(-gate)) * val).astype(jnp.bfloat16)
        part = _dot(act, wd_ref[c * chunk:(c + 1) * chunk, :])
        acc = part if acc is None else acc + part

    y = gt_f * _rms(acc, gffn_ref[...])
    for j in range(nlb):
        yperm_ref[j] = y[:, j * LANES:(j + 1) * LANES]
    for k in range(8):
        rows = slice(k * groups, (k + 1) * groups)
        yk = jnp.concatenate([yperm_ref[j, pl.ds(k, groups, stride=8), :] for j in range(nlb)], axis=1)
        o_ref[rows, :] = o_ref[rows, :] + yk


def _mix_ffn(ona, odn, x, mods, lw, *, seq, seq_per_mod):
    ntok, d = x.shape
    t = FFN_TILE
    halo = 16
    assert seq & (seq - 1) == 0 and (t % seq == 0 or seq % t == 0) and ntok % t == 0 and t % 64 == 0
    nhalo = ntok // halo
    if seq_per_mod is None:
        mod_spec = pl.BlockSpec((1, 6, d), lambda i: (0, 0, 0))
    else:
        tps = seq_per_mod // t
        mod_spec = pl.BlockSpec((1, 6, d), lambda i: (i // tps, 0, 0))
    tok = lambda w: pl.BlockSpec((t, w), lambda i: (i, 0))
    prv = lambda w: pl.BlockSpec((halo, w), lambda i: (jnp.maximum(i * (t // halo) - 1, 0), 0))
    nxt = lambda w: pl.BlockSpec((halo, w), lambda i: (jnp.minimum((i + 1) * (t // halo), nhalo - 1), 0))
    full = lambda e: _layer_spec(e, buffered=True)
    weights = [lw["w_out_na"], lw["w_out_dn"], lw["g_post_mix"], lw["g_pre_ffn"], lw["w_up"], lw["conv_w"],
               lw["conv_b"], lw["w_down"], lw["g_post_ffn"]]
    widths = (NA_WIDTH, DENSE_WIDTH, d)
    return pl.pallas_call(
        functools.partial(_mix_ffn_kernel, seq=seq, chunk=FF_CHUNK),
        grid=(ntok // t,),
        in_specs=[tok(w) for w in widths] + [prv(w) for w in widths] + [nxt(w) for w in widths] + [mod_spec]
                 + [full(a) for a in weights],
        out_specs=tok(d),
        out_shape=jax.ShapeDtypeStruct((ntok, d), jnp.float32),
        scratch_shapes=[pltpu.VMEM((d // LANES, t, LANES), jnp.float32)] * 2,
        compiler_params=_params("arbitrary"),
        name="mix_ffn",
    )(ona, odn, x, ona, odn, x, ona, odn, x, mods, *[w[0] for w in weights])


def _prep_weights(w_in, mla_w_uq, mla_w_ukv, w_out, gqa_g_q, gqa_g_k):
    nl, d, _ = w_in.shape
    bf = jnp.bfloat16
    cat = lambda parts: jnp.concatenate(parts, axis=-1)
    o_kr = 3 * NA_WIDTH + MLA_Q_LORA + MLA_KV_LORA
    o_qg = o_kr + MLA_ROPE
    o_kg = o_qg + GQA_WIDTH
    o_vg = o_kg + GQA_KV_HEADS * HEAD_DIM
    pad_r = LANES - MLA_NOPE - MLA_ROPE
    wb = w_in.astype(bf)
    z = lambda rows, n: jnp.zeros((nl, rows, n), bf)
    kr = wb[..., o_kr:o_qg]
    qg = cat([wb[..., o_qg + HEAD_DIM * h:o_qg + HEAD_DIM * (h + 1)] for h in GQA_ORDER])
    kg = wb[..., o_kg:o_vg]
    w_all = cat([wb[..., :o_kr], qg, kg, wb[..., o_vg:], z(d, MLA_NOPE), kr, z(d, pad_r)])
    assert w_all.shape[-1] == C_END

    uq = mla_w_uq.astype(bf)
    ukv = mla_w_ukv.astype(bf)
    ql, kl = uq.shape[1], ukv.shape[1]
    qw = MLA_NOPE + MLA_ROPE
    kvw = MLA_NOPE + MLA_V
    w_uq = cat([p for h in range(MLA_HEADS) for p in (uq[..., qw * h:qw * (h + 1)], z(ql, pad_r))])
    w_uk = cat([p for h in range(MLA_HEADS) for p in (ukv[..., kvw * h:kvw * h + MLA_NOPE], z(kl, LANES - MLA_NOPE))])
    w_uv = cat([ukv[..., kvw * h + MLA_NOPE:kvw * (h + 1)] for h in range(MLA_HEADS)])

    wo = w_out.astype(bf)
    o_gqa = NA_WIDTH + MLA_WIDTH
    w_out_dn = jnp.concatenate([wo[:, NA_WIDTH:o_gqa]]
                               + [wo[:, o_gqa + HEAD_DIM * h:o_gqa + HEAD_DIM * (h + 1)] for h in GQA_ORDER], axis=1)

    ggq = jnp.tile(gqa_g_q, (1, GQA_HEADS))[:, None, :]
    ggk = jnp.tile(gqa_g_k, (1, GQA_KV_HEADS))[:, None, :]
    ones = jnp.asarray(np.kron(np.eye(LANES // HEAD_DIM), np.full((HEAD_DIM, HEAD_DIM), 1.0 / HEAD_DIM)), bf)
    return dict(w_in=w_all, w_uq=w_uq, w_uk=w_uk, w_uv=w_uv, w_out_na=wo[:, :NA_WIDTH], w_out_dn=w_out_dn,
                ggq=ggq, ggk=ggk, ones=ones)


def _rope_tables(seq):
    t = jnp.arange(seq, dtype=jnp.int32)
    r, c = (t // GRID_W).astype(jnp.float32), (t % GRID_W).astype(jnp.float32)

    def tables(dim):
        quarter = dim // 4
        freqs = ROPE_THETA ** (-jnp.arange(quarter, dtype=jnp.float32) / quarter)
        ar_, ac_ = r[:, None] * freqs[None, :], c[:, None] * freqs[None, :]
        cos = jnp.concatenate([jnp.cos(ar_), jnp.cos(ar_), jnp.cos(ac_), jnp.cos(ac_)], axis=1)
        sin = jnp.concatenate([-jnp.sin(ar_), jnp.sin(ar_), -jnp.sin(ac_), jnp.sin(ac_)], axis=1)
        return cos, sin

    c64, s64 = tables(HEAD_DIM)
    c32, s32 = tables(MLA_ROPE)
    pad = LANES - MLA_NOPE - MLA_ROPE
    cm = jnp.concatenate([jnp.ones((seq, MLA_NOPE)), c32, jnp.zeros((seq, pad))], axis=1)
    sm = jnp.concatenate([jnp.zeros((seq, MLA_NOPE)), s32, jnp.zeros((seq, pad))], axis=1)
    return dict(cg=jnp.tile(c64, (1, 2)), sg=jnp.tile(s64, (1, 2)), cm=cm.astype(jnp.float32), sm=sm.astype(jnp.float32))


def kernel(x_prompt, x_sample, c, cache_na_k, cache_na_v, cache_mla_ckv, cache_mla_krope, cache_gqa_k, cache_gqa_v, c_ctx, w_ada, b_ada, g_pre_mix, g_post_mix, g_pre_ffn, g_post_ffn, w_in, na_rpb, mla_g_q, mla_w_uq, mla_g_kv, mla_w_ukv, gqa_g_q, gqa_g_k, w_out, ffn_w_up, ffn_conv_w, ffn_conv_b, ffn_w_down):
    batch, seq, d = x_prompt.shape
    dbatch, dseq, _ = x_sample.shape
    nl = w_ada.shape[0]
    past = cache_na_k.shape[2]
    bf = jnp.bfloat16
    assert dbatch + 1 <= 8 and dseq % GRID_W == 0

    cond = jnp.zeros((8, d), jnp.float32).at[0].set(c_ctx).at[1:1 + dbatch].set(c)
    mods_all = _ada_mods(cond, w_ada, b_ada).reshape(nl, 8, 6, d)

    pw = _prep_weights(w_in, mla_w_uq, mla_w_ukv, w_out, gqa_g_q, gqa_g_k)
    rope = _rope_tables(dseq)
    vec = lambda g: g[:, None, :]
    stacked = dict(
        g_pre_mix=vec(g_pre_mix), g_post_mix=vec(g_post_mix), g_pre_ffn=vec(g_pre_ffn), g_post_ffn=vec(g_post_ffn),
        mla_g_q=vec(mla_g_q), mla_g_kv=vec(mla_g_kv), w_in=pw["w_in"],
        w_uq=pw["w_uq"], w_uk=pw["w_uk"], w_uv=pw["w_uv"], ggq=pw["ggq"], ggk=pw["ggk"],
        w_out_na=pw["w_out_na"], w_out_dn=pw["w_out_dn"],
        w_up=ffn_w_up.astype(bf), w_down=ffn_w_down.astype(bf), conv_w=ffn_conv_w, conv_b=vec(ffn_conv_b))

    kc_na = cache_na_k.reshape(dbatch, nl, past, NA_WIDTH).astype(bf)
    vc_na = cache_na_v.reshape(dbatch, nl, past, NA_WIDTH).astype(bf)
    kc_g = cache_gqa_k.reshape(dbatch, nl, past, GQA_KV_HEADS * HEAD_DIM).astype(bf)
    vc_g = cache_gqa_v.reshape(dbatch, nl, past, GQA_KV_HEADS * HEAD_DIM).astype(bf)
    kr_pad = jnp.pad(cache_mla_krope, ((0, 0), (0, 0), (0, 0), (MLA_NOPE, LANES - MLA_NOPE - MLA_ROPE))).astype(bf)
    kc_m, vc_m = _mla_ctx_expand(cache_mla_ckv.astype(bf), kr_pad, pw["w_uk"], pw["w_uv"])

    na_pairs = _na_pair_tables(na_rpb)

    xp = x_prompt.reshape(batch * seq, d)
    xs = x_sample.reshape(dbatch * dseq, d)
    caches = [[] for _ in range(6)]
    for l in range(nl):
        lw = {k: (v, l) for k, v in stacked.items()}
        lw["ones"] = (pw["ones"], None)
        mods_ctx = mods_all[l, 0:1]
        mods_lat = mods_all[l, 1:1 + dbatch]

        qna, kna, vna, qm, ckv, kr, km, vm, qg, kg, vg = _proj(xp, mods_ctx, lw, None, latent=False, seq=seq)
        for lst, a in zip(caches, (kna, vna, ckv, kr, kg, vg)):
            lst.append(a)
        ona, odn = _ctx_attn(qna, kna, vna, qm, km, vm, qg, kg, vg, batch=batch, seq=seq)
        xp = _mix_ffn(ona, odn, xp, mods_ctx, lw, seq=seq, seq_per_mod=None)

        qna, kna, vna, qm, km, vm, qg, kg, vg = _proj(xs, mods_lat, lw, rope, latent=True, seq=dseq)
        ona = _na_attn(qna, kna, vna, kc_na, vc_na, na_pairs, l, batch=dbatch, seq=dseq)
        odn = _dense_attn(qm, qg, kc_m, vc_m, kc_g, vc_g, km, vm, kg, vg, l, batch=dbatch, seq=dseq)
        xs = _mix_ffn(ona, odn, xs, mods_lat, lw, seq=dseq, seq_per_mod=dseq)

    def stack(lst, tail):
        return jnp.stack([a.reshape((batch, seq) + tail) for a in lst], axis=1)

    return (xp.reshape(batch, seq, d), xs.reshape(dbatch, dseq, d),
            stack(caches[0], (NA_HEADS, HEAD_DIM)), stack(caches[1], (NA_HEADS, HEAD_DIM)),
            stack(caches[2], (MLA_KV_LORA,)), stack(caches[3], (MLA_ROPE,)),
            stack(caches[4], (GQA_KV_HEADS, HEAD_DIM)), stack(caches[5], (GQA_KV_HEADS, HEAD_DIM)))
```

```python
import functools

import numpy as np
import jax
import jax.numpy as jnp
from jax import lax
from jax.experimental import pallas as pl
from jax.experimental.pallas import tpu as pltpu

GRID_W = 64
HEAD_DIM = 64
NA_HEADS = 6
NA_KH = 8
NA_KW = 16
MLA_HEADS = 4
MLA_Q_LORA = 256
MLA_KV_LORA = 128
MLA_NOPE = 64
MLA_ROPE = 32
MLA_V = 64
GQA_HEADS = 6
GQA_KV_HEADS = 2
GQA_GROUP = GQA_HEADS // GQA_KV_HEADS
ROPE_THETA = 10000.0
EPS = 1e-6

NA_WIDTH = NA_HEADS * HEAD_DIM
MLA_WIDTH = MLA_HEADS * MLA_V
GQA_WIDTH = GQA_HEADS * HEAD_DIM
DENSE_WIDTH = MLA_WIDTH + GQA_WIDTH

LANES = 128
VMEM_LIMIT = 52 * 1024 * 1024
MASK_VALUE = -1e30
LOG2_E = 1.4426950408889634

PROJ_TILE = 512
FFN_TILE = 512
FF_CHUNK = 256
DENSE_Q_TILE = 256
DENSE_K_TILE = 512
NA_ROWS = 4
NA_KEY_ROWS = NA_ROWS + NA_KH
NA_PAD = NA_KEY_ROWS - NA_KH

C_QNA = 0
C_KNA = C_QNA + NA_WIDTH
C_VNA = C_KNA + NA_WIDTH
C_CQ = C_VNA + NA_WIDTH
C_CKV = C_CQ + MLA_Q_LORA
C_QG = C_CKV + MLA_KV_LORA
C_KG = C_QG + GQA_WIDTH
C_VG = C_KG + LANES
C_KR = C_VG + LANES
C_END = C_KR + LANES
PROJ_GROUP = 512

GQA_ORDER = (0, 3, 1, 4, 2, 5)


def _dot(a, b):
    return jnp.dot(a, b, preferred_element_type=jnp.float32)


def _dot_nt(a, b):
    return lax.dot_general(a, b, (((1,), (1,)), ((), ())), preferred_element_type=jnp.float32)


def _params(*sem):
    return pltpu.CompilerParams(dimension_semantics=sem, vmem_limit_bytes=VMEM_LIMIT)


def _rms(x, g):
    return x * lax.rsqrt(jnp.mean(x * x, axis=-1, keepdims=True) + EPS) * g


def _head_mean_sq(x, ones_bf16):
    xx = x * x
    hi = xx.astype(jnp.bfloat16)
    lo = (xx - hi.astype(jnp.float32)).astype(jnp.bfloat16)
    return _dot(hi, ones_bf16) + _dot(lo, ones_bf16)


def _ada_kernel(cond_ref, w_ref, b_ref, o_ref):
    cnd = cond_ref[...]
    s = cnd / (1.0 + jnp.exp(-cnd))
    o_ref[0] = jnp.dot(s, w_ref[0], preferred_element_type=jnp.float32,
                       precision=lax.Precision.HIGHEST) + b_ref[0]


def _ada_mods(cond, w_ada, b_ada):
    nl, d, n6 = w_ada.shape
    tn = 1536
    return pl.pallas_call(
        _ada_kernel,
        grid=(nl, n6 // tn),
        in_specs=[pl.BlockSpec((8, d), lambda l, j: (0, 0)),
                  pl.BlockSpec((1, d, tn), lambda l, j: (l, 0, j)),
                  pl.BlockSpec((1, 1, tn), lambda l, j: (l, 0, j))],
        out_specs=pl.BlockSpec((1, 8, tn), lambda l, j: (l, 0, j)),
        out_shape=jax.ShapeDtypeStruct((nl, 8, n6), jnp.float32),
        compiler_params=_params("arbitrary", "arbitrary"),
        name="ada_mods",
    )(cond, w_ada, b_ada.reshape(nl, 1, n6))


def _swap_halves(x, half):
    lane = lax.broadcasted_iota(jnp.int32, (1, LANES), 1)
    first = jnp.bitwise_and(lane, 2 * half - 1) < half
    blocks = []
    for b in range(x.shape[1] // LANES):
        xb = x[:, b * LANES:(b + 1) * LANES]
        blocks.append(jnp.where(first, pltpu.roll(xb, LANES - half, 1), pltpu.roll(xb, half, 1)))
    return jnp.concatenate(blocks, axis=1)


def _proj_kernel(*refs, latent):
    if latent:
        (x_ref, mod_ref, gpre_ref, w_ref, gq_ref, wuq_ref, gkv_ref, wuk_ref, wuv_ref,
         ones_ref, ggq_ref, ggk_ref, cg_ref, sg_ref, cm_ref, sm_ref,
         qna_ref, kna_ref, vna_ref, qm_ref, km_ref, vm_ref, qg_ref, kg_ref, vg_ref) = refs
    else:
        (x_ref, mod_ref, gpre_ref, w_ref, gq_ref, wuq_ref, gkv_ref, wuk_ref, wuv_ref,
         ones_ref, ggq_ref, ggk_ref,
         qna_ref, kna_ref, vna_ref, qm_ref, ckv_ref, kr_ref, km_ref, vm_ref, qg_ref, kg_ref, vg_ref) = refs

    x = x_ref[...]
    mod = mod_ref[0]
    sh, sc = mod[0:1, :], mod[1:2, :]
    h = (_rms(x, gpre_ref[...]) * (1.0 + sc) + sh).astype(jnp.bfloat16)

    groups = [(c0, _dot(h, w_ref[:, c0:min(c0 + PROJ_GROUP, C_END)])) for c0 in range(0, C_END, PROJ_GROUP)]

    def piece(c0, width):
        parts = []
        for b0 in range(c0, c0 + width, LANES):
            g0, y = groups[b0 // PROJ_GROUP]
            parts.append(y[:, b0 - g0:b0 - g0 + LANES])
        return parts[0] if len(parts) == 1 else jnp.concatenate(parts, axis=1)

    na_scale = HEAD_DIM ** -0.5
    dense_unit = LOG2_E if latent else 1.0
    qna_ref[...] = (piece(C_QNA, NA_WIDTH) * na_scale).astype(qna_ref.dtype)
    kna_ref[...] = piece(C_KNA, NA_WIDTH).astype(kna_ref.dtype)
    vna_ref[...] = piece(C_VNA, NA_WIDTH).astype(vna_ref.dtype)

    mla_scale = (MLA_NOPE + MLA_ROPE) ** -0.5
    cqn = _rms(piece(C_CQ, MLA_Q_LORA), gq_ref[...]).astype(jnp.bfloat16)
    qm = _dot(cqn, wuq_ref[...])
    kr = piece(C_KR, LANES)
    if latent:
        cm = jnp.concatenate([cm_ref[...]] * MLA_HEADS, axis=1)
        sm = jnp.concatenate([sm_ref[...]] * MLA_HEADS, axis=1)
        qm = qm * cm + _swap_halves(qm, MLA_ROPE // 4) * sm
        kr = kr * cm_ref[...] + _swap_halves(kr, MLA_ROPE // 4) * sm_ref[...]
    qm_ref[...] = (qm * (mla_scale * dense_unit)).astype(qm_ref.dtype)
    ckv = _rms(piece(C_CKV, MLA_KV_LORA), gkv_ref[...])
    ckv_b = ckv.astype(jnp.bfloat16)
    kn = _dot(ckv_b, wuk_ref[...])
    km_ref[...] = (kn + jnp.concatenate([kr] * MLA_HEADS, axis=1)).astype(km_ref.dtype)
    vm_ref[...] = _dot(ckv_b, wuv_ref[...]).astype(vm_ref.dtype)
    if not latent:
        ckv_ref[...] = ckv
        kr_ref[...] = kr[:, MLA_NOPE:MLA_NOPE + MLA_ROPE]

    ones = ones_ref[...]
    qg = piece(C_QG, GQA_WIDTH)
    qq = (qg * qg).astype(jnp.bfloat16)
    ms = jnp.concatenate([_dot(qq[:, b * LANES:(b + 1) * LANES], ones) for b in range(GQA_WIDTH // LANES)], axis=1)
    qg = qg * lax.rsqrt(ms + EPS) * ggq_ref[...]
    kg = piece(C_KG, LANES)
    kg = kg * lax.rsqrt(_head_mean_sq(kg, ones) + EPS) * ggk_ref[...]
    if latent:
        cg, sg = cg_ref[...], sg_ref[...]
        cg3 = jnp.concatenate([cg] * (GQA_WIDTH // LANES), axis=1)
        sg3 = jnp.concatenate([sg] * (GQA_WIDTH // LANES), axis=1)
        qg = qg * cg3 + _swap_halves(qg, HEAD_DIM // 4) * sg3
        kg = kg * cg + _swap_halves(kg, HEAD_DIM // 4) * sg
    qg_ref[...] = (qg * (na_scale * dense_unit)).astype(qg_ref.dtype)
    kg_ref[...] = kg.astype(kg_ref.dtype)
    vg_ref[...] = piece(C_VG, LANES).astype(vg_ref.dtype)


def _layer_spec(entry, buffered=False):
    a, layer = entry
    kw = dict(pipeline_mode=pl.Buffered(1)) if buffered else {}
    if layer is None:
        return pl.BlockSpec(a.shape, lambda *_: (0,) * a.ndim, **kw)
    return pl.BlockSpec((None,) + a.shape[1:], lambda *_: (layer,) + (0,) * (a.ndim - 1), **kw)


def _proj(x, mods, lw, rope, *, latent, seq):
    ntok, d = x.shape
    t = PROJ_TILE
    nt = ntok // t
    tiles_per_seq = seq // t if latent else 1
    tok = lambda w: pl.BlockSpec((t, w), lambda i: (i, 0))
    full = lambda e: _layer_spec(e, buffered=True)
    if latent:
        mod_spec = pl.BlockSpec((1, 6, d), lambda i: (i // tiles_per_seq, 0, 0))
    else:
        mod_spec = pl.BlockSpec((1, 6, d), lambda i: (0, 0, 0))
    kv_dt = jnp.bfloat16 if latent else jnp.float32
    bf = jnp.bfloat16
    sds = lambda w, dt: jax.ShapeDtypeStruct((ntok, w), dt)
    common = [x, mods, lw["g_pre_mix"], lw["w_in"], lw["mla_g_q"], lw["w_uq"], lw["mla_g_kv"],
              lw["w_uk"], lw["w_uv"], lw["ones"], lw["ggq"], lw["ggk"]]
    in_specs = [tok(d), mod_spec] + [full(a) for a in common[2:]]
    if latent:
        rope_spec = pl.BlockSpec((t, LANES), lambda i: (i % tiles_per_seq, 0))
        ins = common + [rope["cg"], rope["sg"], rope["cm"], rope["sm"]]
        in_specs = in_specs + [rope_spec] * 4
        outs = [sds(NA_WIDTH, bf), sds(NA_WIDTH, kv_dt), sds(NA_WIDTH, kv_dt), sds(MLA_HEADS * LANES, bf),
                sds(MLA_HEADS * LANES, bf), sds(MLA_WIDTH, bf), sds(GQA_WIDTH, bf), sds(LANES, kv_dt), sds(LANES, kv_dt)]
    else:
        ins = common
        outs = [sds(NA_WIDTH, bf), sds(NA_WIDTH, kv_dt), sds(NA_WIDTH, kv_dt), sds(MLA_HEADS * LANES, bf),
                sds(MLA_KV_LORA, jnp.float32), sds(MLA_ROPE, jnp.float32),
                sds(MLA_HEADS * LANES, bf), sds(MLA_WIDTH, bf), sds(GQA_WIDTH, bf), sds(LANES, kv_dt), sds(LANES, kv_dt)]
    out_specs = [tok(o.shape[1]) for o in outs]
    return pl.pallas_call(
        functools.partial(_proj_kernel, latent=latent),
        grid=(nt,),
        in_specs=in_specs,
        out_specs=out_specs,
        out_shape=outs,
        compiler_params=_params("arbitrary"),
        name="proj_lat" if latent else "proj_ctx",
    )(*[a[0] if isinstance(a, tuple) else a for a in ins])


def _lane_lo():
    return lax.broadcasted_iota(jnp.int32, (1, LANES), 1) < HEAD_DIM


def _split_heads(q):
    lo = _lane_lo()
    zero = jnp.zeros_like(q)
    return jnp.where(lo, q, zero), jnp.where(lo, zero, q)


def _softmax_pv(scores, values):
    m = scores[0].max(axis=-1, keepdims=True)
    for s in scores[1:]:
        m = jnp.maximum(m, s.max(axis=-1, keepdims=True))
    l = None
    acc = None
    for s, v in zip(scores, values):
        p = jnp.exp(s - m)
        ps = p.sum(axis=-1, keepdims=True)
        pv = _dot(p.astype(jnp.bfloat16), v)
        l = ps if l is None else l + ps
        acc = pv if acc is None else acc + pv
    return acc / l


def _ctx_attn_kernel(qna_ref, kna_ref, vna_ref, qm_ref, km_ref, vm_ref, qg_ref, kg_ref, vg_ref, ona_ref, odn_ref):
    lo = _lane_lo()
    bf = jnp.bfloat16

    def pair(q_lo, q_hi, k_lo, k_hi, v):
        o_lo = _softmax_pv([_dot_nt(q_lo, k_lo)], [v])
        o_hi = _softmax_pv([_dot_nt(q_hi, k_hi)], [v])
        return jnp.where(lo, o_lo, o_hi)

    for p in range(NA_WIDTH // LANES):
        cs = slice(p * LANES, (p + 1) * LANES)
        q_lo, q_hi = _split_heads(qna_ref[0, :, cs])
        k = kna_ref[0, :, cs].astype(bf)
        ona_ref[0, :, cs] = pair(q_lo, q_hi, k, k, vna_ref[0, :, cs].astype(bf)).astype(ona_ref.dtype)
    for p in range(MLA_HEADS // 2):
        c0 = 2 * p * LANES
        o = pair(qm_ref[0, :, c0:c0 + LANES], qm_ref[0, :, c0 + LANES:c0 + 2 * LANES],
                 km_ref[0, :, c0:c0 + LANES], km_ref[0, :, c0 + LANES:c0 + 2 * LANES],
                 vm_ref[0, :, p * LANES:(p + 1) * LANES])
        odn_ref[0, :, p * LANES:(p + 1) * LANES] = o.astype(odn_ref.dtype)
    kg = kg_ref[0].astype(bf)
    vg = vg_ref[0].astype(bf)
    for c in range(GQA_WIDTH // LANES):
        q_lo, q_hi = _split_heads(qg_ref[0, :, c * LANES:(c + 1) * LANES])
        o = pair(q_lo, q_hi, kg, kg, vg)
        odn_ref[0, :, MLA_WIDTH + c * LANES:MLA_WIDTH + (c + 1) * LANES] = o.astype(odn_ref.dtype)


def _ctx_attn(qna, kna, vna, qm, km, vm, qg, kg, vg, *, batch, seq):
    ins = [a.reshape(batch, seq, a.shape[-1]) for a in (qna, kna, vna, qm, km, vm, qg, kg, vg)]
    spec = lambda a: pl.BlockSpec((1, seq, a.shape[-1]), lambda b: (b, 0, 0))
    outs = [jax.ShapeDtypeStruct((batch, seq, NA_WIDTH), jnp.bfloat16),
            jax.ShapeDtypeStruct((batch, seq, DENSE_WIDTH), jnp.bfloat16)]
    ona, odn = pl.pallas_call(
        _ctx_attn_kernel,
        grid=(batch,),
        in_specs=[spec(a) for a in ins],
        out_specs=[spec(o) for o in outs],
        out_shape=outs,
        compiler_params=_params("arbitrary"),
        name="ctx_attn",
    )(*ins)
    return ona.reshape(batch * seq, NA_WIDTH), odn.reshape(batch * seq, DENSE_WIDTH)


def _na_attn_kernel(q_ref, k_ref, v_ref, kc_ref, vc_ref, pair_ref, o_ref, *, rows):
    lo = _lane_lo()
    blk = pl.program_id(1)
    row0 = blk * NA_ROWS
    key_row0 = jnp.clip(row0 - NA_KH // 2, 0, rows - NA_KEY_ROWS)
    start = pl.multiple_of(key_row0 * GRID_W, GRID_W)
    tq = NA_ROWS * GRID_W
    nkeys = NA_KEY_ROWS * GRID_W

    rq = row0 + lax.shift_right_logical(lax.broadcasted_iota(jnp.int32, (tq, 1), 0), 6)
    rk = key_row0 + lax.shift_right_logical(lax.broadcasted_iota(jnp.int32, (1, nkeys), 1), 6)
    rs = jnp.clip(rq - NA_KH // 2, 0, rows - NA_KH)
    row_mask = jnp.where((rk >= rs) & (rk < rs + NA_KH), 0.0, MASK_VALUE)

    def bias(h):
        base = key_row0 - row0 + (NA_KH - 1) + NA_PAD
        blocks = [jnp.concatenate([pair_ref[0, h, base + 2 * m - dq] for m in range(NA_KEY_ROWS // 2)], axis=1)
                  for dq in range(NA_ROWS)]
        return jnp.concatenate(blocks, axis=0) + row_mask

    for p in range(NA_WIDTH // LANES):
        cs = slice(p * LANES, (p + 1) * LANES)
        q_lo, q_hi = _split_heads(q_ref[0, :, cs])
        k = k_ref[0, pl.ds(start, nkeys), cs]
        v = v_ref[0, pl.ds(start, nkeys), cs]
        kc = kc_ref[0, 0, :, cs]
        vc = vc_ref[0, 0, :, cs]
        o_lo = _softmax_pv([_dot_nt(q_lo, k) + bias(2 * p), _dot_nt(q_lo, kc)], [v, vc])
        o_hi = _softmax_pv([_dot_nt(q_hi, k) + bias(2 * p + 1), _dot_nt(q_hi, kc)], [v, vc])
        o_ref[0, :, cs] = jnp.where(lo, o_lo, o_hi).astype(o_ref.dtype)


def _na_attn(q, k, v, kc, vc, pairs, layer, *, batch, seq):
    rows = seq // GRID_W
    nblk = rows // NA_ROWS
    tq = NA_ROWS * GRID_W
    past = kc.shape[2]
    assert GRID_W == 64 and rows >= NA_KEY_ROWS and rows % NA_ROWS == 0
    q3, k3, v3 = (a.reshape(batch, seq, NA_WIDTH) for a in (q, k, v))
    o = pl.pallas_call(
        functools.partial(_na_attn_kernel, rows=rows),
        grid=(batch, nblk),
        in_specs=[pl.BlockSpec((1, tq, NA_WIDTH), lambda b, i: (b, i, 0)),
                  pl.BlockSpec((1, seq, NA_WIDTH), lambda b, i: (b, 0, 0)),
                  pl.BlockSpec((1, seq, NA_WIDTH), lambda b, i: (b, 0, 0)),
                  pl.BlockSpec((1, 1, past, NA_WIDTH), lambda b, i: (b, layer, 0, 0)),
                  pl.BlockSpec((1, 1, past, NA_WIDTH), lambda b, i: (b, layer, 0, 0)),
                  pl.BlockSpec((1,) + pairs.shape[1:], lambda b, i: (layer, 0, 0, 0, 0))],
        out_specs=pl.BlockSpec((1, tq, NA_WIDTH), lambda b, i: (b, i, 0)),
        out_shape=jax.ShapeDtypeStruct((batch, seq, NA_WIDTH), jnp.bfloat16),
        compiler_params=_params("arbitrary", "arbitrary"),
        name="na_attn",
    )(q3, k3, v3, kc, vc, pairs)
    return o.reshape(batch * seq, NA_WIDTH)


def _na_pair_tables(na_rpb):
    cq = np.arange(GRID_W)[:, None]
    ck = np.arange(GRID_W)[None, :]
    cs = np.clip(cq - NA_KW // 2, 0, GRID_W - NA_KW)
    valid = (ck >= cs) & (ck < cs + NA_KW)
    onehot = ((ck - cq + NA_KW - 1)[None] == np.arange(2 * NA_KW - 1)[:, None, None]) & valid[None]
    t = jnp.einsum("lhdo,oqk->lhdqk", na_rpb, jnp.asarray(onehot, na_rpb.dtype), precision=lax.Precision.HIGHEST)
    t = jnp.where(jnp.asarray(valid), t, MASK_VALUE)
    t = jnp.pad(t, ((0, 0), (0, 0), (NA_PAD, NA_PAD), (0, 0), (0, 0)), constant_values=MASK_VALUE)
    return jnp.concatenate([t[:, :, :-1], t[:, :, 1:]], axis=-1)


def _dense_attn_kernel(qm_ref, qg_ref, kmc_ref, vmc_ref, kgc_ref, vgc_ref, kml_ref, vml_ref, kgl_ref, vgl_ref,
                       o_ref, s_lo_ref, s_hi_ref, *, nk, tk):
    lo = _lane_lo()
    bf = jnp.bfloat16
    past = kmc_ref.shape[2]

    def lane_fold(x, op):
        parts = [x[:, c * LANES:(c + 1) * LANES] for c in range(x.shape[1] // LANES)]
        while len(parts) > 1:
            parts = [op(parts[i], parts[i + 1]) if i + 1 < len(parts) else parts[i] for i in range(0, len(parts), 2)]
        return parts[0]

    def head(q, kc, vc, k_at, v_at, s_ref):
        spans = [(0, past)] + [(past + j * tk, tk) for j in range(nk)]
        mx = None
        for idx, (c0, width) in enumerate(spans):
            s = _dot_nt(q, kc if idx == 0 else k_at(idx - 1))
            s_ref[:, c0:c0 + width] = s
            part = lane_fold(s, jnp.maximum)
            mx = part if mx is None else jnp.maximum(mx, part)
        m = mx.max(axis=-1, keepdims=True)
        lsum = None
        acc = None
        for idx, (c0, width) in enumerate(spans):
            p = jnp.exp2(s_ref[:, c0:c0 + width] - m)
            part = lane_fold(p, jnp.add)
            pv = _dot(p.astype(bf), vc if idx == 0 else v_at(idx - 1))
            lsum = part if lsum is None else lsum + part
            acc = pv if acc is None else acc + pv
        return acc / lsum.sum(axis=-1, keepdims=True)

    def unit(q_lo, q_hi, kc_lo, kc_hi, vc, k_lo_at, k_hi_at, v_at):
        o_lo = head(q_lo, kc_lo, vc, k_lo_at, v_at, s_lo_ref)
        o_hi = head(q_hi, kc_hi, vc, k_hi_at, v_at, s_hi_ref)
        return jnp.where(lo, o_lo, o_hi)

    for p in range(MLA_HEADS // 2):
        c0 = 2 * p * LANES
        c1 = c0 + LANES
        o = unit(qm_ref[0, :, c0:c1], qm_ref[0, :, c1:c1 + LANES],
                 kmc_ref[0, 0, :, c0:c1], kmc_ref[0, 0, :, c1:c1 + LANES], vmc_ref[0, 0, :, p * LANES:(p + 1) * LANES],
                 lambda j, c0=c0, c1=c1: kml_ref[0, j * tk:(j + 1) * tk, c0:c1],
                 lambda j, c1=c1: kml_ref[0, j * tk:(j + 1) * tk, c1:c1 + LANES],
                 lambda j, p=p: vml_ref[0, j * tk:(j + 1) * tk, p * LANES:(p + 1) * LANES])
        o_ref[0, :, p * LANES:(p + 1) * LANES] = o.astype(o_ref.dtype)
    kgc = kgc_ref[0, 0]
    vgc = vgc_ref[0, 0]
    kg_at = lambda j: kgl_ref[0, j * tk:(j + 1) * tk, :]
    vg_at = lambda j: vgl_ref[0, j * tk:(j + 1) * tk, :]
    for c in range(GQA_WIDTH // LANES):
        q_lo, q_hi = _split_heads(qg_ref[0, :, c * LANES:(c + 1) * LANES])
        o = unit(q_lo, q_hi, kgc, kgc, vgc, kg_at, kg_at, vg_at)
        o_ref[0, :, MLA_WIDTH + c * LANES:MLA_WIDTH + (c + 1) * LANES] = o.astype(o_ref.dtype)


def _dense_attn(qm, qg, kmc, vmc, kgc, vgc, kml, vml, kgl, vgl, layer, *, batch, seq):
    tq, tk = DENSE_Q_TILE, DENSE_K_TILE
    r3 = lambda a: a.reshape(batch, seq, a.shape[-1])
    qm, qg, kml, vml, kgl, vgl = (r3(a) for a in (qm, qg, kml, vml, kgl, vgl))
    qspec = lambda a: pl.BlockSpec((1, tq, a.shape[-1]), lambda b, i: (b, i, 0))
    cspec = lambda a: pl.BlockSpec((1, 1) + a.shape[2:], lambda b, i: (b, layer, 0, 0))
    lspec = lambda a: pl.BlockSpec((1, seq, a.shape[-1]), lambda b, i: (b, 0, 0), pipeline_mode=pl.Buffered(1))
    o = pl.pallas_call(
        functools.partial(_dense_attn_kernel, nk=seq // tk, tk=tk),
        grid=(batch, seq // tq),
        in_specs=[qspec(qm), qspec(qg), cspec(kmc), cspec(vmc), cspec(kgc), cspec(vgc),
                  lspec(kml), lspec(vml), lspec(kgl), lspec(vgl)],
        out_specs=pl.BlockSpec((1, tq, DENSE_WIDTH), lambda b, i: (b, i, 0)),
        out_shape=jax.ShapeDtypeStruct((batch, seq, DENSE_WIDTH), jnp.bfloat16),
        scratch_shapes=[pltpu.VMEM((tq, kmc.shape[2] + seq), jnp.float32)] * 2,
        compiler_params=_params("arbitrary", "arbitrary"),
        name="dense_attn",
    )(qm, qg, kmc, vmc, kgc, vgc, kml, vml, kgl, vgl)
    return o.reshape(batch * seq, DENSE_WIDTH)


def _mla_ctx_kernel(ckv_ref, kr_ref, wuk_ref, wuv_ref, k_ref, v_ref):
    ckv = ckv_ref[0, 0]
    kn = _dot(ckv, wuk_ref[0])
    k_ref[0, 0] = (kn + jnp.concatenate([kr_ref[0, 0].astype(jnp.float32)] * MLA_HEADS, axis=1)).astype(k_ref.dtype)
    v_ref[0, 0] = _dot(ckv, wuv_ref[0]).astype(v_ref.dtype)


def _mla_ctx_expand(ckv, kr_pad, wuk, wuv):
    b, nl, past, _ = ckv.shape
    spec = lambda w: pl.BlockSpec((1, 1, past, w), lambda i, l: (i, l, 0, 0))
    wspec = lambda a: pl.BlockSpec((1,) + a.shape[1:], lambda i, l: (l, 0, 0))
    return pl.pallas_call(
        _mla_ctx_kernel,
        grid=(b, nl),
        in_specs=[spec(MLA_KV_LORA), spec(LANES), wspec(wuk), wspec(wuv)],
        out_specs=[spec(MLA_HEADS * LANES), spec(MLA_WIDTH)],
        out_shape=[jax.ShapeDtypeStruct((b, nl, past, MLA_HEADS * LANES), jnp.bfloat16),
                   jax.ShapeDtypeStruct((b, nl, past, MLA_WIDTH), jnp.bfloat16)],
        compiler_params=_params("arbitrary", "arbitrary"),
        name="mla_ctx_expand",
    )(ckv, kr_pad, wuk, wuv)


def _mix_ffn_kernel(ona_ref, odn_ref, x_ref, onap_ref, odnp_ref, xp_ref, onan_ref, odnn_ref, xn_ref, mod_ref,
                    wna_ref, wdn_ref, gpost_ref, gpre_ref, wup_ref, cw_ref, cb_ref, wd_ref, gffn_ref,
                    o_ref, hperm_ref, yperm_ref, *, seq, chunk):
    i = pl.program_id(0)
    t, d = x_ref.shape
    halo = xp_ref.shape[0]
    dff = wd_ref.shape[0]
    groups = t // 8
    mod = mod_ref[0]
    gt_m, sh_f, sc_f, gt_f = mod[2:3, :], mod[3:4, :], mod[4:5, :], mod[5:6, :]

    cat0 = lambda parts: jnp.concatenate(parts, axis=0)
    ona = cat0([ona_ref[...], onap_ref[...], onan_ref[...]])
    odn = cat0([odn_ref[...], odnp_ref[...], odnn_ref[...]])
    x = cat0([x_ref[...], xp_ref[...], xn_ref[...]])
    o = _dot(ona, wna_ref[...]) + _dot(odn, wdn_ref[...])
    x1 = x + gt_m * _rms(o, gpost_ref[...])
    h2 = _rms(x1, gpre_ref[...]) * (1.0 + sc_f) + sh_f
    o_ref[...] = x1[0:t, :]

    nlb = d // LANES
    for j in range(nlb):
        for k in range(8):
            hperm_ref[j, pl.ds(k, groups, stride=8), :] = h2[k * groups:(k + 1) * groups, j * LANES:(j + 1) * LANES]
    h = cat0([jnp.concatenate([hperm_ref[j] for j in range(nlb)], axis=1), h2[t:, :]]).astype(jnp.bfloat16)

    kk = lax.broadcasted_iota(jnp.int32, (8, 1), 0)
    tok = i * t + groups * kk
    prev_zero = jnp.bitwise_and(tok, seq - 1) == 0
    next_zero = jnp.bitwise_and(tok + groups, seq - 1) == 0

    def conv(c0):
        w = wup_ref[:, c0:c0 + chunk]
        u_all = _dot(h, w)
        u = u_all[0:t, :]
        u_hp = u_all[t + halo - 1:t + halo, :]
        u_hn = u_all[t + halo:t + halo + 1, :]
        b_prev = jnp.where(kk == 0, u_hp, pltpu.roll(u[t - 8:t, :], 1, 0))
        b_prev = jnp.where(prev_zero, 0.0, b_prev)
        b_next = jnp.where(kk == 7, u_hn, pltpu.roll(u[0:8, :], 7, 0))
        b_next = jnp.where(next_zero, 0.0, b_next)
        prev = jnp.concatenate([b_prev, u[0:t - 8, :]], axis=0)
        nxt = jnp.concatenate([u[8:t, :], b_next], axis=0)
        cw = cw_ref[:, c0:c0 + chunk]
        return prev * cw[0:1, :] + u * cw[1:2, :] + nxt * cw[2:3, :] + cb_ref[:, c0:c0 + chunk]

    acts = []
    for c in range(dff // chunk):
        gate = conv(c * chunk)
        val = conv(dff + c * chunk)
        acts.append((gate / (1.0 + jnp.exp(-gate)) * val).astype(jnp.bfloat16))
    acc = _dot(jnp.concatenate(acts, axis=1), wd_ref[...])

    y = gt_f * _rms(acc, gffn_ref[...])
    for j in range(nlb):
        yperm_ref[j] = y[:, j * LANES:(j + 1) * LANES]
    for k in range(8):
        rows = slice(k * groups, (k + 1) * groups)
        yk = jnp.concatenate([yperm_ref[j, pl.ds(k, groups, stride=8), :] for j in range(nlb)], axis=1)
        o_ref[rows, :] = o_ref[rows, :] + yk


def _mix_ffn(ona, odn, x, mods, lw, *, seq, seq_per_mod):
    ntok, d = x.shape
    t = FFN_TILE
    halo = 16
    assert seq & (seq - 1) == 0 and (t % seq == 0 or seq % t == 0) and ntok % t == 0 and t % 64 == 0
    nhalo = ntok // halo
    if seq_per_mod is None:
        mod_spec = pl.BlockSpec((1, 6, d), lambda i: (0, 0, 0))
    else:
        tps = seq_per_mod // t
        mod_spec = pl.BlockSpec((1, 6, d), lambda i: (i // tps, 0, 0))
    tok = lambda w: pl.BlockSpec((t, w), lambda i: (i, 0))
    prv = lambda w: pl.BlockSpec((halo, w), lambda i: (jnp.maximum(i * (t // halo) - 1, 0), 0))
    nxt = lambda w: pl.BlockSpec((halo, w), lambda i: (jnp.minimum((i + 1) * (t // halo), nhalo - 1), 0))
    full = lambda e: _layer_spec(e, buffered=True)
    weights = [lw["w_out_na"], lw["w_out_dn"], lw["g_post_mix"], lw["g_pre_ffn"], lw["w_up"], lw["conv_w"],
               lw["conv_b"], lw["w_down"], lw["g_post_ffn"]]
    widths = (NA_WIDTH, DENSE_WIDTH, d)
    return pl.pallas_call(
        functools.partial(_mix_ffn_kernel, seq=seq, chunk=FF_CHUNK),
        grid=(ntok // t,),
        in_specs=[tok(w) for w in widths] + [prv(w) for w in widths] + [nxt(w) for w in widths] + [mod_spec]
                 + [full(a) for a in weights],
        out_specs=tok(d),
        out_shape=jax.ShapeDtypeStruct((ntok, d), jnp.float32),
        scratch_shapes=[pltpu.VMEM((d // LANES, t, LANES), jnp.float32)] * 2,
        compiler_params=_params("arbitrary"),
        name="mix_ffn",
    )(ona, odn, x, ona, odn, x, ona, odn, x, mods, *[w[0] for w in weights])


def _prep_weights(w_in, mla_w_uq, mla_w_ukv, w_out, gqa_g_q, gqa_g_k):
    nl, d, _ = w_in.shape
    bf = jnp.bfloat16
    cat = lambda parts: jnp.concatenate(parts, axis=-1)
    o_kr = 3 * NA_WIDTH + MLA_Q_LORA + MLA_KV_LORA
    o_qg = o_kr + MLA_ROPE
    o_kg = o_qg + GQA_WIDTH
    o_vg = o_kg + GQA_KV_HEADS * HEAD_DIM
    pad_r = LANES - MLA_NOPE - MLA_ROPE
    wb = w_in.astype(bf)
    z = lambda rows, n: jnp.zeros((nl, rows, n), bf)
    kr = wb[..., o_kr:o_qg]
    qg = cat([wb[..., o_qg + HEAD_DIM * h:o_qg + HEAD_DIM * (h + 1)] for h in GQA_ORDER])
    kg = wb[..., o_kg:o_vg]
    w_all = cat([wb[..., :o_kr], qg, kg, wb[..., o_vg:], z(d, MLA_NOPE), kr, z(d, pad_r)])
    assert w_all.shape[-1] == C_END

    uq = mla_w_uq.astype(bf)
    ukv = mla_w_ukv.astype(bf)
    ql, kl = uq.shape[1], ukv.shape[1]
    qw = MLA_NOPE + MLA_ROPE
    kvw = MLA_NOPE + MLA_V
    w_uq = cat([p for h in range(MLA_HEADS) for p in (uq[..., qw * h:qw * (h + 1)], z(ql, pad_r))])
    w_uk = cat([p for h in range(MLA_HEADS) for p in (ukv[..., kvw * h:kvw * h + MLA_NOPE], z(kl, LANES - MLA_NOPE))])
    w_uv = cat([ukv[..., kvw * h + MLA_NOPE:kvw * (h + 1)] for h in range(MLA_HEADS)])

    wo = w_out.astype(bf)
    o_gqa = NA_WIDTH + MLA_WIDTH
    w_out_dn = jnp.concatenate([wo[:, NA_WIDTH:o_gqa]]
                               + [wo[:, o_gqa + HEAD_DIM * h:o_gqa + HEAD_DIM * (h + 1)] for h in GQA_ORDER], axis=1)

    ggq = jnp.tile(gqa_g_q, (1, GQA_HEADS))[:, None, :]
    ggk = jnp.tile(gqa_g_k, (1, GQA_KV_HEADS))[:, None, :]
    ones = jnp.asarray(np.kron(np.eye(LANES // HEAD_DIM), np.full((HEAD_DIM, HEAD_DIM), 1.0 / HEAD_DIM)), bf)
    return dict(w_in=w_all, w_uq=w_uq, w_uk=w_uk, w_uv=w_uv, w_out_na=wo[:, :NA_WIDTH], w_out_dn=w_out_dn,
                ggq=ggq, ggk=ggk, ones=ones)


def _rope_tables(seq):
    t = jnp.arange(seq, dtype=jnp.int32)
    r, c = (t // GRID_W).astype(jnp.float32), (t % GRID_W).astype(jnp.float32)

    def tables(dim):
        quarter = dim // 4
        freqs = ROPE_THETA ** (-jnp.arange(quarter, dtype=jnp.float32) / quarter)
        ar_, ac_ = r[:, None] * freqs[None, :], c[:, None] * freqs[None, :]
        cos = jnp.concatenate([jnp.cos(ar_), jnp.cos(ar_), jnp.cos(ac_), jnp.cos(ac_)], axis=1)
        sin = jnp.concatenate([-jnp.sin(ar_), jnp.sin(ar_), -jnp.sin(ac_), jnp.sin(ac_)], axis=1)
        return cos, sin

    c64, s64 = tables(HEAD_DIM)
    c32, s32 = tables(MLA_ROPE)
    pad = LANES - MLA_NOPE - MLA_ROPE
    cm = jnp.concatenate([jnp.ones((seq, MLA_NOPE)), c32, jnp.zeros((seq, pad))], axis=1)
    sm = jnp.concatenate([jnp.zeros((seq, MLA_NOPE)), s32, jnp.zeros((seq, pad))], axis=1)
    return dict(cg=jnp.tile(c64, (1, 2)), sg=jnp.tile(s64, (1, 2)), cm=cm.astype(jnp.float32), sm=sm.astype(jnp.float32))


def kernel(x_prompt, x_sample, c, cache_na_k, cache_na_v, cache_mla_ckv, cache_mla_krope, cache_gqa_k, cache_gqa_v, c_ctx, w_ada, b_ada, g_pre_mix, g_post_mix, g_pre_ffn, g_post_ffn, w_in, na_rpb, mla_g_q, mla_w_uq, mla_g_kv, mla_w_ukv, gqa_g_q, gqa_g_k, w_out, ffn_w_up, ffn_conv_w, ffn_conv_b, ffn_w_down):
    batch, seq, d = x_prompt.shape
    dbatch, dseq, _ = x_sample.shape
    nl = w_ada.shape[0]
    past = cache_na_k.shape[2]
    bf = jnp.bfloat16
    assert dbatch + 1 <= 8 and dseq % GRID_W == 0

    cond = jnp.zeros((8, d), jnp.float32).at[0].set(c_ctx).at[1:1 + dbatch].set(c)
    mods_all = _ada_mods(cond, w_ada, b_ada).reshape(nl, 8, 6, d)

    pw = _prep_weights(w_in, mla_w_uq, mla_w_ukv, w_out, gqa_g_q, gqa_g_k)
    rope = _rope_tables(dseq)
    vec = lambda g: g[:, None, :]
    stacked = dict(
        g_pre_mix=vec(g_pre_mix), g_post_mix=vec(g_post_mix), g_pre_ffn=vec(g_pre_ffn), g_post_ffn=vec(g_post_ffn),
        mla_g_q=vec(mla_g_q), mla_g_kv=vec(mla_g_kv), w_in=pw["w_in"],
        w_uq=pw["w_uq"], w_uk=pw["w_uk"], w_uv=pw["w_uv"], ggq=pw["ggq"], ggk=pw["ggk"],
        w_out_na=pw["w_out_na"], w_out_dn=pw["w_out_dn"],
        w_up=ffn_w_up.astype(bf), w_down=ffn_w_down.astype(bf), conv_w=ffn_conv_w, conv_b=vec(ffn_conv_b))

    kc_na = cache_na_k.reshape(dbatch, nl, past, NA_WIDTH).astype(bf)
    vc_na = cache_na_v.reshape(dbatch, nl, past, NA_WIDTH).astype(bf)
    kc_g = cache_gqa_k.reshape(dbatch, nl, past, GQA_KV_HEADS * HEAD_DIM).astype(bf)
    vc_g = cache_gqa_v.reshape(dbatch, nl, past, GQA_KV_HEADS * HEAD_DIM).astype(bf)
    kr_pad = jnp.pad(cache_mla_krope, ((0, 0), (0, 0), (0, 0), (MLA_NOPE, LANES - MLA_NOPE - MLA_ROPE))).astype(bf)
    kc_m, vc_m = _mla_ctx_expand(cache_mla_ckv.astype(bf), kr_pad, pw["w_uk"], pw["w_uv"])

    na_pairs = _na_pair_tables(na_rpb)

    xp = x_prompt.reshape(batch * seq, d)
    xs = x_sample.reshape(dbatch * dseq, d)
    caches = [[] for _ in range(6)]
    for l in range(nl):
        lw = {k: (v, l) for k, v in stacked.items()}
        lw["ones"] = (pw["ones"], None)
        mods_ctx = mods_all[l, 0:1]
        mods_lat = mods_all[l, 1:1 + dbatch]

        qna, kna, vna, qm, ckv, kr, km, vm, qg, kg, vg = _proj(xp, mods_ctx, lw, None, latent=False, seq=seq)
        for lst, a in zip(caches, (kna, vna, ckv, kr, kg, vg)):
            lst.append(a)
        ona, odn = _ctx_attn(qna, kna, vna, qm, km, vm, qg, kg, vg, batch=batch, seq=seq)
        xp = _mix_ffn(ona, odn, xp, mods_ctx, lw, seq=seq, seq_per_mod=None)

        qna, kna, vna, qm, km, vm, qg, kg, vg = _proj(xs, mods_lat, lw, rope, latent=True, seq=dseq)
        ona = _na_attn(qna, kna, vna, kc_na, vc_na, na_pairs, l, batch=dbatch, seq=dseq)
        odn = _dense_attn(qm, qg, kc_m, vc_m, kc_g, vc_g, km, vm, kg, vg, l, batch=dbatch, seq=dseq)
        xs = _mix_ffn(ona, odn, xs, mods_lat, lw, seq=dseq, seq_per_mod=dseq)

    def stack(lst, tail):
        return jnp.stack([a.reshape((batch, seq) + tail) for a in lst], axis=1)

    return (xp.reshape(batch, seq, d), xs.reshape(dbatch, dseq, d),
            stack(caches[0], (NA_HEADS, HEAD_DIM)), stack(caches[1], (NA_HEADS, HEAD_DIM)),
            stack(caches[2], (MLA_KV_LORA,)), stack(caches[3], (MLA_ROPE,)),
            stack(caches[4], (GQA_KV_HEADS, HEAD_DIM)), stack(caches[5], (GQA_KV_HEADS, HEAD_DIM)))
```

```python
import functools

import numpy as np
import jax
import jax.numpy as jnp
from jax import lax
from jax.experimental import pallas as pl
from jax.experimental.pallas import tpu as pltpu

GRID_W = 64
HEAD_DIM = 64
NA_HEADS = 6
NA_KH = 8
NA_KW = 16
MLA_HEADS = 4
MLA_Q_LORA = 256
MLA_KV_LORA = 128
MLA_NOPE = 64
MLA_ROPE = 32
MLA_V = 64
GQA_HEADS = 6
GQA_KV_HEADS = 2
GQA_GROUP = GQA_HEADS // GQA_KV_HEADS
ROPE_THETA = 10000.0
EPS = 1e-6

NA_WIDTH = NA_HEADS * HEAD_DIM
MLA_WIDTH = MLA_HEADS * MLA_V
GQA_WIDTH = GQA_HEADS * HEAD_DIM
DENSE_WIDTH = MLA_WIDTH + GQA_WIDTH

LANES = 128
VMEM_LIMIT = 52 * 1024 * 1024
MASK_VALUE = -1e30
LOG2_E = 1.4426950408889634

PROJ_TILE = 512
FFN_TILE = 512
FF_CHUNK = 256
DENSE_Q_TILE = 256
DENSE_K_TILE = 512
NA_ROWS = 4
NA_KEY_ROWS = NA_ROWS + NA_KH
NA_PAD = NA_KEY_ROWS - NA_KH

C_QNA = 0
C_KNA = C_QNA + NA_WIDTH
C_VNA = C_KNA + NA_WIDTH
C_CQ = C_VNA + NA_WIDTH
C_CKV = C_CQ + MLA_Q_LORA
C_QG = C_CKV + MLA_KV_LORA
C_KG = C_QG + GQA_WIDTH
C_VG = C_KG + LANES
C_KR = C_VG + LANES
C_END = C_KR + LANES
PROJ_GROUP = 512

GQA_ORDER = (0, 3, 1, 4, 2, 5)


def _dot(a, b):
    return jnp.dot(a, b, preferred_element_type=jnp.float32)


def _dot_nt(a, b):
    return lax.dot_general(a, b, (((1,), (1,)), ((), ())), preferred_element_type=jnp.float32)


def _params(*sem):
    return pltpu.CompilerParams(dimension_semantics=sem, vmem_limit_bytes=VMEM_LIMIT)


def _rms(x, g):
    return x * lax.rsqrt(jnp.mean(x * x, axis=-1, keepdims=True) + EPS) * g


def _head_mean_sq(x, ones_bf16):
    xx = x * x
    hi = xx.astype(jnp.bfloat16)
    lo = (xx - hi.astype(jnp.float32)).astype(jnp.bfloat16)
    return _dot(hi, ones_bf16) + _dot(lo, ones_bf16)


def _ada_kernel(cond_ref, w_ref, b_ref, o_ref):
    cnd = cond_ref[...]
    s = cnd / (1.0 + jnp.exp(-cnd))
    o_ref[0] = jnp.dot(s, w_ref[0], preferred_element_type=jnp.float32,
                       precision=lax.Precision.HIGHEST) + b_ref[0]


def _ada_mods(cond, w_ada, b_ada):
    nl, d, n6 = w_ada.shape
    tn = 1536
    return pl.pallas_call(
        _ada_kernel,
        grid=(nl, n6 // tn),
        in_specs=[pl.BlockSpec((8, d), lambda l, j: (0, 0)),
                  pl.BlockSpec((1, d, tn), lambda l, j: (l, 0, j)),
                  pl.BlockSpec((1, 1, tn), lambda l, j: (l, 0, j))],
        out_specs=pl.BlockSpec((1, 8, tn), lambda l, j: (l, 0, j)),
        out_shape=jax.ShapeDtypeStruct((nl, 8, n6), jnp.float32),
        compiler_params=_params("arbitrary", "arbitrary"),
        name="ada_mods",
    )(cond, w_ada, b_ada.reshape(nl, 1, n6))


def _swap_halves(x, half):
    lane = lax.broadcasted_iota(jnp.int32, (1, LANES), 1)
    first = jnp.bitwise_and(lane, 2 * half - 1) < half
    blocks = []
    for b in range(x.shape[1] // LANES):
        xb = x[:, b * LANES:(b + 1) * LANES]
        blocks.append(jnp.where(first, pltpu.roll(xb, LANES - half, 1), pltpu.roll(xb, half, 1)))
    return jnp.concatenate(blocks, axis=1)


def _proj_kernel(*refs, latent):
    if latent:
        (x_ref, mod_ref, gpre_ref, w_ref, gq_ref, wuq_ref, gkv_ref, wuk_ref, wuv_ref,
         ones_ref, ggq_ref, ggk_ref, cg_ref, sg_ref, cm_ref, sm_ref,
         qna_ref, kna_ref, vna_ref, qm_ref, km_ref, vm_ref, qg_ref, kg_ref, vg_ref) = refs
    else:
        (x_ref, mod_ref, gpre_ref, w_ref, gq_ref, wuq_ref, gkv_ref, wuk_ref, wuv_ref,
         ones_ref, ggq_ref, ggk_ref) = refs[:12]
        (qna_ref, kna_ref, vna_ref, qm_ref, ckv_ref, kr_ref, km_ref, vm_ref, qg_ref, kg_ref, vg_ref) = refs[18:]

    x = x_ref[...]
    mod = mod_ref[0]
    sh, sc = mod[0:1, :], mod[1:2, :]
    h = (_rms(x, gpre_ref[...]) * (1.0 + sc) + sh).astype(jnp.bfloat16)

    groups = [(c0, _dot(h, w_ref[:, c0:min(c0 + PROJ_GROUP, C_END)])) for c0 in range(0, C_END, PROJ_GROUP)]

    def piece(c0, width):
        parts = []
        for b0 in range(c0, c0 + width, LANES):
            g0, y = groups[b0 // PROJ_GROUP]
            parts.append(y[:, b0 - g0:b0 - g0 + LANES])
        return parts[0] if len(parts) == 1 else jnp.concatenate(parts, axis=1)

    na_scale = HEAD_DIM ** -0.5
    dense_unit = LOG2_E if latent else 1.0
    qna_ref[...] = (piece(C_QNA, NA_WIDTH) * na_scale).astype(qna_ref.dtype)
    def put(ref, val):
        ref[...] = val.astype(ref.dtype).reshape(ref.shape)

    put(kna_ref, piece(C_KNA, NA_WIDTH))
    put(vna_ref, piece(C_VNA, NA_WIDTH))

    mla_scale = (MLA_NOPE + MLA_ROPE) ** -0.5
    cqn = _rms(piece(C_CQ, MLA_Q_LORA), gq_ref[...]).astype(jnp.bfloat16)
    qm = _dot(cqn, wuq_ref[...])
    kr = piece(C_KR, LANES)
    if latent:
        cm = jnp.concatenate([cm_ref[...]] * MLA_HEADS, axis=1)
        sm = jnp.concatenate([sm_ref[...]] * MLA_HEADS, axis=1)
        qm = qm * cm + _swap_halves(qm, MLA_ROPE // 4) * sm
        kr = kr * cm_ref[...] + _swap_halves(kr, MLA_ROPE // 4) * sm_ref[...]
    qm_ref[...] = (qm * (mla_scale * dense_unit)).astype(qm_ref.dtype)
    ckv = _rms(piece(C_CKV, MLA_KV_LORA), gkv_ref[...])
    ckv_b = ckv.astype(jnp.bfloat16)
    kn = _dot(ckv_b, wuk_ref[...])
    km_ref[...] = (kn + jnp.concatenate([kr] * MLA_HEADS, axis=1)).astype(km_ref.dtype)
    vm_ref[...] = _dot(ckv_b, wuv_ref[...]).astype(vm_ref.dtype)
    if not latent:
        put(ckv_ref, ckv)
        put(kr_ref, kr[:, MLA_NOPE:MLA_NOPE + MLA_ROPE])

    ones = ones_ref[...]
    qg = piece(C_QG, GQA_WIDTH)
    qq = (qg * qg).astype(jnp.bfloat16)
    ms = jnp.concatenate([_dot(qq[:, b * LANES:(b + 1) * LANES], ones) for b in range(GQA_WIDTH // LANES)], axis=1)
    qg = qg * lax.rsqrt(ms + EPS) * ggq_ref[...]
    kg = piece(C_KG, LANES)
    kg = kg * lax.rsqrt(_head_mean_sq(kg, ones) + EPS) * ggk_ref[...]
    if latent:
        cg, sg = cg_ref[...], sg_ref[...]
        cg3 = jnp.concatenate([cg] * (GQA_WIDTH // LANES), axis=1)
        sg3 = jnp.concatenate([sg] * (GQA_WIDTH // LANES), axis=1)
        qg = qg * cg3 + _swap_halves(qg, HEAD_DIM // 4) * sg3
        kg = kg * cg + _swap_halves(kg, HEAD_DIM // 4) * sg
    qg_ref[...] = (qg * (na_scale * dense_unit)).astype(qg_ref.dtype)
    put(kg_ref, kg)
    put(vg_ref, piece(C_VG, LANES))


def _layer_spec(entry, buffered=False):
    a, layer = entry
    kw = dict(pipeline_mode=pl.Buffered(1)) if buffered else {}
    if layer is None:
        return pl.BlockSpec(a.shape, lambda *_: (0,) * a.ndim, **kw)
    return pl.BlockSpec((None,) + a.shape[1:], lambda *_: (layer,) + (0,) * (a.ndim - 1), **kw)


def _proj(x, mods, lw, rope, *, latent, seq, caches=None, layer=None):
    ntok, d = x.shape
    t = PROJ_TILE
    nt = ntok // t
    tiles_per_seq = seq // t if latent else 1
    tok = lambda w: pl.BlockSpec((t, w), lambda i: (i, 0))
    full = lambda e: _layer_spec(e, buffered=True)
    if latent:
        mod_spec = pl.BlockSpec((1, 6, d), lambda i: (i // tiles_per_seq, 0, 0))
    else:
        mod_spec = pl.BlockSpec((1, 6, d), lambda i: (0, 0, 0))
    kv_dt = jnp.bfloat16 if latent else jnp.float32
    bf = jnp.bfloat16
    sds = lambda w, dt: jax.ShapeDtypeStruct((ntok, w), dt)
    common = [x, mods, lw["g_pre_mix"], lw["w_in"], lw["mla_g_q"], lw["w_uq"], lw["mla_g_kv"],
              lw["w_uk"], lw["w_uv"], lw["ones"], lw["ggq"], lw["ggk"]]
    in_specs = [tok(d), mod_spec] + [full(a) for a in common[2:]]
    if latent:
        rope_spec = pl.BlockSpec((t, LANES), lambda i: (i % tiles_per_seq, 0))
        ins = common + [rope["cg"], rope["sg"], rope["cm"], rope["sm"]]
        in_specs = in_specs + [rope_spec] * 4
        outs = [sds(NA_WIDTH, bf), sds(NA_WIDTH, kv_dt), sds(NA_WIDTH, kv_dt), sds(MLA_HEADS * LANES, bf),
                sds(MLA_HEADS * LANES, bf), sds(MLA_WIDTH, bf), sds(GQA_WIDTH, bf), sds(LANES, kv_dt), sds(LANES, kv_dt)]
        aliases = {}
    else:
        ins = common + list(caches)
        in_specs = in_specs + [pl.BlockSpec(memory_space=pl.ANY)] * len(caches)
        cache_sds = [jax.ShapeDtypeStruct(a.shape, a.dtype) for a in caches]
        outs = [sds(NA_WIDTH, bf), cache_sds[0], cache_sds[1], sds(MLA_HEADS * LANES, bf), cache_sds[2], cache_sds[3],
                sds(MLA_HEADS * LANES, bf), sds(MLA_WIDTH, bf), sds(GQA_WIDTH, bf), cache_sds[4], cache_sds[5]]
        cache_out = (1, 2, 4, 5, 9, 10)
        aliases = {len(common) + k: cache_out[k] for k in range(len(caches))}
    spt = t // seq
    cache_spec = lambda o: pl.BlockSpec((spt, None, seq, o.shape[-1]), lambda i: (i, layer, 0, 0))
    out_specs = [cache_spec(o) if len(o.shape) == 4 else tok(o.shape[1]) for o in outs]
    return pl.pallas_call(
        functools.partial(_proj_kernel, latent=latent),
        grid=(nt,),
        in_specs=in_specs,
        out_specs=out_specs,
        out_shape=outs,
        input_output_aliases=aliases,
        compiler_params=_params("arbitrary"),
        name="proj_lat" if latent else "proj_ctx",
    )(*[a[0] if isinstance(a, tuple) else a for a in ins])


def _lane_lo():
    return lax.broadcasted_iota(jnp.int32, (1, LANES), 1) < HEAD_DIM


def _split_heads(q):
    lo = _lane_lo()
    zero = jnp.zeros_like(q)
    return jnp.where(lo, q, zero), jnp.where(lo, zero, q)


def _softmax_pv(scores, values):
    m = scores[0].max(axis=-1, keepdims=True)
    for s in scores[1:]:
        m = jnp.maximum(m, s.max(axis=-1, keepdims=True))
    l = None
    acc = None
    for s, v in zip(scores, values):
        p = jnp.exp(s - m)
        ps = p.sum(axis=-1, keepdims=True)
        pv = _dot(p.astype(jnp.bfloat16), v)
        l = ps if l is None else l + ps
        acc = pv if acc is None else acc + pv
    return acc / l


def _ctx_attn_kernel(qna_ref, kna_ref, vna_ref, qm_ref, km_ref, vm_ref, qg_ref, kg_ref, vg_ref, ona_ref, odn_ref):
    lo = _lane_lo()
    bf = jnp.bfloat16

    def pair(q_lo, q_hi, k_lo, k_hi, v):
        o_lo = _softmax_pv([_dot_nt(q_lo, k_lo)], [v])
        o_hi = _softmax_pv([_dot_nt(q_hi, k_hi)], [v])
        return jnp.where(lo, o_lo, o_hi)

    for p in range(NA_WIDTH // LANES):
        cs = slice(p * LANES, (p + 1) * LANES)
        q_lo, q_hi = _split_heads(qna_ref[0, :, cs])
        k = kna_ref[0, :, cs].astype(bf)
        ona_ref[0, :, cs] = pair(q_lo, q_hi, k, k, vna_ref[0, :, cs].astype(bf)).astype(ona_ref.dtype)
    for p in range(MLA_HEADS // 2):
        c0 = 2 * p * LANES
        o = pair(qm_ref[0, :, c0:c0 + LANES], qm_ref[0, :, c0 + LANES:c0 + 2 * LANES],
                 km_ref[0, :, c0:c0 + LANES], km_ref[0, :, c0 + LANES:c0 + 2 * LANES],
                 vm_ref[0, :, p * LANES:(p + 1) * LANES])
        odn_ref[0, :, p * LANES:(p + 1) * LANES] = o.astype(odn_ref.dtype)
    kg = kg_ref[0].astype(bf)
    vg = vg_ref[0].astype(bf)
    for c in range(GQA_WIDTH // LANES):
        q_lo, q_hi = _split_heads(qg_ref[0, :, c * LANES:(c + 1) * LANES])
        o = pair(q_lo, q_hi, kg, kg, vg)
        odn_ref[0, :, MLA_WIDTH + c * LANES:MLA_WIDTH + (c + 1) * LANES] = o.astype(odn_ref.dtype)


def _ctx_attn(qna, kna, vna, qm, km, vm, qg, kg, vg, layer, *, batch, seq):
    ins = [a if a.ndim == 4 else a.reshape(batch, seq, a.shape[-1]) for a in (qna, kna, vna, qm, km, vm, qg, kg, vg)]
    spec = lambda a: (pl.BlockSpec((1, None, seq, a.shape[-1]), lambda b: (b, layer, 0, 0)) if len(a.shape) == 4
                      else pl.BlockSpec((1, seq, a.shape[-1]), lambda b: (b, 0, 0)))
    outs = [jax.ShapeDtypeStruct((batch, seq, NA_WIDTH), jnp.bfloat16),
            jax.ShapeDtypeStruct((batch, seq, DENSE_WIDTH), jnp.bfloat16)]
    ona, odn = pl.pallas_call(
        _ctx_attn_kernel,
        grid=(batch,),
        in_specs=[spec(a) for a in ins],
        out_specs=[spec(o) for o in outs],
        out_shape=outs,
        compiler_params=_params("arbitrary"),
        name="ctx_attn",
    )(*ins)
    return ona.reshape(batch * seq, NA_WIDTH), odn.reshape(batch * seq, DENSE_WIDTH)


def _na_attn_kernel(q_ref, k_ref, v_ref, kc_ref, vc_ref, pair_ref, o_ref, *, rows):
    lo = _lane_lo()
    blk = pl.program_id(1)
    row0 = blk * NA_ROWS
    key_row0 = jnp.clip(row0 - NA_KH // 2, 0, rows - NA_KEY_ROWS)
    start = pl.multiple_of(key_row0 * GRID_W, GRID_W)
    tq = NA_ROWS * GRID_W
    nkeys = NA_KEY_ROWS * GRID_W

    rq = row0 + lax.shift_right_logical(lax.broadcasted_iota(jnp.int32, (tq, 1), 0), 6)
    rk = key_row0 + lax.shift_right_logical(lax.broadcasted_iota(jnp.int32, (1, nkeys), 1), 6)
    rs = jnp.clip(rq - NA_KH // 2, 0, rows - NA_KH)
    row_mask = jnp.where((rk >= rs) & (rk < rs + NA_KH), 0.0, MASK_VALUE)

    def bias(h):
        base = key_row0 - row0 + (NA_KH - 1) + NA_PAD
        blocks = [jnp.concatenate([pair_ref[0, h, base + 2 * m - dq] for m in range(NA_KEY_ROWS // 2)], axis=1)
                  for dq in range(NA_ROWS)]
        return jnp.concatenate(blocks, axis=0) + row_mask

    for p in range(NA_WIDTH // LANES):
        cs = slice(p * LANES, (p + 1) * LANES)
        q_lo, q_hi = _split_heads(q_ref[0, :, cs])
        k = k_ref[0, pl.ds(start, nkeys), cs]
        v = v_ref[0, pl.ds(start, nkeys), cs]
        kc = kc_ref[0, 0, :, cs]
        vc = vc_ref[0, 0, :, cs]
        o_lo = _softmax_pv([_dot_nt(q_lo, k) + bias(2 * p), _dot_nt(q_lo, kc)], [v, vc])
        o_hi = _softmax_pv([_dot_nt(q_hi, k) + bias(2 * p + 1), _dot_nt(q_hi, kc)], [v, vc])
        o_ref[0, :, cs] = jnp.where(lo, o_lo, o_hi).astype(o_ref.dtype)


def _na_attn(q, k, v, kc, vc, pairs, layer, *, batch, seq):
    rows = seq // GRID_W
    nblk = rows // NA_ROWS
    tq = NA_ROWS * GRID_W
    past = kc.shape[2]
    assert GRID_W == 64 and rows >= NA_KEY_ROWS and rows % NA_ROWS == 0
    q3, k3, v3 = (a.reshape(batch, seq, NA_WIDTH) for a in (q, k, v))
    o = pl.pallas_call(
        functools.partial(_na_attn_kernel, rows=rows),
        grid=(batch, nblk),
        in_specs=[pl.BlockSpec((1, tq, NA_WIDTH), lambda b, i: (b, i, 0)),
                  pl.BlockSpec((1, seq, NA_WIDTH), lambda b, i: (b, 0, 0)),
                  pl.BlockSpec((1, seq, NA_WIDTH), lambda b, i: (b, 0, 0)),
                  pl.BlockSpec((1, 1, past, NA_WIDTH), lambda b, i: (b, layer, 0, 0)),
                  pl.BlockSpec((1, 1, past, NA_WIDTH), lambda b, i: (b, layer, 0, 0)),
                  pl.BlockSpec((1,) + pairs.shape[1:], lambda b, i: (layer, 0, 0, 0, 0))],
        out_specs=pl.BlockSpec((1, tq, NA_WIDTH), lambda b, i: (b, i, 0)),
        out_shape=jax.ShapeDtypeStruct((batch, seq, NA_WIDTH), jnp.bfloat16),
        compiler_params=_params("arbitrary", "arbitrary"),
        name="na_attn",
    )(q3, k3, v3, kc, vc, pairs)
    return o.reshape(batch * seq, NA_WIDTH)


def _na_pair_tables(na_rpb):
    cq = np.arange(GRID_W)[:, None]
    ck = np.arange(GRID_W)[None, :]
    cs = np.clip(cq - NA_KW // 2, 0, GRID_W - NA_KW)
    valid = (ck >= cs) & (ck < cs + NA_KW)
    onehot = ((ck - cq + NA_KW - 1)[None] == np.arange(2 * NA_KW - 1)[:, None, None]) & valid[None]
    t = jnp.einsum("lhdo,oqk->lhdqk", na_rpb, jnp.asarray(onehot, na_rpb.dtype), precision=lax.Precision.HIGHEST)
    t = jnp.where(jnp.asarray(valid), t, MASK_VALUE)
    t = jnp.pad(t, ((0, 0), (0, 0), (NA_PAD, NA_PAD), (0, 0), (0, 0)), constant_values=MASK_VALUE)
    return jnp.concatenate([t[:, :, :-1], t[:, :, 1:]], axis=-1)


def _dense_attn_kernel(qm_ref, qg_ref, kmc_ref, vmc_ref, kgc_ref, vgc_ref, kml_ref, vml_ref, kgl_ref, vgl_ref,
                       o_ref, s_lo_ref, s_hi_ref, *, nk, tk):
    lo = _lane_lo()
    bf = jnp.bfloat16
    past = kmc_ref.shape[2]

    def lane_fold(x, op):
        parts = [x[:, c * LANES:(c + 1) * LANES] for c in range(x.shape[1] // LANES)]
        while len(parts) > 1:
            parts = [op(parts[i], parts[i + 1]) if i + 1 < len(parts) else parts[i] for i in range(0, len(parts), 2)]
        return parts[0]

    def head(q, kc, vc, k_at, v_at, s_ref):
        spans = [(0, past)] + [(past + j * tk, tk) for j in range(nk)]
        mx = None
        for idx, (c0, width) in enumerate(spans):
            s = _dot_nt(q, kc if idx == 0 else k_at(idx - 1))
            s_ref[:, c0:c0 + width] = s
            part = lane_fold(s, jnp.maximum)
            mx = part if mx is None else jnp.maximum(mx, part)
        m = mx.max(axis=-1, keepdims=True)
        lsum = None
        acc = None
        for idx, (c0, width) in enumerate(spans):
            p = jnp.exp2(s_ref[:, c0:c0 + width] - m)
            part = lane_fold(p, jnp.add)
            pv = _dot(p.astype(bf), vc if idx == 0 else v_at(idx - 1))
            lsum = part if lsum is None else lsum + part
            acc = pv if acc is None else acc + pv
        return acc / lsum.sum(axis=-1, keepdims=True)

    def unit(q_lo, q_hi, kc_lo, kc_hi, vc, k_lo_at, k_hi_at, v_at):
        o_lo = head(q_lo, kc_lo, vc, k_lo_at, v_at, s_lo_ref)
        o_hi = head(q_hi, kc_hi, vc, k_hi_at, v_at, s_hi_ref)
        return jnp.where(lo, o_lo, o_hi)

    for p in range(MLA_HEADS // 2):
        c0 = 2 * p * LANES
        c1 = c0 + LANES
        o = unit(qm_ref[0, :, c0:c1], qm_ref[0, :, c1:c1 + LANES],
                 kmc_ref[0, 0, :, c0:c1], kmc_ref[0, 0, :, c1:c1 + LANES], vmc_ref[0, 0, :, p * LANES:(p + 1) * LANES],
                 lambda j, c0=c0, c1=c1: kml_ref[0, j * tk:(j + 1) * tk, c0:c1],
                 lambda j, c1=c1: kml_ref[0, j * tk:(j + 1) * tk, c1:c1 + LANES],
                 lambda j, p=p: vml_ref[0, j * tk:(j + 1) * tk, p * LANES:(p + 1) * LANES])
        o_ref[0, :, p * LANES:(p + 1) * LANES] = o.astype(o_ref.dtype)
    kgc = kgc_ref[0, 0]
    vgc = vgc_ref[0, 0]
    kg_at = lambda j: kgl_ref[0, j * tk:(j + 1) * tk, :]
    vg_at = lambda j: vgl_ref[0, j * tk:(j + 1) * tk, :]
    for c in range(GQA_WIDTH // LANES):
        q_lo, q_hi = _split_heads(qg_ref[0, :, c * LANES:(c + 1) * LANES])
        o = unit(q_lo, q_hi, kgc, kgc, vgc, kg_at, kg_at, vg_at)
        o_ref[0, :, MLA_WIDTH + c * LANES:MLA_WIDTH + (c + 1) * LANES] = o.astype(o_ref.dtype)


def _dense_attn(qm, qg, kmc, vmc, kgc, vgc, kml, vml, kgl, vgl, layer, *, batch, seq):
    tq, tk = DENSE_Q_TILE, DENSE_K_TILE
    r3 = lambda a: a.reshape(batch, seq, a.shape[-1])
    qm, qg, kml, vml, kgl, vgl = (r3(a) for a in (qm, qg, kml, vml, kgl, vgl))
    qspec = lambda a: pl.BlockSpec((1, tq, a.shape[-1]), lambda b, i: (b, i, 0))
    cspec = lambda a: pl.BlockSpec((1, 1) + a.shape[2:], lambda b, i: (b, layer, 0, 0))
    lspec = lambda a: pl.BlockSpec((1, seq, a.shape[-1]), lambda b, i: (b, 0, 0), pipeline_mode=pl.Buffered(1))
    o = pl.pallas_call(
        functools.partial(_dense_attn_kernel, nk=seq // tk, tk=tk),
        grid=(batch, seq // tq),
        in_specs=[qspec(qm), qspec(qg), cspec(kmc), cspec(vmc), cspec(kgc), cspec(vgc),
                  lspec(kml), lspec(vml), lspec(kgl), lspec(vgl)],
        out_specs=pl.BlockSpec((1, tq, DENSE_WIDTH), lambda b, i: (b, i, 0)),
        out_shape=jax.ShapeDtypeStruct((batch, seq, DENSE_WIDTH), jnp.bfloat16),
        scratch_shapes=[pltpu.VMEM((tq, kmc.shape[2] + seq), jnp.float32)] * 2,
        compiler_params=_params("arbitrary", "arbitrary"),
        name="dense_attn",
    )(qm, qg, kmc, vmc, kgc, vgc, kml, vml, kgl, vgl)
    return o.reshape(batch * seq, DENSE_WIDTH)


def _mla_ctx_kernel(ckv_ref, kr_ref, wuk_ref, wuv_ref, k_ref, v_ref):
    ckv = ckv_ref[0, 0]
    kn = _dot(ckv, wuk_ref[0])
    k_ref[0, 0] = (kn + jnp.concatenate([kr_ref[0, 0].astype(jnp.float32)] * MLA_HEADS, axis=1)).astype(k_ref.dtype)
    v_ref[0, 0] = _dot(ckv, wuv_ref[0]).astype(v_ref.dtype)


def _mla_ctx_expand(ckv, kr_pad, wuk, wuv):
    b, nl, past, _ = ckv.shape
    spec = lambda w: pl.BlockSpec((1, 1, past, w), lambda i, l: (i, l, 0, 0))
    wspec = lambda a: pl.BlockSpec((1,) + a.shape[1:], lambda i, l: (l, 0, 0))
    return pl.pallas_call(
        _mla_ctx_kernel,
        grid=(b, nl),
        in_specs=[spec(MLA_KV_LORA), spec(LANES), wspec(wuk), wspec(wuv)],
        out_specs=[spec(MLA_HEADS * LANES), spec(MLA_WIDTH)],
        out_shape=[jax.ShapeDtypeStruct((b, nl, past, MLA_HEADS * LANES), jnp.bfloat16),
                   jax.ShapeDtypeStruct((b, nl, past, MLA_WIDTH), jnp.bfloat16)],
        compiler_params=_params("arbitrary", "arbitrary"),
        name="mla_ctx_expand",
    )(ckv, kr_pad, wuk, wuv)


def _mix_ffn_kernel(ona_ref, odn_ref, x_ref, onap_ref, odnp_ref, xp_ref, onan_ref, odnn_ref, xn_ref, mod_ref,
                    wna_ref, wdn_ref, gpost_ref, gpre_ref, wup_ref, cw_ref, cb_ref, wd_ref, gffn_ref,
                    o_ref, hperm_ref, yperm_ref, *, seq, chunk):
    i = pl.program_id(0)
    t, d = x_ref.shape
    halo = xp_ref.shape[0]
    dff = wd_ref.shape[0]
    groups = t // 8
    mod = mod_ref[0]
    gt_m, sh_f, sc_f, gt_f = mod[2:3, :], mod[3:4, :], mod[4:5, :], mod[5:6, :]

    cat0 = lambda parts: jnp.concatenate(parts, axis=0)
    ona = cat0([ona_ref[...], onap_ref[...], onan_ref[...]])
    odn = cat0([odn_ref[...], odnp_ref[...], odnn_ref[...]])
    x = cat0([x_ref[...], xp_ref[...], xn_ref[...]])
    o = _dot(jnp.concatenate([ona, odn], axis=1), jnp.concatenate([wna_ref[...], wdn_ref[...]], axis=0))
    x1 = x + gt_m * _rms(o, gpost_ref[...])
    h2 = _rms(x1, gpre_ref[...]) * (1.0 + sc_f) + sh_f
    o_ref[...] = x1[0:t, :]

    nlb = d // LANES
    for j in range(nlb):
        for k in range(8):
            hperm_ref[j, pl.ds(k, groups, stride=8), :] = h2[k * groups:(k + 1) * groups, j * LANES:(j + 1) * LANES]
    h = cat0([jnp.concatenate([hperm_ref[j] for j in range(nlb)], axis=1), h2[t:, :]]).astype(jnp.bfloat16)

    kk = lax.broadcasted_iota(jnp.int32, (8, 1), 0)
    tok = i * t + groups * kk
    prev_zero = jnp.bitwise_and(tok, seq - 1) == 0
    next_zero = jnp.bitwise_and(tok + groups, seq - 1) == 0

    def conv(c0):
        w = wup_ref[:, c0:c0 + chunk]
        u_all = _dot(h, w)
        u = u_all[0:t, :]
        u_hp = u_all[t + halo - 1:t + halo, :]
        u_hn = u_all[t + halo:t + halo + 1, :]
        b_prev = jnp.where(kk == 0, u_hp, pltpu.roll(u[t - 8:t, :], 1, 0))
        b_prev = jnp.where(prev_zero, 0.0, b_prev)
        b_next = jnp.where(kk == 7, u_hn, pltpu.roll(u[0:8, :], 7, 0))
        b_next = jnp.where(next_zero, 0.0, b_next)
        prev = jnp.concatenate([b_prev, u[0:t - 8, :]], axis=0)
        nxt = jnp.concatenate([u[8:t, :], b_next], axis=0)
        cw = cw_ref[:, c0:c0 + chunk]
        return prev * cw[0:1, :] + u * cw[1:2, :] + nxt * cw[2:3, :] + cb_ref[:, c0:c0 + chunk]

    acts = []
    for c in range(dff // chunk):
        gate = conv(c * chunk)
        val = conv(dff + c * chunk)
        acts.append((gate / (1.0 + jnp.exp(-gate)) * val).astype(jnp.bfloat16))
    acc = _dot(jnp.concatenate(acts, axis=1), wd_ref[...])

    y = gt_f * _rms(acc, gffn_ref[...])
    for j in range(nlb):
        yperm_ref[j] = y[:, j * LANES:(j + 1) * LANES]
    for k in range(8):
        rows = slice(k * groups, (k + 1) * groups)
        yk = jnp.concatenate([yperm_ref[j, pl.ds(k, groups, stride=8), :] for j in range(nlb)], axis=1)
        o_ref[rows, :] = o_ref[rows, :] + yk


def _mix_ffn(ona, odn, x, mods, lw, *, seq, seq_per_mod):
    ntok, d = x.shape
    t = FFN_TILE
    halo = 16
    assert seq & (seq - 1) == 0 and (t % seq == 0 or seq % t == 0) and ntok % t == 0 and t % 64 == 0
    nhalo = ntok // halo
    if seq_per_mod is None:
        mod_spec = pl.BlockSpec((1, 6, d), lambda i: (0, 0, 0))
    else:
        tps = seq_per_mod // t
        mod_spec = pl.BlockSpec((1, 6, d), lambda i: (i // tps, 0, 0))
    tok = lambda w: pl.BlockSpec((t, w), lambda i: (i, 0))
    prv = lambda w: pl.BlockSpec((halo, w), lambda i: (jnp.maximum(i * (t // halo) - 1, 0), 0))
    nxt = lambda w: pl.BlockSpec((halo, w), lambda i: (jnp.minimum((i + 1) * (t // halo), nhalo - 1), 0))
    full = lambda e: _layer_spec(e, buffered=True)
    weights = [lw["w_out_na"], lw["w_out_dn"], lw["g_post_mix"], lw["g_pre_ffn"], lw["w_up"], lw["conv_w"],
               lw["conv_b"], lw["w_down"], lw["g_post_ffn"]]
    widths = (NA_WIDTH, DENSE_WIDTH, d)
    return pl.pallas_call(
        functools.partial(_mix_ffn_kernel, seq=seq, chunk=FF_CHUNK),
        grid=(ntok // t,),
        in_specs=[tok(w) for w in widths] + [prv(w) for w in widths] + [nxt(w) for w in widths] + [mod_spec]
                 + [full(a) for a in weights],
        out_specs=tok(d),
        out_shape=jax.ShapeDtypeStruct((ntok, d), jnp.float32),
        scratch_shapes=[pltpu.VMEM((d // LANES, t, LANES), jnp.float32)] * 2,
        compiler_params=_params("arbitrary"),
        name="mix_ffn",
    )(ona, odn, x, ona, odn, x, ona, odn, x, mods, *[w[0] for w in weights])


def _prep_weights(w_in, mla_w_uq, mla_w_ukv, w_out, gqa_g_q, gqa_g_k):
    nl, d, _ = w_in.shape
    bf = jnp.bfloat16
    cat = lambda parts: jnp.concatenate(parts, axis=-1)
    o_kr = 3 * NA_WIDTH + MLA_Q_LORA + MLA_KV_LORA
    o_qg = o_kr + MLA_ROPE
    o_kg = o_qg + GQA_WIDTH
    o_vg = o_kg + GQA_KV_HEADS * HEAD_DIM
    pad_r = LANES - MLA_NOPE - MLA_ROPE
    wb = w_in.astype(bf)
    z = lambda rows, n: jnp.zeros((nl, rows, n), bf)
    kr = wb[..., o_kr:o_qg]
    qg = cat([wb[..., o_qg + HEAD_DIM * h:o_qg + HEAD_DIM * (h + 1)] for h in GQA_ORDER])
    kg = wb[..., o_kg:o_vg]
    w_all = cat([wb[..., :o_kr], qg, kg, wb[..., o_vg:], z(d, MLA_NOPE), kr, z(d, pad_r)])
    assert w_all.shape[-1] == C_END

    uq = mla_w_uq.astype(bf)
    ukv = mla_w_ukv.astype(bf)
    ql, kl = uq.shape[1], ukv.shape[1]
    qw = MLA_NOPE + MLA_ROPE
    kvw = MLA_NOPE + MLA_V
    w_uq = cat([p for h in range(MLA_HEADS) for p in (uq[..., qw * h:qw * (h + 1)], z(ql, pad_r))])
    w_uk = cat([p for h in range(MLA_HEADS) for p in (ukv[..., kvw * h:kvw * h + MLA_NOPE], z(kl, LANES - MLA_NOPE))])
    w_uv = cat([ukv[..., kvw * h + MLA_NOPE:kvw * (h + 1)] for h in range(MLA_HEADS)])

    wo = w_out.astype(bf)
    o_gqa = NA_WIDTH + MLA_WIDTH
    w_out_dn = jnp.concatenate([wo[:, NA_WIDTH:o_gqa]]
                               + [wo[:, o_gqa + HEAD_DIM * h:o_gqa + HEAD_DIM * (h + 1)] for h in GQA_ORDER], axis=1)

    ggq = jnp.tile(gqa_g_q, (1, GQA_HEADS))[:, None, :]
    ggk = jnp.tile(gqa_g_k, (1, GQA_KV_HEADS))[:, None, :]
    ones = jnp.asarray(np.kron(np.eye(LANES // HEAD_DIM), np.full((HEAD_DIM, HEAD_DIM), 1.0 / HEAD_DIM)), bf)
    return dict(w_in=w_all, w_uq=w_uq, w_uk=w_uk, w_uv=w_uv, w_out_na=wo[:, :NA_WIDTH], w_out_dn=w_out_dn,
                ggq=ggq, ggk=ggk, ones=ones)


def _rope_tables(seq):
    t = jnp.arange(seq, dtype=jnp.int32)
    r, c = (t // GRID_W).astype(jnp.float32), (t % GRID_W).astype(jnp.float32)

    def tables(dim):
        quarter = dim // 4
        freqs = ROPE_THETA ** (-jnp.arange(quarter, dtype=jnp.float32) / quarter)
        ar_, ac_ = r[:, None] * freqs[None, :], c[:, None] * freqs[None, :]
        cos = jnp.concatenate([jnp.cos(ar_), jnp.cos(ar_), jnp.cos(ac_), jnp.cos(ac_)], axis=1)
        sin = jnp.concatenate([-jnp.sin(ar_), jnp.sin(ar_), -jnp.sin(ac_), jnp.sin(ac_)], axis=1)
        return cos, sin

    c64, s64 = tables(HEAD_DIM)
    c32, s32 = tables(MLA_ROPE)
    pad = LANES - MLA_NOPE - MLA_ROPE
    cm = jnp.concatenate([jnp.ones((seq, MLA_NOPE)), c32, jnp.zeros((seq, pad))], axis=1)
    sm = jnp.concatenate([jnp.zeros((seq, MLA_NOPE)), s32, jnp.zeros((seq, pad))], axis=1)
    return dict(cg=jnp.tile(c64, (1, 2)), sg=jnp.tile(s64, (1, 2)), cm=cm.astype(jnp.float32), sm=sm.astype(jnp.float32))


def kernel(x_prompt, x_sample, c, cache_na_k, cache_na_v, cache_mla_ckv, cache_mla_krope, cache_gqa_k, cache_gqa_v, c_ctx, w_ada, b_ada, g_pre_mix, g_post_mix, g_pre_ffn, g_post_ffn, w_in, na_rpb, mla_g_q, mla_w_uq, mla_g_kv, mla_w_ukv, gqa_g_q, gqa_g_k, w_out, ffn_w_up, ffn_conv_w, ffn_conv_b, ffn_w_down):
    batch, seq, d = x_prompt.shape
    dbatch, dseq, _ = x_sample.shape
    nl = w_ada.shape[0]
    past = cache_na_k.shape[2]
    bf = jnp.bfloat16
    assert dbatch + 1 <= 8 and dseq % GRID_W == 0

    cond = jnp.zeros((8, d), jnp.float32).at[0].set(c_ctx).at[1:1 + dbatch].set(c)
    mods_all = _ada_mods(cond, w_ada, b_ada).reshape(nl, 8, 6, d)

    pw = _prep_weights(w_in, mla_w_uq, mla_w_ukv, w_out, gqa_g_q, gqa_g_k)
    rope = _rope_tables(dseq)
    vec = lambda g: g[:, None, :]
    stacked = dict(
        g_pre_mix=vec(g_pre_mix), g_post_mix=vec(g_post_mix), g_pre_ffn=vec(g_pre_ffn), g_post_ffn=vec(g_post_ffn),
        mla_g_q=vec(mla_g_q), mla_g_kv=vec(mla_g_kv), w_in=pw["w_in"],
        w_uq=pw["w_uq"], w_uk=pw["w_uk"], w_uv=pw["w_uv"], ggq=pw["ggq"], ggk=pw["ggk"],
        w_out_na=pw["w_out_na"], w_out_dn=pw["w_out_dn"],
        w_up=ffn_w_up.astype(bf), w_down=ffn_w_down.astype(bf), conv_w=ffn_conv_w, conv_b=vec(ffn_conv_b))

    kc_na = cache_na_k.reshape(dbatch, nl, past, NA_WIDTH).astype(bf)
    vc_na = cache_na_v.reshape(dbatch, nl, past, NA_WIDTH).astype(bf)
    kc_g = cache_gqa_k.reshape(dbatch, nl, past, GQA_KV_HEADS * HEAD_DIM).astype(bf)
    vc_g = cache_gqa_v.reshape(dbatch, nl, past, GQA_KV_HEADS * HEAD_DIM).astype(bf)
    kr_pad = jnp.pad(cache_mla_krope, ((0, 0), (0, 0), (0, 0), (MLA_NOPE, LANES - MLA_NOPE - MLA_ROPE))).astype(bf)
    kc_m, vc_m = _mla_ctx_expand(cache_mla_ckv.astype(bf), kr_pad, pw["w_uk"], pw["w_uv"])

    na_pairs = _na_pair_tables(na_rpb)

    xp = x_prompt.reshape(batch * seq, d)
    xs = x_sample.reshape(dbatch * dseq, d)
    cache_widths = (NA_WIDTH, NA_WIDTH, MLA_KV_LORA, MLA_ROPE, GQA_KV_HEADS * HEAD_DIM, GQA_KV_HEADS * HEAD_DIM)
    caches = [jnp.zeros((batch, nl, seq, w), jnp.float32) for w in cache_widths]
    for l in range(nl):
        lw = {k: (v, l) for k, v in stacked.items()}
        lw["ones"] = (pw["ones"], None)
        mods_ctx = mods_all[l, 0:1]
        mods_lat = mods_all[l, 1:1 + dbatch]

        qna, kna, vna, qm, ckv, kr, km, vm, qg, kg, vg = _proj(xp, mods_ctx, lw, None, latent=False, seq=seq,
                                                               caches=caches, layer=l)
        caches = [kna, vna, ckv, kr, kg, vg]
        ona, odn = _ctx_attn(qna, kna, vna, qm, km, vm, qg, kg, vg, l, batch=batch, seq=seq)
        xp = _mix_ffn(ona, odn, xp, mods_ctx, lw, seq=seq, seq_per_mod=None)

        qna, kna, vna, qm, km, vm, qg, kg, vg = _proj(xs, mods_lat, lw, rope, latent=True, seq=dseq)
        ona = _na_attn(qna, kna, vna, kc_na, vc_na, na_pairs, l, batch=dbatch, seq=dseq)
        odn = _dense_attn(qm, qg, kc_m, vc_m, kc_g, vc_g, km, vm, kg, vg, l, batch=dbatch, seq=dseq)
        xs = _mix_ffn(ona, odn, xs, mods_lat, lw, seq=dseq, seq_per_mod=dseq)

    heads = lambda a, h: a.reshape(a.shape[:3] + (h, HEAD_DIM))
    return (xp.reshape(batch, seq, d), xs.reshape(dbatch, dseq, d),
            heads(caches[0], NA_HEADS), heads(caches[1], NA_HEADS), caches[2], caches[3],
            heads(caches[4], GQA_KV_HEADS), heads(caches[5], GQA_KV_HEADS))
```

```python
import functools

import numpy as np
import jax
import jax.numpy as jnp
from jax import lax
from jax.experimental import pallas as pl
from jax.experimental.pallas import tpu as pltpu

GRID_W = 64
HEAD_DIM = 64
NA_HEADS = 6
NA_KH = 8
NA_KW = 16
MLA_HEADS = 4
MLA_Q_LORA = 256
MLA_KV_LORA = 128
MLA_NOPE = 64
MLA_ROPE = 32
MLA_V = 64
GQA_HEADS = 6
GQA_KV_HEADS = 2
GQA_GROUP = GQA_HEADS // GQA_KV_HEADS
ROPE_THETA = 10000.0
EPS = 1e-6

NA_WIDTH = NA_HEADS * HEAD_DIM
MLA_WIDTH = MLA_HEADS * MLA_V
GQA_WIDTH = GQA_HEADS * HEAD_DIM
DENSE_WIDTH = MLA_WIDTH + GQA_WIDTH

LANES = 128
VMEM_LIMIT = 52 * 1024 * 1024
MASK_VALUE = -1e30
LOG2_E = 1.4426950408889634

PROJ_TILE = 512
FFN_TILE = 512
FF_CHUNK = 256
DENSE_Q_TILE = 256
CTX_BATCHES = 4
DENSE_K_TILE = 256
NA_ROWS = 4
NA_KEY_ROWS = NA_ROWS + NA_KH
NA_PAD = NA_KEY_ROWS - NA_KH

C_QNA = 0
C_KNA = C_QNA + NA_WIDTH
C_VNA = C_KNA + NA_WIDTH
C_CQ = C_VNA + NA_WIDTH
C_CKV = C_CQ + MLA_Q_LORA
C_QG = C_CKV + MLA_KV_LORA
C_KG = C_QG + GQA_WIDTH
C_VG = C_KG + LANES
C_KR = C_VG + LANES
C_END = C_KR + LANES
PROJ_GROUP = 512

GQA_ORDER = (0, 3, 1, 4, 2, 5)


def _dot(a, b):
    return jnp.dot(a, b, preferred_element_type=jnp.float32)


def _dot_nt(a, b):
    return lax.dot_general(a, b, (((1,), (1,)), ((), ())), preferred_element_type=jnp.float32)


def _params(*sem):
    return pltpu.CompilerParams(dimension_semantics=sem, vmem_limit_bytes=VMEM_LIMIT)


def _rms(x, g):
    return x * lax.rsqrt(jnp.mean(x * x, axis=-1, keepdims=True) + EPS) * g


def _head_mean_sq(x, ones_bf16):
    xx = x * x
    hi = xx.astype(jnp.bfloat16)
    lo = (xx - hi.astype(jnp.float32)).astype(jnp.bfloat16)
    return _dot(hi, ones_bf16) + _dot(lo, ones_bf16)


def _ada_kernel(cond_ref, w_ref, b_ref, o_ref):
    cnd = cond_ref[...]
    s = cnd / (1.0 + jnp.exp(-cnd))
    o_ref[0] = jnp.dot(s, w_ref[0], preferred_element_type=jnp.float32,
                       precision=lax.Precision.HIGHEST) + b_ref[0]


def _ada_mods(cond, w_ada, b_ada):
    nl, d, n6 = w_ada.shape
    tn = 1536
    return pl.pallas_call(
        _ada_kernel,
        grid=(nl, n6 // tn),
        in_specs=[pl.BlockSpec((8, d), lambda l, j: (0, 0)),
                  pl.BlockSpec((1, d, tn), lambda l, j: (l, 0, j)),
                  pl.BlockSpec((1, 1, tn), lambda l, j: (l, 0, j))],
        out_specs=pl.BlockSpec((1, 8, tn), lambda l, j: (l, 0, j)),
        out_shape=jax.ShapeDtypeStruct((nl, 8, n6), jnp.float32),
        compiler_params=_params("arbitrary", "arbitrary"),
        name="ada_mods",
    )(cond, w_ada, b_ada.reshape(nl, 1, n6))


def _swap_halves(x, half):
    lane = lax.broadcasted_iota(jnp.int32, (1, LANES), 1)
    first = jnp.bitwise_and(lane, 2 * half - 1) < half
    blocks = []
    for b in range(x.shape[1] // LANES):
        xb = x[:, b * LANES:(b + 1) * LANES]
        blocks.append(jnp.where(first, pltpu.roll(xb, LANES - half, 1), pltpu.roll(xb, half, 1)))
    return jnp.concatenate(blocks, axis=1)


def _proj_kernel(*refs, latent):
    if latent:
        (x_ref, mod_ref, gpre_ref, w_ref, gq_ref, wuq_ref, gkv_ref, wuk_ref, wuv_ref,
         ones_ref, ggq_ref, ggk_ref, cg_ref, sg_ref, cm_ref, sm_ref,
         qna_ref, kna_ref, vna_ref, qm_ref, km_ref, vm_ref, qg_ref, kg_ref, vg_ref) = refs
    else:
        (x_ref, mod_ref, gpre_ref, w_ref, gq_ref, wuq_ref, gkv_ref, wuk_ref, wuv_ref,
         ones_ref, ggq_ref, ggk_ref) = refs[:12]
        (qna_ref, kna_ref, vna_ref, qm_ref, ckv_ref, kr_ref, km_ref, vm_ref, qg_ref, kg_ref, vg_ref) = refs[18:]

    x = x_ref[...]
    mod = mod_ref[0]
    sh, sc = mod[0:1, :], mod[1:2, :]
    h = (_rms(x, gpre_ref[...]) * (1.0 + sc) + sh).astype(jnp.bfloat16)

    groups = [(c0, _dot(h, w_ref[:, c0:min(c0 + PROJ_GROUP, C_END)])) for c0 in range(0, C_END, PROJ_GROUP)]

    def piece(c0, width):
        parts = []
        for b0 in range(c0, c0 + width, LANES):
            g0, y = groups[b0 // PROJ_GROUP]
            parts.append(y[:, b0 - g0:b0 - g0 + LANES])
        return parts[0] if len(parts) == 1 else jnp.concatenate(parts, axis=1)

    na_scale = HEAD_DIM ** -0.5
    dense_unit = LOG2_E if latent else 1.0
    qna_ref[...] = (piece(C_QNA, NA_WIDTH) * na_scale).astype(qna_ref.dtype)
    def put(ref, val):
        ref[...] = val.astype(ref.dtype).reshape(ref.shape)

    put(kna_ref, piece(C_KNA, NA_WIDTH))
    put(vna_ref, piece(C_VNA, NA_WIDTH))

    mla_scale = (MLA_NOPE + MLA_ROPE) ** -0.5
    cqn = _rms(piece(C_CQ, MLA_Q_LORA), gq_ref[...]).astype(jnp.bfloat16)
    qm = _dot(cqn, wuq_ref[...])
    kr = piece(C_KR, LANES)
    if latent:
        cm = jnp.concatenate([cm_ref[...]] * MLA_HEADS, axis=1)
        sm = jnp.concatenate([sm_ref[...]] * MLA_HEADS, axis=1)
        qm = qm * cm + _swap_halves(qm, MLA_ROPE // 4) * sm
        kr = kr * cm_ref[...] + _swap_halves(kr, MLA_ROPE // 4) * sm_ref[...]
    qm_ref[...] = (qm * (mla_scale * dense_unit)).astype(qm_ref.dtype)
    ckv = _rms(piece(C_CKV, MLA_KV_LORA), gkv_ref[...])
    ckv_b = ckv.astype(jnp.bfloat16)
    kn = _dot(ckv_b, wuk_ref[...])
    km_ref[...] = (kn + jnp.concatenate([kr] * MLA_HEADS, axis=1)).astype(km_ref.dtype)
    vm_ref[...] = _dot(ckv_b, wuv_ref[...]).astype(vm_ref.dtype)
    if not latent:
        put(ckv_ref, ckv)
        put(kr_ref, kr[:, MLA_NOPE:MLA_NOPE + MLA_ROPE])

    ones = ones_ref[...]
    qg = piece(C_QG, GQA_WIDTH)
    qq = (qg * qg).astype(jnp.bfloat16)
    ms = jnp.concatenate([_dot(qq[:, b * LANES:(b + 1) * LANES], ones) for b in range(GQA_WIDTH // LANES)], axis=1)
    qg = qg * lax.rsqrt(ms + EPS) * ggq_ref[...]
    kg = piece(C_KG, LANES)
    kg = kg * lax.rsqrt(_head_mean_sq(kg, ones) + EPS) * ggk_ref[...]
    if latent:
        cg, sg = cg_ref[...], sg_ref[...]
        cg3 = jnp.concatenate([cg] * (GQA_WIDTH // LANES), axis=1)
        sg3 = jnp.concatenate([sg] * (GQA_WIDTH // LANES), axis=1)
        qg = qg * cg3 + _swap_halves(qg, HEAD_DIM // 4) * sg3
        kg = kg * cg + _swap_halves(kg, HEAD_DIM // 4) * sg
    qg_ref[...] = (qg * (na_scale * dense_unit)).astype(qg_ref.dtype)
    put(kg_ref, kg)
    put(vg_ref, piece(C_VG, LANES))


def _layer_spec(entry, buffered=False):
    a, layer = entry
    kw = dict(pipeline_mode=pl.Buffered(1)) if buffered else {}
    if layer is None:
        return pl.BlockSpec(a.shape, lambda *_: (0,) * a.ndim, **kw)
    return pl.BlockSpec((None,) + a.shape[1:], lambda *_: (layer,) + (0,) * (a.ndim - 1), **kw)


def _proj(x, mods, lw, rope, *, latent, seq, caches=None, layer=None):
    ntok, d = x.shape
    t = PROJ_TILE
    nt = ntok // t
    tiles_per_seq = seq // t if latent else 1
    tok = lambda w: pl.BlockSpec((t, w), lambda i: (i, 0))
    full = lambda e: _layer_spec(e, buffered=True)
    if latent:
        mod_spec = pl.BlockSpec((1, 6, d), lambda i: (i // tiles_per_seq, 0, 0))
    else:
        mod_spec = pl.BlockSpec((1, 6, d), lambda i: (0, 0, 0))
    kv_dt = jnp.bfloat16 if latent else jnp.float32
    bf = jnp.bfloat16
    sds = lambda w, dt: jax.ShapeDtypeStruct((ntok, w), dt)
    common = [x, mods, lw["g_pre_mix"], lw["w_in"], lw["mla_g_q"], lw["w_uq"], lw["mla_g_kv"],
              lw["w_uk"], lw["w_uv"], lw["ones"], lw["ggq"], lw["ggk"]]
    in_specs = [tok(d), mod_spec] + [full(a) for a in common[2:]]
    if latent:
        rope_spec = pl.BlockSpec((t, LANES), lambda i: (i % tiles_per_seq, 0))
        ins = common + [rope["cg"], rope["sg"], rope["cm"], rope["sm"]]
        in_specs = in_specs + [rope_spec] * 4
        outs = [sds(NA_WIDTH, bf), sds(NA_WIDTH, kv_dt), sds(NA_WIDTH, kv_dt), sds(MLA_HEADS * LANES, bf),
                sds(MLA_HEADS * LANES, bf), sds(MLA_WIDTH, bf), sds(GQA_WIDTH, bf), sds(LANES, kv_dt), sds(LANES, kv_dt)]
        aliases = {}
    else:
        ins = common + list(caches)
        in_specs = in_specs + [pl.BlockSpec(memory_space=pl.ANY)] * len(caches)
        cache_sds = [jax.ShapeDtypeStruct(a.shape, a.dtype) for a in caches]
        outs = [sds(NA_WIDTH, bf), cache_sds[0], cache_sds[1], sds(MLA_HEADS * LANES, bf), cache_sds[2], cache_sds[3],
                sds(MLA_HEADS * LANES, bf), sds(MLA_WIDTH, bf), sds(GQA_WIDTH, bf), cache_sds[4], cache_sds[5]]
        cache_out = (1, 2, 4, 5, 9, 10)
        aliases = {len(common) + k: cache_out[k] for k in range(len(caches))}
    spt = t // seq
    cache_spec = lambda o: pl.BlockSpec((spt, None, seq, o.shape[-1]), lambda i: (i, layer, 0, 0))
    out_specs = [cache_spec(o) if len(o.shape) == 4 else tok(o.shape[1]) for o in outs]
    return pl.pallas_call(
        functools.partial(_proj_kernel, latent=latent),
        grid=(nt,),
        in_specs=in_specs,
        out_specs=out_specs,
        out_shape=outs,
        input_output_aliases=aliases,
        compiler_params=_params("arbitrary"),
        name="proj_lat" if latent else "proj_ctx",
    )(*[a[0] if isinstance(a, tuple) else a for a in ins])


def _lane_lo():
    return lax.broadcasted_iota(jnp.int32, (1, LANES), 1) < HEAD_DIM


def _split_heads(q):
    lo = _lane_lo()
    zero = jnp.zeros_like(q)
    return jnp.where(lo, q, zero), jnp.where(lo, zero, q)


def _softmax_pv(scores, values):
    m = scores[0].max(axis=-1, keepdims=True)
    for s in scores[1:]:
        m = jnp.maximum(m, s.max(axis=-1, keepdims=True))
    l = None
    acc = None
    for s, v in zip(scores, values):
        p = jnp.exp(s - m)
        ps = p.sum(axis=-1, keepdims=True)
        pv = _dot(p.astype(jnp.bfloat16), v)
        l = ps if l is None else l + ps
        acc = pv if acc is None else acc + pv
    return acc / l


def _ctx_attn_kernel(qna_ref, kna_ref, vna_ref, qm_ref, km_ref, vm_ref, qg_ref, kg_ref, vg_ref, ona_ref, odn_ref):
    lo = _lane_lo()
    bf = jnp.bfloat16

    def pair(q_lo, q_hi, k_lo, k_hi, v):
        o_lo = _softmax_pv([_dot_nt(q_lo, k_lo)], [v])
        o_hi = _softmax_pv([_dot_nt(q_hi, k_hi)], [v])
        return jnp.where(lo, o_lo, o_hi)

    for b in range(qna_ref.shape[0]):
        for p in range(NA_WIDTH // LANES):
            cs = slice(p * LANES, (p + 1) * LANES)
            q_lo, q_hi = _split_heads(qna_ref[b, :, cs])
            k = kna_ref[b, :, cs].astype(bf)
            ona_ref[b, :, cs] = pair(q_lo, q_hi, k, k, vna_ref[b, :, cs].astype(bf)).astype(ona_ref.dtype)
        for p in range(MLA_HEADS // 2):
            c0 = 2 * p * LANES
            o = pair(qm_ref[b, :, c0:c0 + LANES], qm_ref[b, :, c0 + LANES:c0 + 2 * LANES],
                     km_ref[b, :, c0:c0 + LANES], km_ref[b, :, c0 + LANES:c0 + 2 * LANES],
                     vm_ref[b, :, p * LANES:(p + 1) * LANES])
            odn_ref[b, :, p * LANES:(p + 1) * LANES] = o.astype(odn_ref.dtype)
        kg = kg_ref[b].astype(bf)
        vg = vg_ref[b].astype(bf)
        for c in range(GQA_WIDTH // LANES):
            q_lo, q_hi = _split_heads(qg_ref[b, :, c * LANES:(c + 1) * LANES])
            o = pair(q_lo, q_hi, kg, kg, vg)
            odn_ref[b, :, MLA_WIDTH + c * LANES:MLA_WIDTH + (c + 1) * LANES] = o.astype(odn_ref.dtype)


def _ctx_attn(qna, kna, vna, qm, km, vm, qg, kg, vg, layer, *, batch, seq):
    nb = CTX_BATCHES
    ins = [a if a.ndim == 4 else a.reshape(batch, seq, a.shape[-1]) for a in (qna, kna, vna, qm, km, vm, qg, kg, vg)]
    spec = lambda a: (pl.BlockSpec((nb, None, seq, a.shape[-1]), lambda b: (b, layer, 0, 0)) if len(a.shape) == 4
                      else pl.BlockSpec((nb, seq, a.shape[-1]), lambda b: (b, 0, 0)))
    outs = [jax.ShapeDtypeStruct((batch, seq, NA_WIDTH), jnp.bfloat16),
            jax.ShapeDtypeStruct((batch, seq, DENSE_WIDTH), jnp.bfloat16)]
    ona, odn = pl.pallas_call(
        _ctx_attn_kernel,
        grid=(batch // nb,),
        in_specs=[spec(a) for a in ins],
        out_specs=[spec(o) for o in outs],
        out_shape=outs,
        compiler_params=_params("arbitrary"),
        name="ctx_attn",
    )(*ins)
    return ona.reshape(batch * seq, NA_WIDTH), odn.reshape(batch * seq, DENSE_WIDTH)


def _na_attn_kernel(q_ref, k_ref, v_ref, kc_ref, vc_ref, pair_ref, o_ref, *, rows):
    lo = _lane_lo()
    blk = pl.program_id(1)
    row0 = blk * NA_ROWS
    key_row0 = jnp.clip(row0 - NA_KH // 2, 0, rows - NA_KEY_ROWS)
    start = pl.multiple_of(key_row0 * GRID_W, GRID_W)
    tq = NA_ROWS * GRID_W
    nkeys = NA_KEY_ROWS * GRID_W

    rq = row0 + lax.shift_right_logical(lax.broadcasted_iota(jnp.int32, (tq, 1), 0), 6)
    rk = key_row0 + lax.shift_right_logical(lax.broadcasted_iota(jnp.int32, (1, nkeys), 1), 6)
    rs = jnp.clip(rq - NA_KH // 2, 0, rows - NA_KH)
    row_mask = jnp.where((rk >= rs) & (rk < rs + NA_KH), 0.0, MASK_VALUE)

    def bias(h):
        base = key_row0 - row0 + (NA_KH - 1) + NA_PAD
        blocks = [jnp.concatenate([pair_ref[0, h, base + 2 * m - dq] for m in range(NA_KEY_ROWS // 2)], axis=1)
                  for dq in range(NA_ROWS)]
        return jnp.concatenate(blocks, axis=0) + row_mask

    for p in range(NA_WIDTH // LANES):
        cs = slice(p * LANES, (p + 1) * LANES)
        q_lo, q_hi = _split_heads(q_ref[0, :, cs])
        k = k_ref[0, pl.ds(start, nkeys), cs]
        v = v_ref[0, pl.ds(start, nkeys), cs]
        kc = kc_ref[0, 0, :, cs]
        vc = vc_ref[0, 0, :, cs]
        o_lo = _softmax_pv([_dot_nt(q_lo, k) + bias(2 * p), _dot_nt(q_lo, kc)], [v, vc])
        o_hi = _softmax_pv([_dot_nt(q_hi, k) + bias(2 * p + 1), _dot_nt(q_hi, kc)], [v, vc])
        o_ref[0, :, cs] = jnp.where(lo, o_lo, o_hi).astype(o_ref.dtype)


def _na_attn(q, k, v, kc, vc, pairs, layer, *, batch, seq):
    rows = seq // GRID_W
    nblk = rows // NA_ROWS
    tq = NA_ROWS * GRID_W
    past = kc.shape[2]
    assert GRID_W == 64 and rows >= NA_KEY_ROWS and rows % NA_ROWS == 0
    q3, k3, v3 = (a.reshape(batch, seq, NA_WIDTH) for a in (q, k, v))
    o = pl.pallas_call(
        functools.partial(_na_attn_kernel, rows=rows),
        grid=(batch, nblk),
        in_specs=[pl.BlockSpec((1, tq, NA_WIDTH), lambda b, i: (b, i, 0)),
                  pl.BlockSpec((1, seq, NA_WIDTH), lambda b, i: (b, 0, 0)),
                  pl.BlockSpec((1, seq, NA_WIDTH), lambda b, i: (b, 0, 0)),
                  pl.BlockSpec((1, 1, past, NA_WIDTH), lambda b, i: (b, layer, 0, 0)),
                  pl.BlockSpec((1, 1, past, NA_WIDTH), lambda b, i: (b, layer, 0, 0)),
                  pl.BlockSpec((1,) + pairs.shape[1:], lambda b, i: (layer, 0, 0, 0, 0))],
        out_specs=pl.BlockSpec((1, tq, NA_WIDTH), lambda b, i: (b, i, 0)),
        out_shape=jax.ShapeDtypeStruct((batch, seq, NA_WIDTH), jnp.bfloat16),
        compiler_params=_params("arbitrary", "arbitrary"),
        name="na_attn",
    )(q3, k3, v3, kc, vc, pairs)
    return o.reshape(batch * seq, NA_WIDTH)


def _na_pair_tables(na_rpb):
    cq = np.arange(GRID_W)[:, None]
    ck = np.arange(GRID_W)[None, :]
    cs = np.clip(cq - NA_KW // 2, 0, GRID_W - NA_KW)
    valid = (ck >= cs) & (ck < cs + NA_KW)
    onehot = ((ck - cq + NA_KW - 1)[None] == np.arange(2 * NA_KW - 1)[:, None, None]) & valid[None]
    t = jnp.einsum("lhdo,oqk->lhdqk", na_rpb, jnp.asarray(onehot, na_rpb.dtype), precision=lax.Precision.HIGHEST)
    t = jnp.where(jnp.asarray(valid), t, MASK_VALUE)
    t = jnp.pad(t, ((0, 0), (0, 0), (NA_PAD, NA_PAD), (0, 0), (0, 0)), constant_values=MASK_VALUE)
    return jnp.concatenate([t[:, :, :-1], t[:, :, 1:]], axis=-1)


def _dense_attn_kernel(qm_ref, qg_ref, kmc_ref, vmc_ref, kgc_ref, vgc_ref, kml_ref, vml_ref, kgl_ref, vgl_ref,
                       o_ref, s_lo_ref, s_hi_ref, *, nk, tk):
    lo = _lane_lo()
    bf = jnp.bfloat16
    past = kmc_ref.shape[2]

    def lane_fold(x, op):
        parts = [x[:, c * LANES:(c + 1) * LANES] for c in range(x.shape[1] // LANES)]
        while len(parts) > 1:
            parts = [op(parts[i], parts[i + 1]) if i + 1 < len(parts) else parts[i] for i in range(0, len(parts), 2)]
        return parts[0]

    def head(q, kc, vc, k_at, v_at, s_ref):
        spans = [(0, past)] + [(past + j * tk, tk) for j in range(nk)]
        mx = None
        for idx, (c0, width) in enumerate(spans):
            s = _dot_nt(q, kc if idx == 0 else k_at(idx - 1))
            s_ref[:, c0:c0 + width] = s
            part = lane_fold(s, jnp.maximum)
            mx = part if mx is None else jnp.maximum(mx, part)
        m = mx.max(axis=-1, keepdims=True)
        lsum = None
        acc = None
        for idx, (c0, width) in enumerate(spans):
            p = jnp.exp2(s_ref[:, c0:c0 + width] - m)
            part = lane_fold(p, jnp.add)
            pv = _dot(p.astype(bf), vc if idx == 0 else v_at(idx - 1))
            lsum = part if lsum is None else lsum + part
            acc = pv if acc is None else acc + pv
        return acc / lsum.sum(axis=-1, keepdims=True)

    def unit(q_lo, q_hi, kc_lo, kc_hi, vc, k_lo_at, k_hi_at, v_at):
        o_lo = head(q_lo, kc_lo, vc, k_lo_at, v_at, s_lo_ref)
        o_hi = head(q_hi, kc_hi, vc, k_hi_at, v_at, s_hi_ref)
        return jnp.where(lo, o_lo, o_hi)

    for p in range(MLA_HEADS // 2):
        c0 = 2 * p * LANES
        c1 = c0 + LANES
        o = unit(qm_ref[0, :, c0:c1], qm_ref[0, :, c1:c1 + LANES],
                 kmc_ref[0, 0, :, c0:c1], kmc_ref[0, 0, :, c1:c1 + LANES], vmc_ref[0, 0, :, p * LANES:(p + 1) * LANES],
                 lambda j, c0=c0, c1=c1: kml_ref[0, j * tk:(j + 1) * tk, c0:c1],
                 lambda j, c1=c1: kml_ref[0, j * tk:(j + 1) * tk, c1:c1 + LANES],
                 lambda j, p=p: vml_ref[0, j * tk:(j + 1) * tk, p * LANES:(p + 1) * LANES])
        o_ref[0, :, p * LANES:(p + 1) * LANES] = o.astype(o_ref.dtype)
    kgc = kgc_ref[0, 0]
    vgc = vgc_ref[0, 0]
    kg_at = lambda j: kgl_ref[0, j * tk:(j + 1) * tk, :]
    vg_at = lambda j: vgl_ref[0, j * tk:(j + 1) * tk, :]
    for c in range(GQA_WIDTH // LANES):
        q_lo, q_hi = _split_heads(qg_ref[0, :, c * LANES:(c + 1) * LANES])
        o = unit(q_lo, q_hi, kgc, kgc, vgc, kg_at, kg_at, vg_at)
        o_ref[0, :, MLA_WIDTH + c * LANES:MLA_WIDTH + (c + 1) * LANES] = o.astype(o_ref.dtype)


def _dense_attn(qm, qg, kmc, vmc, kgc, vgc, kml, vml, kgl, vgl, layer, *, batch, seq):
    tq, tk = DENSE_Q_TILE, DENSE_K_TILE
    r3 = lambda a: a.reshape(batch, seq, a.shape[-1])
    qm, qg, kml, vml, kgl, vgl = (r3(a) for a in (qm, qg, kml, vml, kgl, vgl))
    qspec = lambda a: pl.BlockSpec((1, tq, a.shape[-1]), lambda b, i: (b, i, 0))
    cspec = lambda a: pl.BlockSpec((1, 1) + a.shape[2:], lambda b, i: (b, layer, 0, 0))
    lspec = lambda a: pl.BlockSpec((1, seq, a.shape[-1]), lambda b, i: (b, 0, 0), pipeline_mode=pl.Buffered(1))
    o = pl.pallas_call(
        functools.partial(_dense_attn_kernel, nk=seq // tk, tk=tk),
        grid=(batch, seq // tq),
        in_specs=[qspec(qm), qspec(qg), cspec(kmc), cspec(vmc), cspec(kgc), cspec(vgc),
                  lspec(kml), lspec(vml), lspec(kgl), lspec(vgl)],
        out_specs=pl.BlockSpec((1, tq, DENSE_WIDTH), lambda b, i: (b, i, 0)),
        out_shape=jax.ShapeDtypeStruct((batch, seq, DENSE_WIDTH), jnp.bfloat16),
        scratch_shapes=[pltpu.VMEM((tq, kmc.shape[2] + seq), jnp.float32)] * 2,
        compiler_params=_params("arbitrary", "arbitrary"),
        name="dense_attn",
    )(qm, qg, kmc, vmc, kgc, vgc, kml, vml, kgl, vgl)
    return o.reshape(batch * seq, DENSE_WIDTH)


def _mla_ctx_kernel(ckv_ref, kr_ref, wuk_ref, wuv_ref, k_ref, v_ref):
    ckv = ckv_ref[0, 0]
    kn = _dot(ckv, wuk_ref[0])
    k_ref[0, 0] = (kn + jnp.concatenate([kr_ref[0, 0].astype(jnp.float32)] * MLA_HEADS, axis=1)).astype(k_ref.dtype)
    v_ref[0, 0] = _dot(ckv, wuv_ref[0]).astype(v_ref.dtype)


def _mla_ctx_expand(ckv, kr_pad, wuk, wuv):
    b, nl, past, _ = ckv.shape
    spec = lambda w: pl.BlockSpec((1, 1, past, w), lambda i, l: (i, l, 0, 0))
    wspec = lambda a: pl.BlockSpec((1,) + a.shape[1:], lambda i, l: (l, 0, 0))
    return pl.pallas_call(
        _mla_ctx_kernel,
        grid=(b, nl),
        in_specs=[spec(MLA_KV_LORA), spec(LANES), wspec(wuk), wspec(wuv)],
        out_specs=[spec(MLA_HEADS * LANES), spec(MLA_WIDTH)],
        out_shape=[jax.ShapeDtypeStruct((b, nl, past, MLA_HEADS * LANES), jnp.bfloat16),
                   jax.ShapeDtypeStruct((b, nl, past, MLA_WIDTH), jnp.bfloat16)],
        compiler_params=_params("arbitrary", "arbitrary"),
        name="mla_ctx_expand",
    )(ckv, kr_pad, wuk, wuv)


def _mix_ffn_kernel(ona_ref, odn_ref, x_ref, onap_ref, odnp_ref, xp_ref, onan_ref, odnn_ref, xn_ref, mod_ref,
                    wna_ref, wdn_ref, gpost_ref, gpre_ref, wup_ref, cw_ref, cb_ref, wd_ref, gffn_ref,
                    o_ref, hperm_ref, yperm_ref, *, seq, chunk):
    i = pl.program_id(0)
    t, d = x_ref.shape
    halo = xp_ref.shape[0]
    dff = wd_ref.shape[0]
    groups = t // 8
    mod = mod_ref[0]
    gt_m, sh_f, sc_f, gt_f = mod[2:3, :], mod[3:4, :], mod[4:5, :], mod[5:6, :]

    cat0 = lambda parts: jnp.concatenate(parts, axis=0)
    use_halo = seq > t
    if use_halo:
        ona = cat0([ona_ref[...], onap_ref[...], onan_ref[...]])
        odn = cat0([odn_ref[...], odnp_ref[...], odnn_ref[...]])
        x = cat0([x_ref[...], xp_ref[...], xn_ref[...]])
    else:
        ona, odn, x = ona_ref[...], odn_ref[...], x_ref[...]
    o = _dot(jnp.concatenate([ona, odn], axis=1), jnp.concatenate([wna_ref[...], wdn_ref[...]], axis=0))
    x1 = x + gt_m * _rms(o, gpost_ref[...])
    h2 = _rms(x1, gpre_ref[...]) * (1.0 + sc_f) + sh_f
    o_ref[...] = x1[0:t, :]

    nlb = d // LANES
    for j in range(nlb):
        for k in range(8):
            hperm_ref[j, pl.ds(k, groups, stride=8), :] = h2[k * groups:(k + 1) * groups, j * LANES:(j + 1) * LANES]
    h = jnp.concatenate([hperm_ref[j] for j in range(nlb)], axis=1)
    if use_halo:
        h = cat0([h, h2[t:, :]])
    h = h.astype(jnp.bfloat16)

    kk = lax.broadcasted_iota(jnp.int32, (8, 1), 0)
    tok = i * t + groups * kk
    prev_zero = jnp.bitwise_and(tok, seq - 1) == 0
    next_zero = jnp.bitwise_and(tok + groups, seq - 1) == 0

    def conv(c0):
        w = wup_ref[:, c0:c0 + chunk]
        u_all = _dot(h, w)
        u = u_all[0:t, :]
        b_prev = pltpu.roll(u[t - 8:t, :], 1, 0)
        b_next = pltpu.roll(u[0:8, :], 7, 0)
        if use_halo:
            b_prev = jnp.where(kk == 0, u_all[t + halo - 1:t + halo, :], b_prev)
            b_next = jnp.where(kk == 7, u_all[t + halo:t + halo + 1, :], b_next)
        b_prev = jnp.where(prev_zero, 0.0, b_prev)
        b_next = jnp.where(next_zero, 0.0, b_next)
        prev = jnp.concatenate([b_prev, u[0:t - 8, :]], axis=0)
        nxt = jnp.concatenate([u[8:t, :], b_next], axis=0)
        cw = cw_ref[:, c0:c0 + chunk]
        return prev * cw[0:1, :] + u * cw[1:2, :] + nxt * cw[2:3, :] + cb_ref[:, c0:c0 + chunk]

    acts = []
    for c in range(dff // chunk):
        gate = conv(c * chunk)
        val = conv(dff + c * chunk)
        acts.append((gate / (1.0 + jnp.exp(-gate)) * val).astype(jnp.bfloat16))
    acc = _dot(jnp.concatenate(acts, axis=1), wd_ref[...])

    y = gt_f * _rms(acc, gffn_ref[...])
    for j in range(nlb):
        yperm_ref[j] = y[:, j * LANES:(j + 1) * LANES]
    for k in range(8):
        rows = slice(k * groups, (k + 1) * groups)
        yk = jnp.concatenate([yperm_ref[j, pl.ds(k, groups, stride=8), :] for j in range(nlb)], axis=1)
        o_ref[rows, :] = o_ref[rows, :] + yk


def _mix_ffn(ona, odn, x, mods, lw, *, seq, seq_per_mod):
    ntok, d = x.shape
    t = FFN_TILE
    halo = 16
    assert seq & (seq - 1) == 0 and (t % seq == 0 or seq % t == 0) and ntok % t == 0 and t % 64 == 0
    nhalo = ntok // halo
    if seq_per_mod is None:
        mod_spec = pl.BlockSpec((1, 6, d), lambda i: (0, 0, 0))
    else:
        tps = seq_per_mod // t
        mod_spec = pl.BlockSpec((1, 6, d), lambda i: (i // tps, 0, 0))
    tok = lambda w: pl.BlockSpec((t, w), lambda i: (i, 0))
    prv = lambda w: pl.BlockSpec((halo, w), lambda i: (jnp.maximum(i * (t // halo) - 1, 0), 0))
    nxt = lambda w: pl.BlockSpec((halo, w), lambda i: (jnp.minimum((i + 1) * (t // halo), nhalo - 1), 0))
    full = lambda e: _layer_spec(e, buffered=True)
    weights = [lw["w_out_na"], lw["w_out_dn"], lw["g_post_mix"], lw["g_pre_ffn"], lw["w_up"], lw["conv_w"],
               lw["conv_b"], lw["w_down"], lw["g_post_ffn"]]
    widths = (NA_WIDTH, DENSE_WIDTH, d)
    return pl.pallas_call(
        functools.partial(_mix_ffn_kernel, seq=seq, chunk=FF_CHUNK),
        grid=(ntok // t,),
        in_specs=[tok(w) for w in widths] + [prv(w) for w in widths] + [nxt(w) for w in widths] + [mod_spec]
                 + [full(a) for a in weights],
        out_specs=tok(d),
        out_shape=jax.ShapeDtypeStruct((ntok, d), jnp.float32),
        scratch_shapes=[pltpu.VMEM((d // LANES, t, LANES), jnp.float32)] * 2,
        compiler_params=_params("arbitrary"),
        name="mix_ffn",
    )(ona, odn, x, ona, odn, x, ona, odn, x, mods, *[w[0] for w in weights])


def _prep_weights(w_in, mla_w_uq, mla_w_ukv, w_out, gqa_g_q, gqa_g_k):
    nl, d, _ = w_in.shape
    bf = jnp.bfloat16
    cat = lambda parts: jnp.concatenate(parts, axis=-1)
    o_kr = 3 * NA_WIDTH + MLA_Q_LORA + MLA_KV_LORA
    o_qg = o_kr + MLA_ROPE
    o_kg = o_qg + GQA_WIDTH
    o_vg = o_kg + GQA_KV_HEADS * HEAD_DIM
    pad_r = LANES - MLA_NOPE - MLA_ROPE
    z = lambda rows, n: jnp.zeros((nl, rows, n), bf)
    tail = w_in[..., o_kr:].astype(bf)
    kr = tail[..., :MLA_ROPE]
    qg = cat([tail[..., o_qg - o_kr + HEAD_DIM * h:o_qg - o_kr + HEAD_DIM * (h + 1)] for h in GQA_ORDER])
    kg = tail[..., o_kg - o_kr:o_vg - o_kr]
    w_all = cat([w_in[..., :o_kr].astype(bf), qg, kg, tail[..., o_vg - o_kr:], z(d, MLA_NOPE), kr, z(d, pad_r)])
    assert w_all.shape[-1] == C_END

    uq = mla_w_uq.astype(bf)
    ukv = mla_w_ukv.astype(bf)
    ql, kl = uq.shape[1], ukv.shape[1]
    qw = MLA_NOPE + MLA_ROPE
    kvw = MLA_NOPE + MLA_V
    w_uq = cat([p for h in range(MLA_HEADS) for p in (uq[..., qw * h:qw * (h + 1)], z(ql, pad_r))])
    w_uk = cat([p for h in range(MLA_HEADS) for p in (ukv[..., kvw * h:kvw * h + MLA_NOPE], z(kl, LANES - MLA_NOPE))])
    w_uv = cat([ukv[..., kvw * h + MLA_NOPE:kvw * (h + 1)] for h in range(MLA_HEADS)])

    wo = w_out.astype(bf)
    o_gqa = NA_WIDTH + MLA_WIDTH
    w_out_dn = jnp.concatenate([wo[:, NA_WIDTH:o_gqa]]
                               + [wo[:, o_gqa + HEAD_DIM * h:o_gqa + HEAD_DIM * (h + 1)] for h in GQA_ORDER], axis=1)

    ggq = jnp.tile(gqa_g_q, (1, GQA_HEADS))[:, None, :]
    ggk = jnp.tile(gqa_g_k, (1, GQA_KV_HEADS))[:, None, :]
    ones = jnp.asarray(np.kron(np.eye(LANES // HEAD_DIM), np.full((HEAD_DIM, HEAD_DIM), 1.0 / HEAD_DIM)), bf)
    return dict(w_in=w_all, w_uq=w_uq, w_uk=w_uk, w_uv=w_uv, w_out_na=wo[:, :NA_WIDTH], w_out_dn=w_out_dn,
                ggq=ggq, ggk=ggk, ones=ones)


def _rope_tables(seq):
    t = jnp.arange(seq, dtype=jnp.int32)
    r, c = (t // GRID_W).astype(jnp.float32), (t % GRID_W).astype(jnp.float32)

    def tables(dim):
        quarter = dim // 4
        freqs = ROPE_THETA ** (-jnp.arange(quarter, dtype=jnp.float32) / quarter)
        ar_, ac_ = r[:, None] * freqs[None, :], c[:, None] * freqs[None, :]
        cos = jnp.concatenate([jnp.cos(ar_), jnp.cos(ar_), jnp.cos(ac_), jnp.cos(ac_)], axis=1)
        sin = jnp.concatenate([-jnp.sin(ar_), jnp.sin(ar_), -jnp.sin(ac_), jnp.sin(ac_)], axis=1)
        return cos, sin

    c64, s64 = tables(HEAD_DIM)
    c32, s32 = tables(MLA_ROPE)
    pad = LANES - MLA_NOPE - MLA_ROPE
    cm = jnp.concatenate([jnp.ones((seq, MLA_NOPE)), c32, jnp.zeros((seq, pad))], axis=1)
    sm = jnp.concatenate([jnp.zeros((seq, MLA_NOPE)), s32, jnp.zeros((seq, pad))], axis=1)
    return dict(cg=jnp.tile(c64, (1, 2)), sg=jnp.tile(s64, (1, 2)), cm=cm.astype(jnp.float32), sm=sm.astype(jnp.float32))


def kernel(x_prompt, x_sample, c, cache_na_k, cache_na_v, cache_mla_ckv, cache_mla_krope, cache_gqa_k, cache_gqa_v, c_ctx, w_ada, b_ada, g_pre_mix, g_post_mix, g_pre_ffn, g_post_ffn, w_in, na_rpb, mla_g_q, mla_w_uq, mla_g_kv, mla_w_ukv, gqa_g_q, gqa_g_k, w_out, ffn_w_up, ffn_conv_w, ffn_conv_b, ffn_w_down):
    batch, seq, d = x_prompt.shape
    dbatch, dseq, _ = x_sample.shape
    nl = w_ada.shape[0]
    past = cache_na_k.shape[2]
    bf = jnp.bfloat16
    assert dbatch + 1 <= 8 and dseq % GRID_W == 0

    cond = jnp.zeros((8, d), jnp.float32).at[0].set(c_ctx).at[1:1 + dbatch].set(c)
    mods_all = _ada_mods(cond, w_ada, b_ada).reshape(nl, 8, 6, d)

    pw = _prep_weights(w_in, mla_w_uq, mla_w_ukv, w_out, gqa_g_q, gqa_g_k)
    rope = _rope_tables(dseq)
    vec = lambda g: g[:, None, :]
    stacked = dict(
        g_pre_mix=vec(g_pre_mix), g_post_mix=vec(g_post_mix), g_pre_ffn=vec(g_pre_ffn), g_post_ffn=vec(g_post_ffn),
        mla_g_q=vec(mla_g_q), mla_g_kv=vec(mla_g_kv), w_in=pw["w_in"],
        w_uq=pw["w_uq"], w_uk=pw["w_uk"], w_uv=pw["w_uv"], ggq=pw["ggq"], ggk=pw["ggk"],
        w_out_na=pw["w_out_na"], w_out_dn=pw["w_out_dn"],
        w_up=ffn_w_up.astype(bf), w_down=ffn_w_down.astype(bf), conv_w=ffn_conv_w, conv_b=vec(ffn_conv_b))

    kc_na = cache_na_k.reshape(dbatch, nl, past, NA_WIDTH).astype(bf)
    vc_na = cache_na_v.reshape(dbatch, nl, past, NA_WIDTH).astype(bf)
    kc_g = cache_gqa_k.reshape(dbatch, nl, past, GQA_KV_HEADS * HEAD_DIM).astype(bf)
    vc_g = cache_gqa_v.reshape(dbatch, nl, past, GQA_KV_HEADS * HEAD_DIM).astype(bf)
    kr_pad = jnp.pad(cache_mla_krope, ((0, 0), (0, 0), (0, 0), (MLA_NOPE, LANES - MLA_NOPE - MLA_ROPE))).astype(bf)
    kc_m, vc_m = _mla_ctx_expand(cache_mla_ckv.astype(bf), kr_pad, pw["w_uk"], pw["w_uv"])

    na_pairs = _na_pair_tables(na_rpb)

    xp = x_prompt.reshape(batch * seq, d)
    xs = x_sample.reshape(dbatch * dseq, d)
    cache_widths = (NA_WIDTH, NA_WIDTH, MLA_KV_LORA, MLA_ROPE, GQA_KV_HEADS * HEAD_DIM, GQA_KV_HEADS * HEAD_DIM)
    caches = [jnp.zeros((batch, nl, seq, w), jnp.float32) for w in cache_widths]
    for l in range(nl):
        lw = {k: (v, l) for k, v in stacked.items()}
        lw["ones"] = (pw["ones"], None)
        mods_ctx = mods_all[l, 0:1]
        mods_lat = mods_all[l, 1:1 + dbatch]

        qna, kna, vna, qm, ckv, kr, km, vm, qg, kg, vg = _proj(xp, mods_ctx, lw, None, latent=False, seq=seq,
                                                               caches=caches, layer=l)
        caches = [kna, vna, ckv, kr, kg, vg]
        ona, odn = _ctx_attn(qna, kna, vna, qm, km, vm, qg, kg, vg, l, batch=batch, seq=seq)
        xp = _mix_ffn(ona, odn, xp, mods_ctx, lw, seq=seq, seq_per_mod=None)

        qna, kna, vna, qm, km, vm, qg, kg, vg = _proj(xs, mods_lat, lw, rope, latent=True, seq=dseq)
        ona = _na_attn(qna, kna, vna, kc_na, vc_na, na_pairs, l, batch=dbatch, seq=dseq)
        odn = _dense_attn(qm, qg, kc_m, vc_m, kc_g, vc_g, km, vm, kg, vg, l, batch=dbatch, seq=dseq)
        xs = _mix_ffn(ona, odn, xs, mods_lat, lw, seq=dseq, seq_per_mod=dseq)

    heads = lambda a, h: a.reshape(a.shape[:3] + (h, HEAD_DIM))
    return (xp.reshape(batch, seq, d), xs.reshape(dbatch, dseq, d),
            heads(caches[0], NA_HEADS), heads(caches[1], NA_HEADS), caches[2], caches[3],
            heads(caches[4], GQA_KV_HEADS), heads(caches[5], GQA_KV_HEADS))
```

```python
import functools

import numpy as np
import jax
import jax.numpy as jnp
from jax import lax
from jax.experimental import pallas as pl
from jax.experimental.pallas import tpu as pltpu

GRID_W = 64
HEAD_DIM = 64
NA_HEADS = 6
NA_KH = 8
NA_KW = 16
MLA_HEADS = 4
MLA_Q_LORA = 256
MLA_KV_LORA = 128
MLA_NOPE = 64
MLA_ROPE = 32
MLA_V = 64
GQA_HEADS = 6
GQA_KV_HEADS = 2
GQA_GROUP = GQA_HEADS // GQA_KV_HEADS
ROPE_THETA = 10000.0
EPS = 1e-6

NA_WIDTH = NA_HEADS * HEAD_DIM
MLA_WIDTH = MLA_HEADS * MLA_V
GQA_WIDTH = GQA_HEADS * HEAD_DIM
DENSE_WIDTH = MLA_WIDTH + GQA_WIDTH

LANES = 128
VMEM_LIMIT = 52 * 1024 * 1024
MASK_VALUE = -1e30
LOG2_E = 1.4426950408889634

PROJ_TILE = 512
FFN_TILE = 512
FF_CHUNK = 256
DENSE_Q_TILE = 256
CTX_BATCHES = 4
DENSE_K_TILE = 256
NA_ROWS = 4
NA_KEY_ROWS = NA_ROWS + NA_KH
NA_PAD = NA_KEY_ROWS - NA_KH

C_QNA = 0
C_KNA = C_QNA + NA_WIDTH
C_VNA = C_KNA + NA_WIDTH
C_CQ = C_VNA + NA_WIDTH
C_CKV = C_CQ + MLA_Q_LORA
C_QG = C_CKV + MLA_KV_LORA
C_KG = C_QG + GQA_WIDTH
C_VG = C_KG + LANES
C_KR = C_VG + LANES
C_END = C_KR + LANES
PROJ_GROUP = 512

GQA_ORDER = (0, 3, 1, 4, 2, 5)


def _dot(a, b):
    return jnp.dot(a, b, preferred_element_type=jnp.float32)


def _dot_nt(a, b):
    return lax.dot_general(a, b, (((1,), (1,)), ((), ())), preferred_element_type=jnp.float32)


def _params(*sem):
    return pltpu.CompilerParams(dimension_semantics=sem, vmem_limit_bytes=VMEM_LIMIT)


def _rms(x, g):
    return x * lax.rsqrt(jnp.mean(x * x, axis=-1, keepdims=True) + EPS) * g


def _head_mean_sq(x, ones_bf16):
    xx = x * x
    hi = xx.astype(jnp.bfloat16)
    lo = (xx - hi.astype(jnp.float32)).astype(jnp.bfloat16)
    return _dot(hi, ones_bf16) + _dot(lo, ones_bf16)


def _ada_kernel(cond_ref, w_ref, b_ref, o_ref):
    cnd = cond_ref[...]
    s = cnd / (1.0 + jnp.exp(-cnd))
    o_ref[0] = jnp.dot(s, w_ref[0], preferred_element_type=jnp.float32,
                       precision=lax.Precision.HIGHEST) + b_ref[0]


def _ada_mods(cond, w_ada, b_ada):
    nl, d, n6 = w_ada.shape
    tn = 1536
    return pl.pallas_call(
        _ada_kernel,
        grid=(nl, n6 // tn),
        in_specs=[pl.BlockSpec((8, d), lambda l, j: (0, 0)),
                  pl.BlockSpec((1, d, tn), lambda l, j: (l, 0, j)),
                  pl.BlockSpec((1, 1, tn), lambda l, j: (l, 0, j))],
        out_specs=pl.BlockSpec((1, 8, tn), lambda l, j: (l, 0, j)),
        out_shape=jax.ShapeDtypeStruct((nl, 8, n6), jnp.float32),
        compiler_params=_params("arbitrary", "arbitrary"),
        name="ada_mods",
    )(cond, w_ada, b_ada.reshape(nl, 1, n6))


def _swap_halves(x, half):
    lane = lax.broadcasted_iota(jnp.int32, (1, LANES), 1)
    first = jnp.bitwise_and(lane, 2 * half - 1) < half
    blocks = []
    for b in range(x.shape[1] // LANES):
        xb = x[:, b * LANES:(b + 1) * LANES]
        blocks.append(jnp.where(first, pltpu.roll(xb, LANES - half, 1), pltpu.roll(xb, half, 1)))
    return jnp.concatenate(blocks, axis=1)


def _proj_kernel(*refs, latent):
    if latent:
        (x_ref, mod_ref, gpre_ref, w_ref, wt_ref, gq_ref, wuq_ref, gkv_ref, wuk_ref, wuv_ref,
         ones_ref, ggq_ref, ggk_ref, cg_ref, sg_ref, cm_ref, sm_ref,
         qna_ref, kna_ref, vna_ref, qm_ref, km_ref, vm_ref, qg_ref, kg_ref, vg_ref) = refs
    else:
        (x_ref, mod_ref, gpre_ref, w_ref, wt_ref, gq_ref, wuq_ref, gkv_ref, wuk_ref, wuv_ref,
         ones_ref, ggq_ref, ggk_ref) = refs[:13]
        (qna_ref, kna_ref, vna_ref, qm_ref, ckv_ref, kr_ref, km_ref, vm_ref, qg_ref, kg_ref, vg_ref) = refs[19:]

    x = x_ref[...]
    mod = mod_ref[0]
    sh, sc = mod[0:1, :], mod[1:2, :]
    h = (_rms(x, gpre_ref[...]) * (1.0 + sc) + sh).astype(jnp.bfloat16)

    groups = [(c0, _dot(h, w_ref[:, c0:c0 + PROJ_GROUP])) for c0 in range(0, C_QG, PROJ_GROUP)]
    groups += [(c0, _dot(h, wt_ref[:, c0 - C_QG:min(c0 + PROJ_GROUP, C_END) - C_QG])) for c0 in range(C_QG, C_END, PROJ_GROUP)]

    def piece(c0, width):
        parts = []
        for b0 in range(c0, c0 + width, LANES):
            g0, y = groups[b0 // PROJ_GROUP]
            parts.append(y[:, b0 - g0:b0 - g0 + LANES])
        return parts[0] if len(parts) == 1 else jnp.concatenate(parts, axis=1)

    na_scale = HEAD_DIM ** -0.5
    dense_unit = LOG2_E if latent else 1.0
    qna_ref[...] = (piece(C_QNA, NA_WIDTH) * na_scale).astype(qna_ref.dtype)
    def put(ref, val):
        ref[...] = val.astype(ref.dtype).reshape(ref.shape)

    put(kna_ref, piece(C_KNA, NA_WIDTH))
    put(vna_ref, piece(C_VNA, NA_WIDTH))

    mla_scale = (MLA_NOPE + MLA_ROPE) ** -0.5
    cqn = _rms(piece(C_CQ, MLA_Q_LORA), gq_ref[...]).astype(jnp.bfloat16)
    qm = _dot(cqn, wuq_ref[...])
    kr = piece(C_KR, LANES)
    if latent:
        cm = jnp.concatenate([cm_ref[...]] * MLA_HEADS, axis=1)
        sm = jnp.concatenate([sm_ref[...]] * MLA_HEADS, axis=1)
        qm = qm * cm + _swap_halves(qm, MLA_ROPE // 4) * sm
        kr = kr * cm_ref[...] + _swap_halves(kr, MLA_ROPE // 4) * sm_ref[...]
    qm_ref[...] = (qm * (mla_scale * dense_unit)).astype(qm_ref.dtype)
    ckv = _rms(piece(C_CKV, MLA_KV_LORA), gkv_ref[...])
    ckv_b = ckv.astype(jnp.bfloat16)
    kn = _dot(ckv_b, wuk_ref[...])
    km_ref[...] = (kn + jnp.concatenate([kr] * MLA_HEADS, axis=1)).astype(km_ref.dtype)
    vm_ref[...] = _dot(ckv_b, wuv_ref[...]).astype(vm_ref.dtype)
    if not latent:
        put(ckv_ref, ckv)
        put(kr_ref, kr[:, MLA_NOPE:MLA_NOPE + MLA_ROPE])

    ones = ones_ref[...]
    qg = piece(C_QG, GQA_WIDTH)
    qq = (qg * qg).astype(jnp.bfloat16)
    ms = jnp.concatenate([_dot(qq[:, b * LANES:(b + 1) * LANES], ones) for b in range(GQA_WIDTH // LANES)], axis=1)
    qg = qg * lax.rsqrt(ms + EPS) * ggq_ref[...]
    kg = piece(C_KG, LANES)
    kg = kg * lax.rsqrt(_head_mean_sq(kg, ones) + EPS) * ggk_ref[...]
    if latent:
        cg, sg = cg_ref[...], sg_ref[...]
        cg3 = jnp.concatenate([cg] * (GQA_WIDTH // LANES), axis=1)
        sg3 = jnp.concatenate([sg] * (GQA_WIDTH // LANES), axis=1)
        qg = qg * cg3 + _swap_halves(qg, HEAD_DIM // 4) * sg3
        kg = kg * cg + _swap_halves(kg, HEAD_DIM // 4) * sg
    qg_ref[...] = (qg * (na_scale * dense_unit)).astype(qg_ref.dtype)
    put(kg_ref, kg)
    put(vg_ref, piece(C_VG, LANES))


def _layer_spec(entry, buffered=False):
    a, layer = entry
    kw = dict(pipeline_mode=pl.Buffered(1)) if buffered else {}
    if layer is None:
        return pl.BlockSpec(a.shape, lambda *_: (0,) * a.ndim, **kw)
    return pl.BlockSpec((None,) + a.shape[1:], lambda *_: (layer,) + (0,) * (a.ndim - 1), **kw)


def _proj(x, mods, lw, rope, *, latent, seq, caches=None, layer=None):
    ntok, d = x.shape
    t = PROJ_TILE
    nt = ntok // t
    tiles_per_seq = seq // t if latent else 1
    tok = lambda w: pl.BlockSpec((t, w), lambda i: (i, 0))
    full = lambda e: _layer_spec(e, buffered=True)
    if latent:
        mod_spec = pl.BlockSpec((1, 6, d), lambda i: (i // tiles_per_seq, 0, 0))
    else:
        mod_spec = pl.BlockSpec((1, 6, d), lambda i: (0, 0, 0))
    kv_dt = jnp.bfloat16 if latent else jnp.float32
    bf = jnp.bfloat16
    sds = lambda w, dt: jax.ShapeDtypeStruct((ntok, w), dt)
    common = [x, mods, lw["g_pre_mix"], lw["w_in"], lw["w_tail"], lw["mla_g_q"], lw["w_uq"], lw["mla_g_kv"],
              lw["w_uk"], lw["w_uv"], lw["ones"], lw["ggq"], lw["ggk"]]
    in_specs = [tok(d), mod_spec] + [full(a) for a in common[2:]]
    if latent:
        rope_spec = pl.BlockSpec((t, LANES), lambda i: (i % tiles_per_seq, 0))
        ins = common + [rope["cg"], rope["sg"], rope["cm"], rope["sm"]]
        in_specs = in_specs + [rope_spec] * 4
        outs = [sds(NA_WIDTH, bf), sds(NA_WIDTH, kv_dt), sds(NA_WIDTH, kv_dt), sds(MLA_HEADS * LANES, bf),
                sds(MLA_HEADS * LANES, bf), sds(MLA_WIDTH, bf), sds(GQA_WIDTH, bf), sds(LANES, kv_dt), sds(LANES, kv_dt)]
        aliases = {}
    else:
        ins = common + list(caches)
        in_specs = in_specs + [pl.BlockSpec(memory_space=pl.ANY)] * len(caches)
        cache_sds = [jax.ShapeDtypeStruct(a.shape, a.dtype) for a in caches]
        outs = [sds(NA_WIDTH, bf), cache_sds[0], cache_sds[1], sds(MLA_HEADS * LANES, bf), cache_sds[2], cache_sds[3],
                sds(MLA_HEADS * LANES, bf), sds(MLA_WIDTH, bf), sds(GQA_WIDTH, bf), cache_sds[4], cache_sds[5]]
        cache_out = (1, 2, 4, 5, 9, 10)
        aliases = {len(common) + k: cache_out[k] for k in range(len(caches))}
    spt = t // seq
    cache_spec = lambda o: pl.BlockSpec((spt, None, seq, o.shape[-1]), lambda i: (i, layer, 0, 0))
    out_specs = [cache_spec(o) if len(o.shape) == 4 else tok(o.shape[1]) for o in outs]
    return pl.pallas_call(
        functools.partial(_proj_kernel, latent=latent),
        grid=(nt,),
        in_specs=in_specs,
        out_specs=out_specs,
        out_shape=outs,
        input_output_aliases=aliases,
        compiler_params=_params("arbitrary"),
        name="proj_lat" if latent else "proj_ctx",
    )(*[a[0] if isinstance(a, tuple) else a for a in ins])


def _lane_lo():
    return lax.broadcasted_iota(jnp.int32, (1, LANES), 1) < HEAD_DIM


def _split_heads(q):
    lo = _lane_lo()
    zero = jnp.zeros_like(q)
    return jnp.where(lo, q, zero), jnp.where(lo, zero, q)


def _softmax_pv(scores, values):
    m = scores[0].max(axis=-1, keepdims=True)
    for s in scores[1:]:
        m = jnp.maximum(m, s.max(axis=-1, keepdims=True))
    l = None
    acc = None
    for s, v in zip(scores, values):
        p = jnp.exp(s - m)
        ps = p.sum(axis=-1, keepdims=True)
        pv = _dot(p.astype(jnp.bfloat16), v)
        l = ps if l is None else l + ps
        acc = pv if acc is None else acc + pv
    return acc / l


def _ctx_attn_kernel(qna_ref, kna_ref, vna_ref, qm_ref, km_ref, vm_ref, qg_ref, kg_ref, vg_ref, ona_ref, odn_ref):
    lo = _lane_lo()
    bf = jnp.bfloat16

    def pair(q_lo, q_hi, k_lo, k_hi, v):
        o_lo = _softmax_pv([_dot_nt(q_lo, k_lo)], [v])
        o_hi = _softmax_pv([_dot_nt(q_hi, k_hi)], [v])
        return jnp.where(lo, o_lo, o_hi)

    for b in range(qna_ref.shape[0]):
        for p in range(NA_WIDTH // LANES):
            cs = slice(p * LANES, (p + 1) * LANES)
            q_lo, q_hi = _split_heads(qna_ref[b, :, cs])
            k = kna_ref[b, :, cs].astype(bf)
            ona_ref[b, :, cs] = pair(q_lo, q_hi, k, k, vna_ref[b, :, cs].astype(bf)).astype(ona_ref.dtype)
        for p in range(MLA_HEADS // 2):
            c0 = 2 * p * LANES
            o = pair(qm_ref[b, :, c0:c0 + LANES], qm_ref[b, :, c0 + LANES:c0 + 2 * LANES],
                     km_ref[b, :, c0:c0 + LANES], km_ref[b, :, c0 + LANES:c0 + 2 * LANES],
                     vm_ref[b, :, p * LANES:(p + 1) * LANES])
            odn_ref[b, :, p * LANES:(p + 1) * LANES] = o.astype(odn_ref.dtype)
        kg = kg_ref[b].astype(bf)
        vg = vg_ref[b].astype(bf)
        for c in range(GQA_WIDTH // LANES):
            q_lo, q_hi = _split_heads(qg_ref[b, :, c * LANES:(c + 1) * LANES])
            o = pair(q_lo, q_hi, kg, kg, vg)
            odn_ref[b, :, MLA_WIDTH + c * LANES:MLA_WIDTH + (c + 1) * LANES] = o.astype(odn_ref.dtype)


def _ctx_attn(qna, kna, vna, qm, km, vm, qg, kg, vg, layer, *, batch, seq):
    nb = CTX_BATCHES
    ins = [a if a.ndim == 4 else a.reshape(batch, seq, a.shape[-1]) for a in (qna, kna, vna, qm, km, vm, qg, kg, vg)]
    spec = lambda a: (pl.BlockSpec((nb, None, seq, a.shape[-1]), lambda b: (b, layer, 0, 0)) if len(a.shape) == 4
                      else pl.BlockSpec((nb, seq, a.shape[-1]), lambda b: (b, 0, 0)))
    outs = [jax.ShapeDtypeStruct((batch, seq, NA_WIDTH), jnp.bfloat16),
            jax.ShapeDtypeStruct((batch, seq, DENSE_WIDTH), jnp.bfloat16)]
    ona, odn = pl.pallas_call(
        _ctx_attn_kernel,
        grid=(batch // nb,),
        in_specs=[spec(a) for a in ins],
        out_specs=[spec(o) for o in outs],
        out_shape=outs,
        compiler_params=_params("arbitrary"),
        name="ctx_attn",
    )(*ins)
    return ona.reshape(batch * seq, NA_WIDTH), odn.reshape(batch * seq, DENSE_WIDTH)


def _na_attn_kernel(q_ref, k_ref, v_ref, kc_ref, vc_ref, pair_ref, o_ref, *, rows):
    lo = _lane_lo()
    blk = pl.program_id(1)
    row0 = blk * NA_ROWS
    key_row0 = jnp.clip(row0 - NA_KH // 2, 0, rows - NA_KEY_ROWS)
    start = pl.multiple_of(key_row0 * GRID_W, GRID_W)
    tq = NA_ROWS * GRID_W
    nkeys = NA_KEY_ROWS * GRID_W

    rq = row0 + lax.shift_right_logical(lax.broadcasted_iota(jnp.int32, (tq, 1), 0), 6)
    rk = key_row0 + lax.shift_right_logical(lax.broadcasted_iota(jnp.int32, (1, nkeys), 1), 6)
    rs = jnp.clip(rq - NA_KH // 2, 0, rows - NA_KH)
    row_mask = jnp.where((rk >= rs) & (rk < rs + NA_KH), 0.0, MASK_VALUE)

    def bias(h):
        base = key_row0 - row0 + (NA_KH - 1) + NA_PAD
        blocks = [jnp.concatenate([pair_ref[0, h, base + 2 * m - dq] for m in range(NA_KEY_ROWS // 2)], axis=1)
                  for dq in range(NA_ROWS)]
        return jnp.concatenate(blocks, axis=0) + row_mask

    for p in range(NA_WIDTH // LANES):
        cs = slice(p * LANES, (p + 1) * LANES)
        q_lo, q_hi = _split_heads(q_ref[0, :, cs])
        k = k_ref[0, pl.ds(start, nkeys), cs]
        v = v_ref[0, pl.ds(start, nkeys), cs]
        kc = kc_ref[0, 0, :, cs]
        vc = vc_ref[0, 0, :, cs]
        o_lo = _softmax_pv([_dot_nt(q_lo, k) + bias(2 * p), _dot_nt(q_lo, kc)], [v, vc])
        o_hi = _softmax_pv([_dot_nt(q_hi, k) + bias(2 * p + 1), _dot_nt(q_hi, kc)], [v, vc])
        o_ref[0, :, cs] = jnp.where(lo, o_lo, o_hi).astype(o_ref.dtype)


def _na_attn(q, k, v, kc, vc, pairs, layer, *, batch, seq):
    rows = seq // GRID_W
    nblk = rows // NA_ROWS
    tq = NA_ROWS * GRID_W
    past = kc.shape[2]
    assert GRID_W == 64 and rows >= NA_KEY_ROWS and rows % NA_ROWS == 0
    q3, k3, v3 = (a.reshape(batch, seq, NA_WIDTH) for a in (q, k, v))
    o = pl.pallas_call(
        functools.partial(_na_attn_kernel, rows=rows),
        grid=(batch, nblk),
        in_specs=[pl.BlockSpec((1, tq, NA_WIDTH), lambda b, i: (b, i, 0)),
                  pl.BlockSpec((1, seq, NA_WIDTH), lambda b, i: (b, 0, 0)),
                  pl.BlockSpec((1, seq, NA_WIDTH), lambda b, i: (b, 0, 0)),
                  pl.BlockSpec((1, 1, past, NA_WIDTH), lambda b, i: (b, layer, 0, 0)),
                  pl.BlockSpec((1, 1, past, NA_WIDTH), lambda b, i: (b, layer, 0, 0)),
                  pl.BlockSpec((1,) + pairs.shape[1:], lambda b, i: (layer, 0, 0, 0, 0))],
        out_specs=pl.BlockSpec((1, tq, NA_WIDTH), lambda b, i: (b, i, 0)),
        out_shape=jax.ShapeDtypeStruct((batch, seq, NA_WIDTH), jnp.bfloat16),
        compiler_params=_params("arbitrary", "arbitrary"),
        name="na_attn",
    )(q3, k3, v3, kc, vc, pairs)
    return o.reshape(batch * seq, NA_WIDTH)


def _na_pair_tables(na_rpb):
    cq = np.arange(GRID_W)[:, None]
    ck = np.arange(GRID_W)[None, :]
    cs = np.clip(cq - NA_KW // 2, 0, GRID_W - NA_KW)
    valid = (ck >= cs) & (ck < cs + NA_KW)
    onehot = ((ck - cq + NA_KW - 1)[None] == np.arange(2 * NA_KW - 1)[:, None, None]) & valid[None]
    t = jnp.einsum("lhdo,oqk->lhdqk", na_rpb, jnp.asarray(onehot, na_rpb.dtype), precision=lax.Precision.HIGHEST)
    t = jnp.where(jnp.asarray(valid), t, MASK_VALUE)
    t = jnp.pad(t, ((0, 0), (0, 0), (NA_PAD, NA_PAD), (0, 0), (0, 0)), constant_values=MASK_VALUE)
    return jnp.concatenate([t[:, :, :-1], t[:, :, 1:]], axis=-1)


def _dense_attn_kernel(qm_ref, qg_ref, kmc_ref, vmc_ref, kgc_ref, vgc_ref, kml_ref, vml_ref, kgl_ref, vgl_ref,
                       o_ref, s_lo_ref, s_hi_ref, *, nk, tk):
    lo = _lane_lo()
    bf = jnp.bfloat16
    past = kmc_ref.shape[2]

    def lane_fold(x, op):
        parts = [x[:, c * LANES:(c + 1) * LANES] for c in range(x.shape[1] // LANES)]
        while len(parts) > 1:
            parts = [op(parts[i], parts[i + 1]) if i + 1 < len(parts) else parts[i] for i in range(0, len(parts), 2)]
        return parts[0]

    def head(q, kc, vc, k_at, v_at, s_ref):
        spans = [(0, past)] + [(past + j * tk, tk) for j in range(nk)]
        mx = None
        for idx, (c0, width) in enumerate(spans):
            s = _dot_nt(q, kc if idx == 0 else k_at(idx - 1))
            s_ref[:, c0:c0 + width] = s
            part = lane_fold(s, jnp.maximum)
            mx = part if mx is None else jnp.maximum(mx, part)
        m = mx.max(axis=-1, keepdims=True)
        lsum = None
        acc = None
        for idx, (c0, width) in enumerate(spans):
            p = jnp.exp2(s_ref[:, c0:c0 + width] - m)
            part = lane_fold(p, jnp.add)
            pv = _dot(p.astype(bf), vc if idx == 0 else v_at(idx - 1))
            lsum = part if lsum is None else lsum + part
            acc = pv if acc is None else acc + pv
        return acc / lsum.sum(axis=-1, keepdims=True)

    def unit(q_lo, q_hi, kc_lo, kc_hi, vc, k_lo_at, k_hi_at, v_at):
        o_lo = head(q_lo, kc_lo, vc, k_lo_at, v_at, s_lo_ref)
        o_hi = head(q_hi, kc_hi, vc, k_hi_at, v_at, s_hi_ref)
        return jnp.where(lo, o_lo, o_hi)

    for p in range(MLA_HEADS // 2):
        c0 = 2 * p * LANES
        c1 = c0 + LANES
        o = unit(qm_ref[0, :, c0:c1], qm_ref[0, :, c1:c1 + LANES],
                 kmc_ref[0, 0, :, c0:c1], kmc_ref[0, 0, :, c1:c1 + LANES], vmc_ref[0, 0, :, p * LANES:(p + 1) * LANES],
                 lambda j, c0=c0, c1=c1: kml_ref[0, j * tk:(j + 1) * tk, c0:c1],
                 lambda j, c1=c1: kml_ref[0, j * tk:(j + 1) * tk, c1:c1 + LANES],
                 lambda j, p=p: vml_ref[0, j * tk:(j + 1) * tk, p * LANES:(p + 1) * LANES])
        o_ref[0, :, p * LANES:(p + 1) * LANES] = o.astype(o_ref.dtype)
    kgc = kgc_ref[0, 0]
    vgc = vgc_ref[0, 0]
    kg_at = lambda j: kgl_ref[0, j * tk:(j + 1) * tk, :]
    vg_at = lambda j: vgl_ref[0, j * tk:(j + 1) * tk, :]
    for c in range(GQA_WIDTH // LANES):
        q_lo, q_hi = _split_heads(qg_ref[0, :, c * LANES:(c + 1) * LANES])
        o = unit(q_lo, q_hi, kgc, kgc, vgc, kg_at, kg_at, vg_at)
        o_ref[0, :, MLA_WIDTH + c * LANES:MLA_WIDTH + (c + 1) * LANES] = o.astype(o_ref.dtype)


def _dense_attn(qm, qg, kmc, vmc, kgc, vgc, kml, vml, kgl, vgl, layer, *, batch, seq):
    tq, tk = DENSE_Q_TILE, DENSE_K_TILE
    r3 = lambda a: a.reshape(batch, seq, a.shape[-1])
    qm, qg, kml, vml, kgl, vgl = (r3(a) for a in (qm, qg, kml, vml, kgl, vgl))
    qspec = lambda a: pl.BlockSpec((1, tq, a.shape[-1]), lambda b, i: (b, i, 0))
    cspec = lambda a: pl.BlockSpec((1, 1) + a.shape[2:], lambda b, i: (b, layer, 0, 0))
    lspec = lambda a: pl.BlockSpec((1, seq, a.shape[-1]), lambda b, i: (b, 0, 0), pipeline_mode=pl.Buffered(1))
    o = pl.pallas_call(
        functools.partial(_dense_attn_kernel, nk=seq // tk, tk=tk),
        grid=(batch, seq // tq),
        in_specs=[qspec(qm), qspec(qg), cspec(kmc), cspec(vmc), cspec(kgc), cspec(vgc),
                  lspec(kml), lspec(vml), lspec(kgl), lspec(vgl)],
        out_specs=pl.BlockSpec((1, tq, DENSE_WIDTH), lambda b, i: (b, i, 0)),
        out_shape=jax.ShapeDtypeStruct((batch, seq, DENSE_WIDTH), jnp.bfloat16),
        scratch_shapes=[pltpu.VMEM((tq, kmc.shape[2] + seq), jnp.float32)] * 2,
        compiler_params=_params("arbitrary", "arbitrary"),
        name="dense_attn",
    )(qm, qg, kmc, vmc, kgc, vgc, kml, vml, kgl, vgl)
    return o.reshape(batch * seq, DENSE_WIDTH)


def _mla_ctx_kernel(ckv_ref, kr_ref, wuk_ref, wuv_ref, k_ref, v_ref):
    ckv = ckv_ref[0, 0]
    kn = _dot(ckv, wuk_ref[0])
    k_ref[0, 0] = (kn + jnp.concatenate([kr_ref[0, 0].astype(jnp.float32)] * MLA_HEADS, axis=1)).astype(k_ref.dtype)
    v_ref[0, 0] = _dot(ckv, wuv_ref[0]).astype(v_ref.dtype)


def _mla_ctx_expand(ckv, kr_pad, wuk, wuv):
    b, nl, past, _ = ckv.shape
    spec = lambda w: pl.BlockSpec((1, 1, past, w), lambda i, l: (i, l, 0, 0))
    wspec = lambda a: pl.BlockSpec((1,) + a.shape[1:], lambda i, l: (l, 0, 0))
    return pl.pallas_call(
        _mla_ctx_kernel,
        grid=(b, nl),
        in_specs=[spec(MLA_KV_LORA), spec(LANES), wspec(wuk), wspec(wuv)],
        out_specs=[spec(MLA_HEADS * LANES), spec(MLA_WIDTH)],
        out_shape=[jax.ShapeDtypeStruct((b, nl, past, MLA_HEADS * LANES), jnp.bfloat16),
                   jax.ShapeDtypeStruct((b, nl, past, MLA_WIDTH), jnp.bfloat16)],
        compiler_params=_params("arbitrary", "arbitrary"),
        name="mla_ctx_expand",
    )(ckv, kr_pad, wuk, wuv)


def _mix_ffn_kernel(ona_ref, odn_ref, x_ref, onap_ref, odnp_ref, xp_ref, onan_ref, odnn_ref, xn_ref, mod_ref,
                    wna_ref, wdn_ref, gpost_ref, gpre_ref, wup_ref, cw_ref, cb_ref, wd_ref, gffn_ref,
                    o_ref, hperm_ref, yperm_ref, *, seq, chunk):
    i = pl.program_id(0)
    t, d = x_ref.shape
    halo = xp_ref.shape[0]
    dff = wd_ref.shape[0]
    groups = t // 8
    mod = mod_ref[0]
    gt_m, sh_f, sc_f, gt_f = mod[2:3, :], mod[3:4, :], mod[4:5, :], mod[5:6, :]

    cat0 = lambda parts: jnp.concatenate(parts, axis=0)
    use_halo = seq > t
    if use_halo:
        ona = cat0([ona_ref[...], onap_ref[...], onan_ref[...]])
        odn = cat0([odn_ref[...], odnp_ref[...], odnn_ref[...]])
        x = cat0([x_ref[...], xp_ref[...], xn_ref[...]])
    else:
        ona, odn, x = ona_ref[...], odn_ref[...], x_ref[...]
    o = _dot(jnp.concatenate([ona, odn], axis=1), jnp.concatenate([wna_ref[...], wdn_ref[...]], axis=0))
    x1 = x + gt_m * _rms(o, gpost_ref[...])
    h2 = _rms(x1, gpre_ref[...]) * (1.0 + sc_f) + sh_f
    o_ref[...] = x1[0:t, :]

    nlb = d // LANES
    for j in range(nlb):
        for k in range(8):
            hperm_ref[j, pl.ds(k, groups, stride=8), :] = h2[k * groups:(k + 1) * groups, j * LANES:(j + 1) * LANES]
    h = jnp.concatenate([hperm_ref[j] for j in range(nlb)], axis=1)
    if use_halo:
        h = cat0([h, h2[t:, :]])
    h = h.astype(jnp.bfloat16)

    kk = lax.broadcasted_iota(jnp.int32, (8, 1), 0)
    tok = i * t + groups * kk
    prev_zero = jnp.bitwise_and(tok, seq - 1) == 0
    next_zero = jnp.bitwise_and(tok + groups, seq - 1) == 0

    def conv(c0):
        w = wup_ref[:, c0:c0 + chunk]
        u_all = _dot(h, w)
        u = u_all[0:t, :]
        b_prev = pltpu.roll(u[t - 8:t, :], 1, 0)
        b_next = pltpu.roll(u[0:8, :], 7, 0)
        if use_halo:
            b_prev = jnp.where(kk == 0, u_all[t + halo - 1:t + halo, :], b_prev)
            b_next = jnp.where(kk == 7, u_all[t + halo:t + halo + 1, :], b_next)
        b_prev = jnp.where(prev_zero, 0.0, b_prev)
        b_next = jnp.where(next_zero, 0.0, b_next)
        prev = jnp.concatenate([b_prev, u[0:t - 8, :]], axis=0)
        nxt = jnp.concatenate([u[8:t, :], b_next], axis=0)
        cw = cw_ref[:, c0:c0 + chunk]
        return prev * cw[0:1, :] + u * cw[1:2, :] + nxt * cw[2:3, :] + cb_ref[:, c0:c0 + chunk]

    acts = []
    for c in range(dff // chunk):
        gate = conv(c * chunk)
        val = conv(dff + c * chunk)
        acts.append((gate / (1.0 + jnp.exp(-gate)) * val).astype(jnp.bfloat16))
    acc = _dot(jnp.concatenate(acts, axis=1), wd_ref[...])

    y = gt_f * _rms(acc, gffn_ref[...])
    for j in range(nlb):
        yperm_ref[j] = y[:, j * LANES:(j + 1) * LANES]
    for k in range(8):
        rows = slice(k * groups, (k + 1) * groups)
        yk = jnp.concatenate([yperm_ref[j, pl.ds(k, groups, stride=8), :] for j in range(nlb)], axis=1)
        o_ref[rows, :] = o_ref[rows, :] + yk


def _mix_ffn(ona, odn, x, mods, lw, *, seq, seq_per_mod):
    ntok, d = x.shape
    t = FFN_TILE
    halo = 16
    assert seq & (seq - 1) == 0 and (t % seq == 0 or seq % t == 0) and ntok % t == 0 and t % 64 == 0
    nhalo = ntok // halo
    if seq_per_mod is None:
        mod_spec = pl.BlockSpec((1, 6, d), lambda i: (0, 0, 0))
    else:
        tps = seq_per_mod // t
        mod_spec = pl.BlockSpec((1, 6, d), lambda i: (i // tps, 0, 0))
    tok = lambda w: pl.BlockSpec((t, w), lambda i: (i, 0))
    prv = lambda w: pl.BlockSpec((halo, w), lambda i: (jnp.maximum(i * (t // halo) - 1, 0), 0))
    nxt = lambda w: pl.BlockSpec((halo, w), lambda i: (jnp.minimum((i + 1) * (t // halo), nhalo - 1), 0))
    full = lambda e: _layer_spec(e, buffered=True)
    weights = [lw["w_out_na"], lw["w_out_dn"], lw["g_post_mix"], lw["g_pre_ffn"], lw["w_up"], lw["conv_w"],
               lw["conv_b"], lw["w_down"], lw["g_post_ffn"]]
    widths = (NA_WIDTH, DENSE_WIDTH, d)
    return pl.pallas_call(
        functools.partial(_mix_ffn_kernel, seq=seq, chunk=FF_CHUNK),
        grid=(ntok // t,),
        in_specs=[tok(w) for w in widths] + [prv(w) for w in widths] + [nxt(w) for w in widths] + [mod_spec]
                 + [full(a) for a in weights],
        out_specs=tok(d),
        out_shape=jax.ShapeDtypeStruct((ntok, d), jnp.float32),
        scratch_shapes=[pltpu.VMEM((d // LANES, t, LANES), jnp.float32)] * 2,
        compiler_params=_params("arbitrary"),
        name="mix_ffn",
    )(ona, odn, x, ona, odn, x, ona, odn, x, mods, *[w[0] for w in weights])


def _prep_weights(w_in, mla_w_uq, mla_w_ukv, w_out, gqa_g_q, gqa_g_k):
    nl, d, _ = w_in.shape
    bf = jnp.bfloat16
    cat = lambda parts: jnp.concatenate(parts, axis=-1)
    o_kr = 3 * NA_WIDTH + MLA_Q_LORA + MLA_KV_LORA
    o_qg = o_kr + MLA_ROPE
    o_kg = o_qg + GQA_WIDTH
    o_vg = o_kg + GQA_KV_HEADS * HEAD_DIM
    pad_r = LANES - MLA_NOPE - MLA_ROPE
    z = lambda rows, n: jnp.zeros((nl, rows, n), bf)
    tail = w_in[..., o_kr:].astype(bf)
    kr = tail[..., :MLA_ROPE]
    qg = cat([tail[..., o_qg - o_kr + HEAD_DIM * h:o_qg - o_kr + HEAD_DIM * (h + 1)] for h in GQA_ORDER])
    kg = tail[..., o_kg - o_kr:o_vg - o_kr]
    assert o_kr == C_QG and C_QG % PROJ_GROUP == 0
    w_head = w_in[..., :o_kr].astype(bf)
    w_tail = cat([qg, kg, tail[..., o_vg - o_kr:], z(d, MLA_NOPE), kr, z(d, pad_r)])
    assert w_tail.shape[-1] == C_END - C_QG

    uq = mla_w_uq.astype(bf)
    ukv = mla_w_ukv.astype(bf)
    ql, kl = uq.shape[1], ukv.shape[1]
    qw = MLA_NOPE + MLA_ROPE
    kvw = MLA_NOPE + MLA_V
    w_uq = cat([p for h in range(MLA_HEADS) for p in (uq[..., qw * h:qw * (h + 1)], z(ql, pad_r))])
    w_uk = cat([p for h in range(MLA_HEADS) for p in (ukv[..., kvw * h:kvw * h + MLA_NOPE], z(kl, LANES - MLA_NOPE))])
    w_uv = cat([ukv[..., kvw * h + MLA_NOPE:kvw * (h + 1)] for h in range(MLA_HEADS)])

    wo = w_out.astype(bf)
    o_gqa = NA_WIDTH + MLA_WIDTH
    w_out_dn = jnp.concatenate([wo[:, NA_WIDTH:o_gqa]]
                               + [wo[:, o_gqa + HEAD_DIM * h:o_gqa + HEAD_DIM * (h + 1)] for h in GQA_ORDER], axis=1)

    ggq = jnp.tile(gqa_g_q, (1, GQA_HEADS))[:, None, :]
    ggk = jnp.tile(gqa_g_k, (1, GQA_KV_HEADS))[:, None, :]
    ones = jnp.asarray(np.kron(np.eye(LANES // HEAD_DIM), np.full((HEAD_DIM, HEAD_DIM), 1.0 / HEAD_DIM)), bf)
    return dict(w_in=w_head, w_tail=w_tail, w_uq=w_uq, w_uk=w_uk, w_uv=w_uv, w_out_na=wo[:, :NA_WIDTH], w_out_dn=w_out_dn,
                ggq=ggq, ggk=ggk, ones=ones)


def _rope_tables(seq):
    f32 = np.float32
    t = np.arange(seq)
    r, c = (t // GRID_W).astype(f32), (t % GRID_W).astype(f32)

    def tables(dim):
        quarter = dim // 4
        freqs = f32(ROPE_THETA) ** (-np.arange(quarter, dtype=f32) / f32(quarter))
        ar_, ac_ = r[:, None] * freqs[None, :], c[:, None] * freqs[None, :]
        cos = np.concatenate([np.cos(ar_), np.cos(ar_), np.cos(ac_), np.cos(ac_)], axis=1)
        sin = np.concatenate([-np.sin(ar_), np.sin(ar_), -np.sin(ac_), np.sin(ac_)], axis=1)
        return cos.astype(f32), sin.astype(f32)

    c64, s64 = tables(HEAD_DIM)
    c32, s32 = tables(MLA_ROPE)
    pad = LANES - MLA_NOPE - MLA_ROPE
    cm = np.concatenate([np.ones((seq, MLA_NOPE), f32), c32, np.zeros((seq, pad), f32)], axis=1)
    sm = np.concatenate([np.zeros((seq, MLA_NOPE), f32), s32, np.zeros((seq, pad), f32)], axis=1)
    return dict(cg=jnp.asarray(np.tile(c64, (1, 2))), sg=jnp.asarray(np.tile(s64, (1, 2))),
                cm=jnp.asarray(cm), sm=jnp.asarray(sm))


def kernel(x_prompt, x_sample, c, cache_na_k, cache_na_v, cache_mla_ckv, cache_mla_krope, cache_gqa_k, cache_gqa_v, c_ctx, w_ada, b_ada, g_pre_mix, g_post_mix, g_pre_ffn, g_post_ffn, w_in, na_rpb, mla_g_q, mla_w_uq, mla_g_kv, mla_w_ukv, gqa_g_q, gqa_g_k, w_out, ffn_w_up, ffn_conv_w, ffn_conv_b, ffn_w_down):
    batch, seq, d = x_prompt.shape
    dbatch, dseq, _ = x_sample.shape
    nl = w_ada.shape[0]
    past = cache_na_k.shape[2]
    bf = jnp.bfloat16
    assert dbatch + 1 <= 8 and dseq % GRID_W == 0

    cond = jnp.zeros((8, d), jnp.float32).at[0].set(c_ctx).at[1:1 + dbatch].set(c)
    mods_all = _ada_mods(cond, w_ada, b_ada).reshape(nl, 8, 6, d)

    pw = _prep_weights(w_in, mla_w_uq, mla_w_ukv, w_out, gqa_g_q, gqa_g_k)
    rope = _rope_tables(dseq)
    vec = lambda g: g[:, None, :]
    stacked = dict(
        g_pre_mix=vec(g_pre_mix), g_post_mix=vec(g_post_mix), g_pre_ffn=vec(g_pre_ffn), g_post_ffn=vec(g_post_ffn),
        mla_g_q=vec(mla_g_q), mla_g_kv=vec(mla_g_kv), w_in=pw["w_in"], w_tail=pw["w_tail"],
        w_uq=pw["w_uq"], w_uk=pw["w_uk"], w_uv=pw["w_uv"], ggq=pw["ggq"], ggk=pw["ggk"],
        w_out_na=pw["w_out_na"], w_out_dn=pw["w_out_dn"],
        w_up=ffn_w_up.astype(bf), w_down=ffn_w_down.astype(bf), conv_w=ffn_conv_w, conv_b=vec(ffn_conv_b))

    kc_na = cache_na_k.reshape(dbatch, nl, past, NA_WIDTH).astype(bf)
    vc_na = cache_na_v.reshape(dbatch, nl, past, NA_WIDTH).astype(bf)
    kc_g = cache_gqa_k.reshape(dbatch, nl, past, GQA_KV_HEADS * HEAD_DIM).astype(bf)
    vc_g = cache_gqa_v.reshape(dbatch, nl, past, GQA_KV_HEADS * HEAD_DIM).astype(bf)
    kr_pad = jnp.pad(cache_mla_krope, ((0, 0), (0, 0), (0, 0), (MLA_NOPE, LANES - MLA_NOPE - MLA_ROPE))).astype(bf)
    kc_m, vc_m = _mla_ctx_expand(cache_mla_ckv.astype(bf), kr_pad, pw["w_uk"], pw["w_uv"])

    na_pairs = _na_pair_tables(na_rpb)

    xp = x_prompt.reshape(batch * seq, d)
    xs = x_sample.reshape(dbatch * dseq, d)
    cache_widths = (NA_WIDTH, NA_WIDTH, MLA_KV_LORA, MLA_ROPE, GQA_KV_HEADS * HEAD_DIM, GQA_KV_HEADS * HEAD_DIM)
    caches = [jnp.zeros((batch, nl, seq, w), jnp.float32) for w in cache_widths]
    for l in range(nl):
        lw = {k: (v, l) for k, v in stacked.items()}
        lw["ones"] = (pw["ones"], None)
        mods_ctx = mods_all[l, 0:1]
        mods_lat = mods_all[l, 1:1 + dbatch]

        qna, kna, vna, qm, ckv, kr, km, vm, qg, kg, vg = _proj(xp, mods_ctx, lw, None, latent=False, seq=seq,
                                                               caches=caches, layer=l)
        caches = [kna, vna, ckv, kr, kg, vg]
        ona, odn = _ctx_attn(qna, kna, vna, qm, km, vm, qg, kg, vg, l, batch=batch, seq=seq)
        xp = _mix_ffn(ona, odn, xp, mods_ctx, lw, seq=seq, seq_per_mod=None)

        qna, kna, vna, qm, km, vm, qg, kg, vg = _proj(xs, mods_lat, lw, rope, latent=True, seq=dseq)
        ona = _na_attn(qna, kna, vna, kc_na, vc_na, na_pairs, l, batch=dbatch, seq=dseq)
        odn = _dense_attn(qm, qg, kc_m, vc_m, kc_g, vc_g, km, vm, kg, vg, l, batch=dbatch, seq=dseq)
        xs = _mix_ffn(ona, odn, xs, mods_lat, lw, seq=dseq, seq_per_mod=dseq)

    heads = lambda a, h: a.reshape(a.shape[:3] + (h, HEAD_DIM))
    return (xp.reshape(batch, seq, d), xs.reshape(dbatch, dseq, d),
            heads(caches[0], NA_HEADS), heads(caches[1], NA_HEADS), caches[2], caches[3],
            heads(caches[4], GQA_KV_HEADS), heads(caches[5], GQA_KV_HEADS))
```

```python
import functools

import numpy as np
import jax
import jax.numpy as jnp
from jax import lax
from jax.experimental import pallas as pl
from jax.experimental.pallas import tpu as pltpu

GRID_W = 64
HEAD_DIM = 64
NA_HEADS = 6
NA_KH = 8
NA_KW = 16
MLA_HEADS = 4
MLA_Q_LORA = 256
MLA_KV_LORA = 128
MLA_NOPE = 64
MLA_ROPE = 32
MLA_V = 64
GQA_HEADS = 6
GQA_KV_HEADS = 2
GQA_GROUP = GQA_HEADS // GQA_KV_HEADS
ROPE_THETA = 10000.0
EPS = 1e-6

NA_WIDTH = NA_HEADS * HEAD_DIM
MLA_WIDTH = MLA_HEADS * MLA_V
GQA_WIDTH = GQA_HEADS * HEAD_DIM
DENSE_WIDTH = MLA_WIDTH + GQA_WIDTH

LANES = 128
VMEM_LIMIT = 52 * 1024 * 1024
MASK_VALUE = -1e30
LOG2_E = 1.4426950408889634

PROJ_TILE = 512
FFN_TILE = 512
FF_CHUNK = 256
DENSE_Q_TILE = 256
CTX_BATCHES = 4
DENSE_K_TILE = 256
NA_ROWS = 4
NA_KEY_ROWS = NA_ROWS + NA_KH
NA_PAD = NA_KEY_ROWS - NA_KH

C_QNA = 0
C_KNA = C_QNA + NA_WIDTH
C_VNA = C_KNA + NA_WIDTH
C_CQ = C_VNA + NA_WIDTH
C_CKV = C_CQ + MLA_Q_LORA
C_QG = C_CKV + MLA_KV_LORA
C_KG = C_QG + GQA_WIDTH
C_VG = C_KG + LANES
C_KR = C_VG + LANES
C_END = C_KR + LANES
PROJ_GROUP = 512

GQA_ORDER = (0, 3, 1, 4, 2, 5)


def _dot(a, b):
    return jnp.dot(a, b, preferred_element_type=jnp.float32)


def _dot_nt(a, b):
    return lax.dot_general(a, b, (((1,), (1,)), ((), ())), preferred_element_type=jnp.float32)


def _params(*sem):
    return pltpu.CompilerParams(dimension_semantics=sem, vmem_limit_bytes=VMEM_LIMIT)


def _rms(x, g):
    return x * lax.rsqrt(jnp.mean(x * x, axis=-1, keepdims=True) + EPS) * g


def _head_mean_sq(x, ones_bf16):
    xx = x * x
    hi = xx.astype(jnp.bfloat16)
    lo = (xx - hi.astype(jnp.float32)).astype(jnp.bfloat16)
    return _dot(hi, ones_bf16) + _dot(lo, ones_bf16)


def _ada_kernel(cond_ref, w_ref, b_ref, o_ref):
    cnd = cond_ref[...]
    s = cnd / (1.0 + jnp.exp(-cnd))
    o_ref[0] = jnp.dot(s, w_ref[0], preferred_element_type=jnp.float32,
                       precision=lax.Precision.HIGHEST) + b_ref[0]


def _ada_mods(cond, w_ada, b_ada):
    nl, d, n6 = w_ada.shape
    tn = 1536
    return pl.pallas_call(
        _ada_kernel,
        grid=(nl, n6 // tn),
        in_specs=[pl.BlockSpec((8, d), lambda l, j: (0, 0)),
                  pl.BlockSpec((1, d, tn), lambda l, j: (l, 0, j)),
                  pl.BlockSpec((1, 1, tn), lambda l, j: (l, 0, j))],
        out_specs=pl.BlockSpec((1, 8, tn), lambda l, j: (l, 0, j)),
        out_shape=jax.ShapeDtypeStruct((nl, 8, n6), jnp.float32),
        compiler_params=_params("arbitrary", "arbitrary"),
        name="ada_mods",
    )(cond, w_ada, b_ada.reshape(nl, 1, n6))


def _swap_halves(x, half):
    lane = lax.broadcasted_iota(jnp.int32, (1, LANES), 1)
    first = jnp.bitwise_and(lane, 2 * half - 1) < half
    blocks = []
    for b in range(x.shape[1] // LANES):
        xb = x[:, b * LANES:(b + 1) * LANES]
        blocks.append(jnp.where(first, pltpu.roll(xb, LANES - half, 1), pltpu.roll(xb, half, 1)))
    return jnp.concatenate(blocks, axis=1)


def _proj_kernel(*refs, latent):
    if latent:
        (x_ref, mod_ref, gpre_ref, w_ref, wt_ref, gq_ref, wuq_ref, gkv_ref, wuk_ref, wuv_ref,
         ones_ref, ggq_ref, ggk_ref, cg_ref, sg_ref, cm_ref, sm_ref,
         qna_ref, kna_ref, vna_ref, qm_ref, km_ref, vm_ref, qg_ref, kg_ref, vg_ref, wbf_ref) = refs
    else:
        (x_ref, mod_ref, gpre_ref, w_ref, wt_ref, gq_ref, wuq_ref, gkv_ref, wuk_ref, wuv_ref,
         ones_ref, ggq_ref, ggk_ref) = refs[:13]
        (qna_ref, kna_ref, vna_ref, qm_ref, ckv_ref, kr_ref, km_ref, vm_ref, qg_ref, kg_ref, vg_ref, wbf_ref) = refs[19:]

    @pl.when(pl.program_id(0) == 0)
    def _():
        wbf_ref[...] = w_ref[...].astype(jnp.bfloat16)

    x = x_ref[...]
    mod = mod_ref[0]
    sh, sc = mod[0:1, :], mod[1:2, :]
    h = (_rms(x, gpre_ref[...]) * (1.0 + sc) + sh).astype(jnp.bfloat16)

    groups = [(c0, _dot(h, wbf_ref[:, c0:c0 + PROJ_GROUP])) for c0 in range(0, C_QG, PROJ_GROUP)]
    groups += [(c0, _dot(h, wt_ref[:, c0 - C_QG:min(c0 + PROJ_GROUP, C_END) - C_QG])) for c0 in range(C_QG, C_END, PROJ_GROUP)]

    def piece(c0, width):
        parts = []
        for b0 in range(c0, c0 + width, LANES):
            g0, y = groups[b0 // PROJ_GROUP]
            parts.append(y[:, b0 - g0:b0 - g0 + LANES])
        return parts[0] if len(parts) == 1 else jnp.concatenate(parts, axis=1)

    na_scale = HEAD_DIM ** -0.5
    dense_unit = LOG2_E if latent else 1.0
    qna_ref[...] = (piece(C_QNA, NA_WIDTH) * na_scale).astype(qna_ref.dtype)
    def put(ref, val):
        ref[...] = val.astype(ref.dtype).reshape(ref.shape)

    put(kna_ref, piece(C_KNA, NA_WIDTH))
    put(vna_ref, piece(C_VNA, NA_WIDTH))

    mla_scale = (MLA_NOPE + MLA_ROPE) ** -0.5
    cqn = _rms(piece(C_CQ, MLA_Q_LORA), gq_ref[...]).astype(jnp.bfloat16)
    qm = _dot(cqn, wuq_ref[...])
    kr = piece(C_KR, LANES)
    if latent:
        cm = jnp.concatenate([cm_ref[...]] * MLA_HEADS, axis=1)
        sm = jnp.concatenate([sm_ref[...]] * MLA_HEADS, axis=1)
        qm = qm * cm + _swap_halves(qm, MLA_ROPE // 4) * sm
        kr = kr * cm_ref[...] + _swap_halves(kr, MLA_ROPE // 4) * sm_ref[...]
    qm_ref[...] = (qm * (mla_scale * dense_unit)).astype(qm_ref.dtype)
    ckv = _rms(piece(C_CKV, MLA_KV_LORA), gkv_ref[...])
    ckv_b = ckv.astype(jnp.bfloat16)
    kn = _dot(ckv_b, wuk_ref[...])
    km_ref[...] = (kn + jnp.concatenate([kr] * MLA_HEADS, axis=1)).astype(km_ref.dtype)
    vm_ref[...] = _dot(ckv_b, wuv_ref[...]).astype(vm_ref.dtype)
    if not latent:
        put(ckv_ref, ckv)
        put(kr_ref, kr[:, MLA_NOPE:MLA_NOPE + MLA_ROPE])

    ones = ones_ref[...]
    qg = piece(C_QG, GQA_WIDTH)
    qq = (qg * qg).astype(jnp.bfloat16)
    ms = jnp.concatenate([_dot(qq[:, b * LANES:(b + 1) * LANES], ones) for b in range(GQA_WIDTH // LANES)], axis=1)
    qg = qg * lax.rsqrt(ms + EPS) * ggq_ref[...]
    kg = piece(C_KG, LANES)
    kg = kg * lax.rsqrt(_head_mean_sq(kg, ones) + EPS) * ggk_ref[...]
    if latent:
        cg, sg = cg_ref[...], sg_ref[...]
        cg3 = jnp.concatenate([cg] * (GQA_WIDTH // LANES), axis=1)
        sg3 = jnp.concatenate([sg] * (GQA_WIDTH // LANES), axis=1)
        qg = qg * cg3 + _swap_halves(qg, HEAD_DIM // 4) * sg3
        kg = kg * cg + _swap_halves(kg, HEAD_DIM // 4) * sg
    qg_ref[...] = (qg * (na_scale * dense_unit)).astype(qg_ref.dtype)
    put(kg_ref, kg)
    put(vg_ref, piece(C_VG, LANES))


def _layer_spec(entry, buffered=False, cols=None):
    a, layer = entry
    kw = dict(pipeline_mode=pl.Buffered(1)) if buffered else {}
    if layer is None:
        return pl.BlockSpec(a.shape, lambda *_: (0,) * a.ndim, **kw)
    shape = a.shape[1:] if cols is None else a.shape[1:-1] + (cols,)
    return pl.BlockSpec((None,) + shape, lambda *_: (layer,) + (0,) * (a.ndim - 1), **kw)


def _proj(x, mods, lw, rope, *, latent, seq, caches=None, layer=None):
    ntok, d = x.shape
    t = PROJ_TILE
    nt = ntok // t
    tiles_per_seq = seq // t if latent else 1
    tok = lambda w: pl.BlockSpec((t, w), lambda i: (i, 0))
    full = lambda e: _layer_spec(e, buffered=True)
    if latent:
        mod_spec = pl.BlockSpec((1, 6, d), lambda i: (i // tiles_per_seq, 0, 0))
    else:
        mod_spec = pl.BlockSpec((1, 6, d), lambda i: (0, 0, 0))
    kv_dt = jnp.bfloat16 if latent else jnp.float32
    bf = jnp.bfloat16
    sds = lambda w, dt: jax.ShapeDtypeStruct((ntok, w), dt)
    common = [x, mods, lw["g_pre_mix"], lw["w_in"], lw["w_tail"], lw["mla_g_q"], lw["w_uq"], lw["mla_g_kv"],
              lw["w_uk"], lw["w_uv"], lw["ones"], lw["ggq"], lw["ggk"]]
    in_specs = [tok(d), mod_spec] + [full(a) for a in common[2:]]
    in_specs[3] = _layer_spec(lw["w_in"], buffered=True, cols=C_QG)
    if latent:
        rope_spec = pl.BlockSpec((t, LANES), lambda i: (i % tiles_per_seq, 0))
        ins = common + [rope["cg"], rope["sg"], rope["cm"], rope["sm"]]
        in_specs = in_specs + [rope_spec] * 4
        outs = [sds(NA_WIDTH, bf), sds(NA_WIDTH, kv_dt), sds(NA_WIDTH, kv_dt), sds(MLA_HEADS * LANES, bf),
                sds(MLA_HEADS * LANES, bf), sds(MLA_WIDTH, bf), sds(GQA_WIDTH, bf), sds(LANES, kv_dt), sds(LANES, kv_dt)]
        aliases = {}
    else:
        ins = common + list(caches)
        in_specs = in_specs + [pl.BlockSpec(memory_space=pl.ANY)] * len(caches)
        cache_sds = [jax.ShapeDtypeStruct(a.shape, a.dtype) for a in caches]
        outs = [sds(NA_WIDTH, bf), cache_sds[0], cache_sds[1], sds(MLA_HEADS * LANES, bf), cache_sds[2], cache_sds[3],
                sds(MLA_HEADS * LANES, bf), sds(MLA_WIDTH, bf), sds(GQA_WIDTH, bf), cache_sds[4], cache_sds[5]]
        cache_out = (1, 2, 4, 5, 9, 10)
        aliases = {len(common) + k: cache_out[k] for k in range(len(caches))}
    spt = t // seq
    cache_spec = lambda o: pl.BlockSpec((spt, None, seq, o.shape[-1]), lambda i: (i, layer, 0, 0))
    out_specs = [cache_spec(o) if len(o.shape) == 4 else tok(o.shape[1]) for o in outs]
    return pl.pallas_call(
        functools.partial(_proj_kernel, latent=latent),
        grid=(nt,),
        in_specs=in_specs,
        out_specs=out_specs,
        out_shape=outs,
        input_output_aliases=aliases,
        scratch_shapes=[pltpu.VMEM((d, C_QG), jnp.bfloat16)],
        compiler_params=_params("arbitrary"),
        name="proj_lat" if latent else "proj_ctx",
    )(*[a[0] if isinstance(a, tuple) else a for a in ins])


def _lane_lo():
    return lax.broadcasted_iota(jnp.int32, (1, LANES), 1) < HEAD_DIM


def _split_heads(q):
    lo = _lane_lo()
    zero = jnp.zeros_like(q)
    return jnp.where(lo, q, zero), jnp.where(lo, zero, q)


def _softmax_pv(scores, values):
    m = scores[0].max(axis=-1, keepdims=True)
    for s in scores[1:]:
        m = jnp.maximum(m, s.max(axis=-1, keepdims=True))
    l = None
    acc = None
    for s, v in zip(scores, values):
        p = jnp.exp(s - m)
        ps = p.sum(axis=-1, keepdims=True)
        pv = _dot(p.astype(jnp.bfloat16), v)
        l = ps if l is None else l + ps
        acc = pv if acc is None else acc + pv
    return acc / l


def _ctx_attn_kernel(qna_ref, kna_ref, vna_ref, qm_ref, km_ref, vm_ref, qg_ref, kg_ref, vg_ref, ona_ref, odn_ref):
    lo = _lane_lo()
    bf = jnp.bfloat16

    def pair(q_lo, q_hi, k_lo, k_hi, v):
        o_lo = _softmax_pv([_dot_nt(q_lo, k_lo)], [v])
        o_hi = _softmax_pv([_dot_nt(q_hi, k_hi)], [v])
        return jnp.where(lo, o_lo, o_hi)

    for b in range(qna_ref.shape[0]):
        for p in range(NA_WIDTH // LANES):
            cs = slice(p * LANES, (p + 1) * LANES)
            q_lo, q_hi = _split_heads(qna_ref[b, :, cs])
            k = kna_ref[b, :, cs].astype(bf)
            ona_ref[b, :, cs] = pair(q_lo, q_hi, k, k, vna_ref[b, :, cs].astype(bf)).astype(ona_ref.dtype)
        for p in range(MLA_HEADS // 2):
            c0 = 2 * p * LANES
            o = pair(qm_ref[b, :, c0:c0 + LANES], qm_ref[b, :, c0 + LANES:c0 + 2 * LANES],
                     km_ref[b, :, c0:c0 + LANES], km_ref[b, :, c0 + LANES:c0 + 2 * LANES],
                     vm_ref[b, :, p * LANES:(p + 1) * LANES])
            odn_ref[b, :, p * LANES:(p + 1) * LANES] = o.astype(odn_ref.dtype)
        kg = kg_ref[b].astype(bf)
        vg = vg_ref[b].astype(bf)
        for c in range(GQA_WIDTH // LANES):
            q_lo, q_hi = _split_heads(qg_ref[b, :, c * LANES:(c + 1) * LANES])
            o = pair(q_lo, q_hi, kg, kg, vg)
            odn_ref[b, :, MLA_WIDTH + c * LANES:MLA_WIDTH + (c + 1) * LANES] = o.astype(odn_ref.dtype)


def _ctx_attn(qna, kna, vna, qm, km, vm, qg, kg, vg, layer, *, batch, seq):
    nb = CTX_BATCHES
    ins = [a if a.ndim == 4 else a.reshape(batch, seq, a.shape[-1]) for a in (qna, kna, vna, qm, km, vm, qg, kg, vg)]
    spec = lambda a: (pl.BlockSpec((nb, None, seq, a.shape[-1]), lambda b: (b, layer, 0, 0)) if len(a.shape) == 4
                      else pl.BlockSpec((nb, seq, a.shape[-1]), lambda b: (b, 0, 0)))
    outs = [jax.ShapeDtypeStruct((batch, seq, NA_WIDTH), jnp.bfloat16),
            jax.ShapeDtypeStruct((batch, seq, DENSE_WIDTH), jnp.bfloat16)]
    ona, odn = pl.pallas_call(
        _ctx_attn_kernel,
        grid=(batch // nb,),
        in_specs=[spec(a) for a in ins],
        out_specs=[spec(o) for o in outs],
        out_shape=outs,
        compiler_params=_params("arbitrary"),
        name="ctx_attn",
    )(*ins)
    return ona.reshape(batch * seq, NA_WIDTH), odn.reshape(batch * seq, DENSE_WIDTH)


def _na_attn_kernel(q_ref, k_ref, v_ref, kc_ref, vc_ref, pair_ref, o_ref, *, rows):
    lo = _lane_lo()
    blk = pl.program_id(1)
    row0 = blk * NA_ROWS
    key_row0 = jnp.clip(row0 - NA_KH // 2, 0, rows - NA_KEY_ROWS)
    start = pl.multiple_of(key_row0 * GRID_W, GRID_W)
    tq = NA_ROWS * GRID_W
    nkeys = NA_KEY_ROWS * GRID_W

    rq = row0 + lax.shift_right_logical(lax.broadcasted_iota(jnp.int32, (tq, 1), 0), 6)
    rk = key_row0 + lax.shift_right_logical(lax.broadcasted_iota(jnp.int32, (1, nkeys), 1), 6)
    rs = jnp.clip(rq - NA_KH // 2, 0, rows - NA_KH)
    row_mask = jnp.where((rk >= rs) & (rk < rs + NA_KH), 0.0, MASK_VALUE)

    def bias(h):
        base = key_row0 - row0 + (NA_KH - 1) + NA_PAD
        blocks = [jnp.concatenate([pair_ref[0, h, base + 2 * m - dq] for m in range(NA_KEY_ROWS // 2)], axis=1)
                  for dq in range(NA_ROWS)]
        return jnp.concatenate(blocks, axis=0) + row_mask

    for p in range(NA_WIDTH // LANES):
        cs = slice(p * LANES, (p + 1) * LANES)
        q_lo, q_hi = _split_heads(q_ref[0, :, cs])
        k = k_ref[0, pl.ds(start, nkeys), cs]
        v = v_ref[0, pl.ds(start, nkeys), cs]
        kc = kc_ref[0, 0, :, cs]
        vc = vc_ref[0, 0, :, cs]
        o_lo = _softmax_pv([_dot_nt(q_lo, k) + bias(2 * p), _dot_nt(q_lo, kc)], [v, vc])
        o_hi = _softmax_pv([_dot_nt(q_hi, k) + bias(2 * p + 1), _dot_nt(q_hi, kc)], [v, vc])
        o_ref[0, :, cs] = jnp.where(lo, o_lo, o_hi).astype(o_ref.dtype)


def _na_attn(q, k, v, kc, vc, pairs, layer, *, batch, seq):
    rows = seq // GRID_W
    nblk = rows // NA_ROWS
    tq = NA_ROWS * GRID_W
    past = kc.shape[2]
    assert GRID_W == 64 and rows >= NA_KEY_ROWS and rows % NA_ROWS == 0
    q3, k3, v3 = (a.reshape(batch, seq, NA_WIDTH) for a in (q, k, v))
    o = pl.pallas_call(
        functools.partial(_na_attn_kernel, rows=rows),
        grid=(batch, nblk),
        in_specs=[pl.BlockSpec((1, tq, NA_WIDTH), lambda b, i: (b, i, 0)),
                  pl.BlockSpec((1, seq, NA_WIDTH), lambda b, i: (b, 0, 0)),
                  pl.BlockSpec((1, seq, NA_WIDTH), lambda b, i: (b, 0, 0)),
                  pl.BlockSpec((1, 1, past, NA_WIDTH), lambda b, i: (b, layer, 0, 0)),
                  pl.BlockSpec((1, 1, past, NA_WIDTH), lambda b, i: (b, layer, 0, 0)),
                  pl.BlockSpec((1,) + pairs.shape[1:], lambda b, i: (layer, 0, 0, 0, 0))],
        out_specs=pl.BlockSpec((1, tq, NA_WIDTH), lambda b, i: (b, i, 0)),
        out_shape=jax.ShapeDtypeStruct((batch, seq, NA_WIDTH), jnp.bfloat16),
        compiler_params=_params("arbitrary", "arbitrary"),
        name="na_attn",
    )(q3, k3, v3, kc, vc, pairs)
    return o.reshape(batch * seq, NA_WIDTH)


def _na_pair_tables(na_rpb):
    cq = np.arange(GRID_W)[:, None]
    ck = np.arange(GRID_W)[None, :]
    cs = np.clip(cq - NA_KW // 2, 0, GRID_W - NA_KW)
    valid = (ck >= cs) & (ck < cs + NA_KW)
    onehot = ((ck - cq + NA_KW - 1)[None] == np.arange(2 * NA_KW - 1)[:, None, None]) & valid[None]
    t = jnp.einsum("lhdo,oqk->lhdqk", na_rpb, jnp.asarray(onehot, na_rpb.dtype), precision=lax.Precision.HIGHEST)
    t = jnp.where(jnp.asarray(valid), t, MASK_VALUE)
    t = jnp.pad(t, ((0, 0), (0, 0), (NA_PAD, NA_PAD), (0, 0), (0, 0)), constant_values=MASK_VALUE)
    return jnp.concatenate([t[:, :, :-1], t[:, :, 1:]], axis=-1)


def _dense_attn_kernel(qm_ref, qg_ref, kmc_ref, vmc_ref, kgc_ref, vgc_ref, kml_ref, vml_ref, kgl_ref, vgl_ref,
                       o_ref, s_lo_ref, s_hi_ref, *, nk, tk):
    lo = _lane_lo()
    bf = jnp.bfloat16
    past = kmc_ref.shape[2]

    def lane_fold(x, op):
        parts = [x[:, c * LANES:(c + 1) * LANES] for c in range(x.shape[1] // LANES)]
        while len(parts) > 1:
            parts = [op(parts[i], parts[i + 1]) if i + 1 < len(parts) else parts[i] for i in range(0, len(parts), 2)]
        return parts[0]

    def head(q, kc, vc, k_at, v_at, s_ref):
        spans = [(0, past)] + [(past + j * tk, tk) for j in range(nk)]
        mx = None
        for idx, (c0, width) in enumerate(spans):
            s = _dot_nt(q, kc if idx == 0 else k_at(idx - 1))
            s_ref[:, c0:c0 + width] = s
            part = lane_fold(s, jnp.maximum)
            mx = part if mx is None else jnp.maximum(mx, part)
        m = mx.max(axis=-1, keepdims=True)
        lsum = None
        acc = None
        for idx, (c0, width) in enumerate(spans):
            p = jnp.exp2(s_ref[:, c0:c0 + width] - m)
            part = lane_fold(p, jnp.add)
            pv = _dot(p.astype(bf), vc if idx == 0 else v_at(idx - 1))
            lsum = part if lsum is None else lsum + part
            acc = pv if acc is None else acc + pv
        return acc / lsum.sum(axis=-1, keepdims=True)

    def unit(q_lo, q_hi, kc_lo, kc_hi, vc, k_lo_at, k_hi_at, v_at):
        o_lo = head(q_lo, kc_lo, vc, k_lo_at, v_at, s_lo_ref)
        o_hi = head(q_hi, kc_hi, vc, k_hi_at, v_at, s_hi_ref)
        return jnp.where(lo, o_lo, o_hi)

    for p in range(MLA_HEADS // 2):
        c0 = 2 * p * LANES
        c1 = c0 + LANES
        o = unit(qm_ref[0, :, c0:c1], qm_ref[0, :, c1:c1 + LANES],
                 kmc_ref[0, 0, :, c0:c1], kmc_ref[0, 0, :, c1:c1 + LANES], vmc_ref[0, 0, :, p * LANES:(p + 1) * LANES],
                 lambda j, c0=c0, c1=c1: kml_ref[0, j * tk:(j + 1) * tk, c0:c1],
                 lambda j, c1=c1: kml_ref[0, j * tk:(j + 1) * tk, c1:c1 + LANES],
                 lambda j, p=p: vml_ref[0, j * tk:(j + 1) * tk, p * LANES:(p + 1) * LANES])
        o_ref[0, :, p * LANES:(p + 1) * LANES] = o.astype(o_ref.dtype)
    kgc = kgc_ref[0, 0]
    vgc = vgc_ref[0, 0]
    kg_at = lambda j: kgl_ref[0, j * tk:(j + 1) * tk, :]
    vg_at = lambda j: vgl_ref[0, j * tk:(j + 1) * tk, :]
    for c in range(GQA_WIDTH // LANES):
        q_lo, q_hi = _split_heads(qg_ref[0, :, c * LANES:(c + 1) * LANES])
        o = unit(q_lo, q_hi, kgc, kgc, vgc, kg_at, kg_at, vg_at)
        o_ref[0, :, MLA_WIDTH + c * LANES:MLA_WIDTH + (c + 1) * LANES] = o.astype(o_ref.dtype)


def _dense_attn(qm, qg, kmc, vmc, kgc, vgc, kml, vml, kgl, vgl, layer, *, batch, seq):
    tq, tk = DENSE_Q_TILE, DENSE_K_TILE
    r3 = lambda a: a.reshape(batch, seq, a.shape[-1])
    qm, qg, kml, vml, kgl, vgl = (r3(a) for a in (qm, qg, kml, vml, kgl, vgl))
    qspec = lambda a: pl.BlockSpec((1, tq, a.shape[-1]), lambda b, i: (b, i, 0))
    cspec = lambda a: pl.BlockSpec((1, 1) + a.shape[2:], lambda b, i: (b, layer, 0, 0))
    lspec = lambda a: pl.BlockSpec((1, seq, a.shape[-1]), lambda b, i: (b, 0, 0), pipeline_mode=pl.Buffered(1))
    o = pl.pallas_call(
        functools.partial(_dense_attn_kernel, nk=seq // tk, tk=tk),
        grid=(batch, seq // tq),
        in_specs=[qspec(qm), qspec(qg), cspec(kmc), cspec(vmc), cspec(kgc), cspec(vgc),
                  lspec(kml), lspec(vml), lspec(kgl), lspec(vgl)],
        out_specs=pl.BlockSpec((1, tq, DENSE_WIDTH), lambda b, i: (b, i, 0)),
        out_shape=jax.ShapeDtypeStruct((batch, seq, DENSE_WIDTH), jnp.bfloat16),
        scratch_shapes=[pltpu.VMEM((tq, kmc.shape[2] + seq), jnp.float32)] * 2,
        compiler_params=_params("arbitrary", "arbitrary"),
        name="dense_attn",
    )(qm, qg, kmc, vmc, kgc, vgc, kml, vml, kgl, vgl)
    return o.reshape(batch * seq, DENSE_WIDTH)


def _mla_ctx_kernel(ckv_ref, kr_ref, wuk_ref, wuv_ref, k_ref, v_ref):
    ckv = ckv_ref[0, 0]
    kn = _dot(ckv, wuk_ref[0])
    k_ref[0, 0] = (kn + jnp.concatenate([kr_ref[0, 0].astype(jnp.float32)] * MLA_HEADS, axis=1)).astype(k_ref.dtype)
    v_ref[0, 0] = _dot(ckv, wuv_ref[0]).astype(v_ref.dtype)


def _mla_ctx_expand(ckv, kr_pad, wuk, wuv):
    b, nl, past, _ = ckv.shape
    spec = lambda w: pl.BlockSpec((1, 1, past, w), lambda i, l: (i, l, 0, 0))
    wspec = lambda a: pl.BlockSpec((1,) + a.shape[1:], lambda i, l: (l, 0, 0))
    return pl.pallas_call(
        _mla_ctx_kernel,
        grid=(b, nl),
        in_specs=[spec(MLA_KV_LORA), spec(LANES), wspec(wuk), wspec(wuv)],
        out_specs=[spec(MLA_HEADS * LANES), spec(MLA_WIDTH)],
        out_shape=[jax.ShapeDtypeStruct((b, nl, past, MLA_HEADS * LANES), jnp.bfloat16),
                   jax.ShapeDtypeStruct((b, nl, past, MLA_WIDTH), jnp.bfloat16)],
        compiler_params=_params("arbitrary", "arbitrary"),
        name="mla_ctx_expand",
    )(ckv, kr_pad, wuk, wuv)


def _mix_ffn_kernel(ona_ref, odn_ref, x_ref, onap_ref, odnp_ref, xp_ref, onan_ref, odnn_ref, xn_ref, mod_ref,
                    wna_ref, wdn_ref, gpost_ref, gpre_ref, wup_ref, cw_ref, cb_ref, wd_ref, gffn_ref,
                    o_ref, hperm_ref, yperm_ref, *, seq, chunk):
    i = pl.program_id(0)
    t, d = x_ref.shape
    halo = xp_ref.shape[0]
    dff = wd_ref.shape[0]
    groups = t // 8
    mod = mod_ref[0]
    gt_m, sh_f, sc_f, gt_f = mod[2:3, :], mod[3:4, :], mod[4:5, :], mod[5:6, :]

    cat0 = lambda parts: jnp.concatenate(parts, axis=0)
    use_halo = seq > t
    if use_halo:
        ona = cat0([ona_ref[...], onap_ref[...], onan_ref[...]])
        odn = cat0([odn_ref[...], odnp_ref[...], odnn_ref[...]])
        x = cat0([x_ref[...], xp_ref[...], xn_ref[...]])
    else:
        ona, odn, x = ona_ref[...], odn_ref[...], x_ref[...]
    o = _dot(jnp.concatenate([ona, odn], axis=1), jnp.concatenate([wna_ref[...], wdn_ref[...]], axis=0))
    x1 = x + gt_m * _rms(o, gpost_ref[...])
    h2 = _rms(x1, gpre_ref[...]) * (1.0 + sc_f) + sh_f
    o_ref[...] = x1[0:t, :]

    nlb = d // LANES
    for j in range(nlb):
        for k in range(8):
            hperm_ref[j, pl.ds(k, groups, stride=8), :] = h2[k * groups:(k + 1) * groups, j * LANES:(j + 1) * LANES]
    h = jnp.concatenate([hperm_ref[j] for j in range(nlb)], axis=1)
    if use_halo:
        h = cat0([h, h2[t:, :]])
    h = h.astype(jnp.bfloat16)

    kk = lax.broadcasted_iota(jnp.int32, (8, 1), 0)
    tok = i * t + groups * kk
    prev_zero = jnp.bitwise_and(tok, seq - 1) == 0
    next_zero = jnp.bitwise_and(tok + groups, seq - 1) == 0

    def conv(c0):
        w = wup_ref[:, c0:c0 + chunk]
        u_all = _dot(h, w)
        u = u_all[0:t, :]
        b_prev = pltpu.roll(u[t - 8:t, :], 1, 0)
        b_next = pltpu.roll(u[0:8, :], 7, 0)
        if use_halo:
            b_prev = jnp.where(kk == 0, u_all[t + halo - 1:t + halo, :], b_prev)
            b_next = jnp.where(kk == 7, u_all[t + halo:t + halo + 1, :], b_next)
        b_prev = jnp.where(prev_zero, 0.0, b_prev)
        b_next = jnp.where(next_zero, 0.0, b_next)
        prev = jnp.concatenate([b_prev, u[0:t - 8, :]], axis=0)
        nxt = jnp.concatenate([u[8:t, :], b_next], axis=0)
        cw = cw_ref[:, c0:c0 + chunk]
        return prev * cw[0:1, :] + u * cw[1:2, :] + nxt * cw[2:3, :] + cb_ref[:, c0:c0 + chunk]

    acts = []
    for c in range(dff // chunk):
        gate = conv(c * chunk)
        val = conv(dff + c * chunk)
        acts.append((gate / (1.0 + jnp.exp(-gate)) * val).astype(jnp.bfloat16))
    acc = _dot(jnp.concatenate(acts, axis=1), wd_ref[...])

    y = gt_f * _rms(acc, gffn_ref[...])
    for j in range(nlb):
        yperm_ref[j] = y[:, j * LANES:(j + 1) * LANES]
    for k in range(8):
        rows = slice(k * groups, (k + 1) * groups)
        yk = jnp.concatenate([yperm_ref[j, pl.ds(k, groups, stride=8), :] for j in range(nlb)], axis=1)
        o_ref[rows, :] = o_ref[rows, :] + yk


def _mix_ffn(ona, odn, x, mods, lw, *, seq, seq_per_mod):
    ntok, d = x.shape
    t = FFN_TILE
    halo = 16
    assert seq & (seq - 1) == 0 and (t % seq == 0 or seq % t == 0) and ntok % t == 0 and t % 64 == 0
    nhalo = ntok // halo
    if seq_per_mod is None:
        mod_spec = pl.BlockSpec((1, 6, d), lambda i: (0, 0, 0))
    else:
        tps = seq_per_mod // t
        mod_spec = pl.BlockSpec((1, 6, d), lambda i: (i // tps, 0, 0))
    tok = lambda w: pl.BlockSpec((t, w), lambda i: (i, 0))
    prv = lambda w: pl.BlockSpec((halo, w), lambda i: (jnp.maximum(i * (t // halo) - 1, 0), 0))
    nxt = lambda w: pl.BlockSpec((halo, w), lambda i: (jnp.minimum((i + 1) * (t // halo), nhalo - 1), 0))
    full = lambda e: _layer_spec(e, buffered=True)
    weights = [lw["w_out_na"], lw["w_out_dn"], lw["g_post_mix"], lw["g_pre_ffn"], lw["w_up"], lw["conv_w"],
               lw["conv_b"], lw["w_down"], lw["g_post_ffn"]]
    widths = (NA_WIDTH, DENSE_WIDTH, d)
    return pl.pallas_call(
        functools.partial(_mix_ffn_kernel, seq=seq, chunk=FF_CHUNK),
        grid=(ntok // t,),
        in_specs=[tok(w) for w in widths] + [prv(w) for w in widths] + [nxt(w) for w in widths] + [mod_spec]
                 + [full(a) for a in weights],
        out_specs=tok(d),
        out_shape=jax.ShapeDtypeStruct((ntok, d), jnp.float32),
        scratch_shapes=[pltpu.VMEM((d // LANES, t, LANES), jnp.float32)] * 2,
        compiler_params=_params("arbitrary"),
        name="mix_ffn",
    )(ona, odn, x, ona, odn, x, ona, odn, x, mods, *[w[0] for w in weights])


def _prep_weights(w_in, mla_w_uq, mla_w_ukv, w_out, gqa_g_q, gqa_g_k):
    nl, d, _ = w_in.shape
    bf = jnp.bfloat16
    cat = lambda parts: jnp.concatenate(parts, axis=-1)
    o_kr = 3 * NA_WIDTH + MLA_Q_LORA + MLA_KV_LORA
    o_qg = o_kr + MLA_ROPE
    o_kg = o_qg + GQA_WIDTH
    o_vg = o_kg + GQA_KV_HEADS * HEAD_DIM
    pad_r = LANES - MLA_NOPE - MLA_ROPE
    z = lambda rows, n: jnp.zeros((nl, rows, n), bf)
    tail = w_in[..., o_kr:].astype(bf)
    kr = tail[..., :MLA_ROPE]
    qg = cat([tail[..., o_qg - o_kr + HEAD_DIM * h:o_qg - o_kr + HEAD_DIM * (h + 1)] for h in GQA_ORDER])
    kg = tail[..., o_kg - o_kr:o_vg - o_kr]
    assert o_kr == C_QG and C_QG % PROJ_GROUP == 0
    w_tail = cat([qg, kg, tail[..., o_vg - o_kr:], z(d, MLA_NOPE), kr, z(d, pad_r)])
    assert w_tail.shape[-1] == C_END - C_QG

    uq = mla_w_uq.astype(bf)
    ukv = mla_w_ukv.astype(bf)
    ql, kl = uq.shape[1], ukv.shape[1]
    qw = MLA_NOPE + MLA_ROPE
    kvw = MLA_NOPE + MLA_V
    w_uq = cat([p for h in range(MLA_HEADS) for p in (uq[..., qw * h:qw * (h + 1)], z(ql, pad_r))])
    w_uk = cat([p for h in range(MLA_HEADS) for p in (ukv[..., kvw * h:kvw * h + MLA_NOPE], z(kl, LANES - MLA_NOPE))])
    w_uv = cat([ukv[..., kvw * h + MLA_NOPE:kvw * (h + 1)] for h in range(MLA_HEADS)])

    wo = w_out.astype(bf)
    o_gqa = NA_WIDTH + MLA_WIDTH
    w_out_dn = jnp.concatenate([wo[:, NA_WIDTH:o_gqa]]
                               + [wo[:, o_gqa + HEAD_DIM * h:o_gqa + HEAD_DIM * (h + 1)] for h in GQA_ORDER], axis=1)

    ggq = jnp.tile(gqa_g_q, (1, GQA_HEADS))[:, None, :]
    ggk = jnp.tile(gqa_g_k, (1, GQA_KV_HEADS))[:, None, :]
    ones = jnp.asarray(np.kron(np.eye(LANES // HEAD_DIM), np.full((HEAD_DIM, HEAD_DIM), 1.0 / HEAD_DIM)), bf)
    return dict(w_tail=w_tail, w_uq=w_uq, w_uk=w_uk, w_uv=w_uv, w_out_na=wo[:, :NA_WIDTH], w_out_dn=w_out_dn,
                ggq=ggq, ggk=ggk, ones=ones)


def _rope_tables(seq):
    f32 = np.float32
    t = np.arange(seq)
    r, c = (t // GRID_W).astype(f32), (t % GRID_W).astype(f32)

    def tables(dim):
        quarter = dim // 4
        freqs = f32(ROPE_THETA) ** (-np.arange(quarter, dtype=f32) / f32(quarter))
        ar_, ac_ = r[:, None] * freqs[None, :], c[:, None] * freqs[None, :]
        cos = np.concatenate([np.cos(ar_), np.cos(ar_), np.cos(ac_), np.cos(ac_)], axis=1)
        sin = np.concatenate([-np.sin(ar_), np.sin(ar_), -np.sin(ac_), np.sin(ac_)], axis=1)
        return cos.astype(f32), sin.astype(f32)

    c64, s64 = tables(HEAD_DIM)
    c32, s32 = tables(MLA_ROPE)
    pad = LANES - MLA_NOPE - MLA_ROPE
    cm = np.concatenate([np.ones((seq, MLA_NOPE), f32), c32, np.zeros((seq, pad), f32)], axis=1)
    sm = np.concatenate([np.zeros((seq, MLA_NOPE), f32), s32, np.zeros((seq, pad), f32)], axis=1)
    return dict(cg=jnp.asarray(np.tile(c64, (1, 2))), sg=jnp.asarray(np.tile(s64, (1, 2))),
                cm=jnp.asarray(cm), sm=jnp.asarray(sm))


def kernel(x_prompt, x_sample, c, cache_na_k, cache_na_v, cache_mla_ckv, cache_mla_krope, cache_gqa_k, cache_gqa_v, c_ctx, w_ada, b_ada, g_pre_mix, g_post_mix, g_pre_ffn, g_post_ffn, w_in, na_rpb, mla_g_q, mla_w_uq, mla_g_kv, mla_w_ukv, gqa_g_q, gqa_g_k, w_out, ffn_w_up, ffn_conv_w, ffn_conv_b, ffn_w_down):
    batch, seq, d = x_prompt.shape
    dbatch, dseq, _ = x_sample.shape
    nl = w_ada.shape[0]
    past = cache_na_k.shape[2]
    bf = jnp.bfloat16
    assert dbatch + 1 <= 8 and dseq % GRID_W == 0

    cond = jnp.zeros((8, d), jnp.float32).at[0].set(c_ctx).at[1:1 + dbatch].set(c)
    mods_all = _ada_mods(cond, w_ada, b_ada).reshape(nl, 8, 6, d)

    pw = _prep_weights(w_in, mla_w_uq, mla_w_ukv, w_out, gqa_g_q, gqa_g_k)
    rope = _rope_tables(dseq)
    vec = lambda g: g[:, None, :]
    stacked = dict(
        g_pre_mix=vec(g_pre_mix), g_post_mix=vec(g_post_mix), g_pre_ffn=vec(g_pre_ffn), g_post_ffn=vec(g_post_ffn),
        mla_g_q=vec(mla_g_q), mla_g_kv=vec(mla_g_kv), w_in=w_in, w_tail=pw["w_tail"],
        w_uq=pw["w_uq"], w_uk=pw["w_uk"], w_uv=pw["w_uv"], ggq=pw["ggq"], ggk=pw["ggk"],
        w_out_na=pw["w_out_na"], w_out_dn=pw["w_out_dn"],
        w_up=ffn_w_up.astype(bf), w_down=ffn_w_down.astype(bf), conv_w=ffn_conv_w, conv_b=vec(ffn_conv_b))

    kc_na = cache_na_k.reshape(dbatch, nl, past, NA_WIDTH).astype(bf)
    vc_na = cache_na_v.reshape(dbatch, nl, past, NA_WIDTH).astype(bf)
    kc_g = cache_gqa_k.reshape(dbatch, nl, past, GQA_KV_HEADS * HEAD_DIM).astype(bf)
    vc_g = cache_gqa_v.reshape(dbatch, nl, past, GQA_KV_HEADS * HEAD_DIM).astype(bf)
    kr_pad = jnp.pad(cache_mla_krope, ((0, 0), (0, 0), (0, 0), (MLA_NOPE, LANES - MLA_NOPE - MLA_ROPE))).astype(bf)
    kc_m, vc_m = _mla_ctx_expand(cache_mla_ckv.astype(bf), kr_pad, pw["w_uk"], pw["w_uv"])

    na_pairs = _na_pair_tables(na_rpb)

    xp = x_prompt.reshape(batch * seq, d)
    xs = x_sample.reshape(dbatch * dseq, d)
    cache_widths = (NA_WIDTH, NA_WIDTH, MLA_KV_LORA, MLA_ROPE, GQA_KV_HEADS * HEAD_DIM, GQA_KV_HEADS * HEAD_DIM)
    caches = [jnp.zeros((batch, nl, seq, w), jnp.float32) for w in cache_widths]
    for l in range(nl):
        lw = {k: (v, l) for k, v in stacked.items()}
        lw["ones"] = (pw["ones"], None)
        mods_ctx = mods_all[l, 0:1]
        mods_lat = mods_all[l, 1:1 + dbatch]

        qna, kna, vna, qm, ckv, kr, km, vm, qg, kg, vg = _proj(xp, mods_ctx, lw, None, latent=False, seq=seq,
                                                               caches=caches, layer=l)
        caches = [kna, vna, ckv, kr, kg, vg]
        ona, odn = _ctx_attn(qna, kna, vna, qm, km, vm, qg, kg, vg, l, batch=batch, seq=seq)
        xp = _mix_ffn(ona, odn, xp, mods_ctx, lw, seq=seq, seq_per_mod=None)

        qna, kna, vna, qm, km, vm, qg, kg, vg = _proj(xs, mods_lat, lw, rope, latent=True, seq=dseq)
        ona = _na_attn(qna, kna, vna, kc_na, vc_na, na_pairs, l, batch=dbatch, seq=dseq)
        odn = _dense_attn(qm, qg, kc_m, vc_m, kc_g, vc_g, km, vm, kg, vg, l, batch=dbatch, seq=dseq)
        xs = _mix_ffn(ona, odn, xs, mods_lat, lw, seq=dseq, seq_per_mod=dseq)

    heads = lambda a, h: a.reshape(a.shape[:3] + (h, HEAD_DIM))
    return (xp.reshape(batch, seq, d), xs.reshape(dbatch, dseq, d),
            heads(caches[0], NA_HEADS), heads(caches[1], NA_HEADS), caches[2], caches[3],
            heads(caches[4], GQA_KV_HEADS), heads(caches[5], GQA_KV_HEADS))
```

```python
import functools

import numpy as np
import jax
import jax.numpy as jnp
from jax import lax
from jax.experimental import pallas as pl
from jax.experimental.pallas import tpu as pltpu

GRID_W = 64
HEAD_DIM = 64
NA_HEADS = 6
NA_KH = 8
NA_KW = 16
MLA_HEADS = 4
MLA_Q_LORA = 256
MLA_KV_LORA = 128
MLA_NOPE = 64
MLA_ROPE = 32
MLA_V = 64
GQA_HEADS = 6
GQA_KV_HEADS = 2
GQA_GROUP = GQA_HEADS // GQA_KV_HEADS
ROPE_THETA = 10000.0
EPS = 1e-6

NA_WIDTH = NA_HEADS * HEAD_DIM
MLA_WIDTH = MLA_HEADS * MLA_V
GQA_WIDTH = GQA_HEADS * HEAD_DIM
DENSE_WIDTH = MLA_WIDTH + GQA_WIDTH

LANES = 128
VMEM_LIMIT = 52 * 1024 * 1024
MASK_VALUE = -1e30
LOG2_E = 1.4426950408889634

PROJ_TILE = 512
FFN_TILE = 512
FF_CHUNK = 256
DENSE_Q_TILE = 256
CTX_BATCHES = 4
DENSE_K_TILE = 256
NA_ROWS = 4
NA_KEY_ROWS = NA_ROWS + NA_KH
NA_PAD = NA_KEY_ROWS - NA_KH

C_QNA = 0
C_KNA = C_QNA + NA_WIDTH
C_VNA = C_KNA + NA_WIDTH
C_CQ = C_VNA + NA_WIDTH
C_CKV = C_CQ + MLA_Q_LORA
C_QG = C_CKV + MLA_KV_LORA
C_KG = C_QG + GQA_WIDTH
C_VG = C_KG + LANES
C_KR = C_VG + LANES
C_END = C_KR + LANES
PROJ_GROUP = 512

GQA_ORDER = (0, 3, 1, 4, 2, 5)


def _dot(a, b):
    return jnp.dot(a, b, preferred_element_type=jnp.float32)


def _dot_nt(a, b):
    return lax.dot_general(a, b, (((1,), (1,)), ((), ())), preferred_element_type=jnp.float32)


def _params(*sem):
    return pltpu.CompilerParams(dimension_semantics=sem, vmem_limit_bytes=VMEM_LIMIT)


def _rms(x, g):
    return x * lax.rsqrt(jnp.mean(x * x, axis=-1, keepdims=True) + EPS) * g


def _head_mean_sq(x, ones_bf16):
    xx = x * x
    hi = xx.astype(jnp.bfloat16)
    lo = (xx - hi.astype(jnp.float32)).astype(jnp.bfloat16)
    return _dot(hi, ones_bf16) + _dot(lo, ones_bf16)


def _ada_kernel(cond_ref, w_ref, b_ref, o_ref):
    cnd = cond_ref[...]
    s = cnd / (1.0 + jnp.exp(-cnd))
    o_ref[0] = jnp.dot(s, w_ref[0], preferred_element_type=jnp.float32,
                       precision=lax.Precision.HIGHEST) + b_ref[0]


def _ada_mods(cond, w_ada, b_ada):
    nl, d, n6 = w_ada.shape
    tn = 1536
    return pl.pallas_call(
        _ada_kernel,
        grid=(nl, n6 // tn),
        in_specs=[pl.BlockSpec((8, d), lambda l, j: (0, 0)),
                  pl.BlockSpec((1, d, tn), lambda l, j: (l, 0, j)),
                  pl.BlockSpec((1, 1, tn), lambda l, j: (l, 0, j))],
        out_specs=pl.BlockSpec((1, 8, tn), lambda l, j: (l, 0, j)),
        out_shape=jax.ShapeDtypeStruct((nl, 8, n6), jnp.float32),
        compiler_params=_params("arbitrary", "arbitrary"),
        name="ada_mods",
    )(cond, w_ada, b_ada.reshape(nl, 1, n6))


def _swap_halves(x, half):
    lane = lax.broadcasted_iota(jnp.int32, (1, LANES), 1)
    first = jnp.bitwise_and(lane, 2 * half - 1) < half
    blocks = []
    for b in range(x.shape[1] // LANES):
        xb = x[:, b * LANES:(b + 1) * LANES]
        blocks.append(jnp.where(first, pltpu.roll(xb, LANES - half, 1), pltpu.roll(xb, half, 1)))
    return jnp.concatenate(blocks, axis=1)


def _proj_kernel(*refs, latent):
    if latent:
        (x_ref, mod_ref, gpre_ref, w_ref, wt_ref, gq_ref, wuq_ref, gkv_ref, wuk_ref, wuv_ref,
         ones_ref, ggq_ref, ggk_ref, cg_ref, sg_ref, cm_ref, sm_ref,
         qna_ref, kna_ref, vna_ref, qm_ref, km_ref, vm_ref, qg_ref, kg_ref, vg_ref, wbf_ref, wtb_ref) = refs
    else:
        (x_ref, mod_ref, gpre_ref, w_ref, wt_ref, gq_ref, wuq_ref, gkv_ref, wuk_ref, wuv_ref,
         ones_ref, ggq_ref, ggk_ref) = refs[:13]
        (qna_ref, kna_ref, vna_ref, qm_ref, ckv_ref, kr_ref, km_ref, vm_ref, qg_ref, kg_ref, vg_ref, wbf_ref, wtb_ref) = refs[19:]

    @pl.when(pl.program_id(0) == 0)
    def _():
        wbf_ref[...] = w_ref[...].astype(jnp.bfloat16)
        tl = wt_ref[...]
        o_qg = MLA_ROPE
        o_kg = o_qg + GQA_WIDTH
        o_vg = o_kg + GQA_KV_HEADS * HEAD_DIM
        zeros = lambda n: jnp.zeros((tl.shape[0], n), tl.dtype)
        parts = [tl[:, o_qg + HEAD_DIM * h:o_qg + HEAD_DIM * (h + 1)] for h in GQA_ORDER]
        parts += [tl[:, o_kg:o_vg], tl[:, o_vg:o_vg + GQA_KV_HEADS * HEAD_DIM],
                  zeros(MLA_NOPE), tl[:, 0:MLA_ROPE], zeros(LANES - MLA_NOPE - MLA_ROPE)]
        wtb_ref[...] = jnp.concatenate(parts, axis=1).astype(jnp.bfloat16)

    x = x_ref[...]
    mod = mod_ref[0]
    sh, sc = mod[0:1, :], mod[1:2, :]
    h = (_rms(x, gpre_ref[...]) * (1.0 + sc) + sh).astype(jnp.bfloat16)

    groups = [(c0, _dot(h, wbf_ref[:, c0:c0 + PROJ_GROUP])) for c0 in range(0, C_QG, PROJ_GROUP)]
    groups += [(c0, _dot(h, wtb_ref[:, c0 - C_QG:min(c0 + PROJ_GROUP, C_END) - C_QG])) for c0 in range(C_QG, C_END, PROJ_GROUP)]

    def piece(c0, width):
        parts = []
        for b0 in range(c0, c0 + width, LANES):
            g0, y = groups[b0 // PROJ_GROUP]
            parts.append(y[:, b0 - g0:b0 - g0 + LANES])
        return parts[0] if len(parts) == 1 else jnp.concatenate(parts, axis=1)

    na_scale = HEAD_DIM ** -0.5
    dense_unit = LOG2_E if latent else 1.0
    qna_ref[...] = (piece(C_QNA, NA_WIDTH) * na_scale).astype(qna_ref.dtype)
    def put(ref, val):
        ref[...] = val.astype(ref.dtype).reshape(ref.shape)

    put(kna_ref, piece(C_KNA, NA_WIDTH))
    put(vna_ref, piece(C_VNA, NA_WIDTH))

    mla_scale = (MLA_NOPE + MLA_ROPE) ** -0.5
    cqn = _rms(piece(C_CQ, MLA_Q_LORA), gq_ref[...]).astype(jnp.bfloat16)
    qm = _dot(cqn, wuq_ref[...])
    kr = piece(C_KR, LANES)
    if latent:
        cm = jnp.concatenate([cm_ref[...]] * MLA_HEADS, axis=1)
        sm = jnp.concatenate([sm_ref[...]] * MLA_HEADS, axis=1)
        qm = qm * cm + _swap_halves(qm, MLA_ROPE // 4) * sm
        kr = kr * cm_ref[...] + _swap_halves(kr, MLA_ROPE // 4) * sm_ref[...]
    qm_ref[...] = (qm * (mla_scale * dense_unit)).astype(qm_ref.dtype)
    ckv = _rms(piece(C_CKV, MLA_KV_LORA), gkv_ref[...])
    ckv_b = ckv.astype(jnp.bfloat16)
    kn = _dot(ckv_b, wuk_ref[...])
    km_ref[...] = (kn + jnp.concatenate([kr] * MLA_HEADS, axis=1)).astype(km_ref.dtype)
    vm_ref[...] = _dot(ckv_b, wuv_ref[...]).astype(vm_ref.dtype)
    if not latent:
        put(ckv_ref, ckv)
        put(kr_ref, kr[:, MLA_NOPE:MLA_NOPE + MLA_ROPE])

    ones = ones_ref[...]
    qg = piece(C_QG, GQA_WIDTH)
    qq = (qg * qg).astype(jnp.bfloat16)
    ms = jnp.concatenate([_dot(qq[:, b * LANES:(b + 1) * LANES], ones) for b in range(GQA_WIDTH // LANES)], axis=1)
    qg = qg * lax.rsqrt(ms + EPS) * ggq_ref[...]
    kg = piece(C_KG, LANES)
    kg = kg * lax.rsqrt(_head_mean_sq(kg, ones) + EPS) * ggk_ref[...]
    if latent:
        cg, sg = cg_ref[...], sg_ref[...]
        cg3 = jnp.concatenate([cg] * (GQA_WIDTH // LANES), axis=1)
        sg3 = jnp.concatenate([sg] * (GQA_WIDTH // LANES), axis=1)
        qg = qg * cg3 + _swap_halves(qg, HEAD_DIM // 4) * sg3
        kg = kg * cg + _swap_halves(kg, HEAD_DIM // 4) * sg
    qg_ref[...] = (qg * (na_scale * dense_unit)).astype(qg_ref.dtype)
    put(kg_ref, kg)
    put(vg_ref, piece(C_VG, LANES))


def _layer_spec(entry, buffered=False, cols=None):
    a, layer = entry
    kw = dict(pipeline_mode=pl.Buffered(1)) if buffered else {}
    if layer is None:
        return pl.BlockSpec(a.shape, lambda *_: (0,) * a.ndim, **kw)
    shape = a.shape[1:] if cols is None else a.shape[1:-1] + (cols,)
    return pl.BlockSpec((None,) + shape, lambda *_: (layer,) + (0,) * (a.ndim - 1), **kw)


def _proj(x, mods, lw, rope, *, latent, seq, caches=None, layer=None):
    ntok, d = x.shape
    t = PROJ_TILE
    nt = ntok // t
    tiles_per_seq = seq // t if latent else 1
    tok = lambda w: pl.BlockSpec((t, w), lambda i: (i, 0))
    full = lambda e: _layer_spec(e, buffered=True)
    if latent:
        mod_spec = pl.BlockSpec((1, 6, d), lambda i: (i // tiles_per_seq, 0, 0))
    else:
        mod_spec = pl.BlockSpec((1, 6, d), lambda i: (0, 0, 0))
    kv_dt = jnp.bfloat16 if latent else jnp.float32
    bf = jnp.bfloat16
    sds = lambda w, dt: jax.ShapeDtypeStruct((ntok, w), dt)
    common = [x, mods, lw["g_pre_mix"], lw["w_in"], lw["w_in"], lw["mla_g_q"], lw["w_uq"], lw["mla_g_kv"],
              lw["w_uk"], lw["w_uv"], lw["ones"], lw["ggq"], lw["ggk"]]
    in_specs = [tok(d), mod_spec] + [full(a) for a in common[2:]]
    in_specs[3] = _layer_spec(lw["w_in"], buffered=True, cols=C_QG)
    w_arr, w_layer = lw["w_in"]
    assert C_QG % (C_END - C_QG) == 0 and w_arr.shape[-1] <= C_END
    in_specs[4] = pl.BlockSpec((None, d, C_END - C_QG), lambda *_: (w_layer, 0, C_QG // (C_END - C_QG)),
                               pipeline_mode=pl.Buffered(1))
    if latent:
        rope_spec = pl.BlockSpec((t, LANES), lambda i: (i % tiles_per_seq, 0))
        ins = common + [rope["cg"], rope["sg"], rope["cm"], rope["sm"]]
        in_specs = in_specs + [rope_spec] * 4
        outs = [sds(NA_WIDTH, bf), sds(NA_WIDTH, kv_dt), sds(NA_WIDTH, kv_dt), sds(MLA_HEADS * LANES, bf),
                sds(MLA_HEADS * LANES, bf), sds(MLA_WIDTH, bf), sds(GQA_WIDTH, bf), sds(LANES, kv_dt), sds(LANES, kv_dt)]
        aliases = {}
    else:
        ins = common + list(caches)
        in_specs = in_specs + [pl.BlockSpec(memory_space=pl.ANY)] * len(caches)
        cache_sds = [jax.ShapeDtypeStruct(a.shape, a.dtype) for a in caches]
        outs = [sds(NA_WIDTH, bf), cache_sds[0], cache_sds[1], sds(MLA_HEADS * LANES, bf), cache_sds[2], cache_sds[3],
                sds(MLA_HEADS * LANES, bf), sds(MLA_WIDTH, bf), sds(GQA_WIDTH, bf), cache_sds[4], cache_sds[5]]
        cache_out = (1, 2, 4, 5, 9, 10)
        aliases = {len(common) + k: cache_out[k] for k in range(len(caches))}
    spt = t // seq
    cache_spec = lambda o: pl.BlockSpec((spt, None, seq, o.shape[-1]), lambda i: (i, layer, 0, 0))
    out_specs = [cache_spec(o) if len(o.shape) == 4 else tok(o.shape[1]) for o in outs]
    return pl.pallas_call(
        functools.partial(_proj_kernel, latent=latent),
        grid=(nt,),
        in_specs=in_specs,
        out_specs=out_specs,
        out_shape=outs,
        input_output_aliases=aliases,
        scratch_shapes=[pltpu.VMEM((d, C_QG), jnp.bfloat16), pltpu.VMEM((d, C_END - C_QG), jnp.bfloat16)],
        compiler_params=_params("arbitrary"),
        name="proj_lat" if latent else "proj_ctx",
    )(*[a[0] if isinstance(a, tuple) else a for a in ins])


def _lane_lo():
    return lax.broadcasted_iota(jnp.int32, (1, LANES), 1) < HEAD_DIM


def _split_heads(q):
    lo = _lane_lo()
    zero = jnp.zeros_like(q)
    return jnp.where(lo, q, zero), jnp.where(lo, zero, q)


def _softmax_pv(scores, values):
    m = scores[0].max(axis=-1, keepdims=True)
    for s in scores[1:]:
        m = jnp.maximum(m, s.max(axis=-1, keepdims=True))
    l = None
    acc = None
    for s, v in zip(scores, values):
        p = jnp.exp(s - m)
        ps = p.sum(axis=-1, keepdims=True)
        pv = _dot(p.astype(jnp.bfloat16), v)
        l = ps if l is None else l + ps
        acc = pv if acc is None else acc + pv
    return acc / l


def _ctx_attn_kernel(qna_ref, kna_ref, vna_ref, qm_ref, km_ref, vm_ref, qg_ref, kg_ref, vg_ref, ona_ref, odn_ref):
    lo = _lane_lo()
    bf = jnp.bfloat16

    def pair(q_lo, q_hi, k_lo, k_hi, v):
        o_lo = _softmax_pv([_dot_nt(q_lo, k_lo)], [v])
        o_hi = _softmax_pv([_dot_nt(q_hi, k_hi)], [v])
        return jnp.where(lo, o_lo, o_hi)

    for b in range(qna_ref.shape[0]):
        for p in range(NA_WIDTH // LANES):
            cs = slice(p * LANES, (p + 1) * LANES)
            q_lo, q_hi = _split_heads(qna_ref[b, :, cs])
            k = kna_ref[b, :, cs].astype(bf)
            ona_ref[b, :, cs] = pair(q_lo, q_hi, k, k, vna_ref[b, :, cs].astype(bf)).astype(ona_ref.dtype)
        for p in range(MLA_HEADS // 2):
            c0 = 2 * p * LANES
            o = pair(qm_ref[b, :, c0:c0 + LANES], qm_ref[b, :, c0 + LANES:c0 + 2 * LANES],
                     km_ref[b, :, c0:c0 + LANES], km_ref[b, :, c0 + LANES:c0 + 2 * LANES],
                     vm_ref[b, :, p * LANES:(p + 1) * LANES])
            odn_ref[b, :, p * LANES:(p + 1) * LANES] = o.astype(odn_ref.dtype)
        kg = kg_ref[b].astype(bf)
        vg = vg_ref[b].astype(bf)
        for c in range(GQA_WIDTH // LANES):
            q_lo, q_hi = _split_heads(qg_ref[b, :, c * LANES:(c + 1) * LANES])
            o = pair(q_lo, q_hi, kg, kg, vg)
            odn_ref[b, :, MLA_WIDTH + c * LANES:MLA_WIDTH + (c + 1) * LANES] = o.astype(odn_ref.dtype)


def _ctx_attn(qna, kna, vna, qm, km, vm, qg, kg, vg, layer, *, batch, seq):
    nb = CTX_BATCHES
    assert batch % nb == 0
    ins = [a if a.ndim == 4 else a.reshape(batch, seq, a.shape[-1]) for a in (qna, kna, vna, qm, km, vm, qg, kg, vg)]
    spec = lambda a: (pl.BlockSpec((nb, None, seq, a.shape[-1]), lambda b: (b, layer, 0, 0)) if len(a.shape) == 4
                      else pl.BlockSpec((nb, seq, a.shape[-1]), lambda b: (b, 0, 0)))
    outs = [jax.ShapeDtypeStruct((batch, seq, NA_WIDTH), jnp.bfloat16),
            jax.ShapeDtypeStruct((batch, seq, DENSE_WIDTH), jnp.bfloat16)]
    ona, odn = pl.pallas_call(
        _ctx_attn_kernel,
        grid=(batch // nb,),
        in_specs=[spec(a) for a in ins],
        out_specs=[spec(o) for o in outs],
        out_shape=outs,
        compiler_params=_params("arbitrary"),
        name="ctx_attn",
    )(*ins)
    return ona.reshape(batch * seq, NA_WIDTH), odn.reshape(batch * seq, DENSE_WIDTH)


def _na_attn_kernel(q_ref, k_ref, v_ref, kc_ref, vc_ref, pair_ref, o_ref, *, rows):
    lo = _lane_lo()
    blk = pl.program_id(1)
    row0 = blk * NA_ROWS
    key_row0 = jnp.clip(row0 - NA_KH // 2, 0, rows - NA_KEY_ROWS)
    start = pl.multiple_of(key_row0 * GRID_W, GRID_W)
    tq = NA_ROWS * GRID_W
    nkeys = NA_KEY_ROWS * GRID_W

    rq = row0 + lax.shift_right_logical(lax.broadcasted_iota(jnp.int32, (tq, 1), 0), 6)
    rk = key_row0 + lax.shift_right_logical(lax.broadcasted_iota(jnp.int32, (1, nkeys), 1), 6)
    rs = jnp.clip(rq - NA_KH // 2, 0, rows - NA_KH)
    row_mask = jnp.where((rk >= rs) & (rk < rs + NA_KH), 0.0, MASK_VALUE)

    def bias(h):
        base = key_row0 - row0 + (NA_KH - 1) + NA_PAD
        blocks = [jnp.concatenate([pair_ref[0, h, base + 2 * m - dq] for m in range(NA_KEY_ROWS // 2)], axis=1)
                  for dq in range(NA_ROWS)]
        return jnp.concatenate(blocks, axis=0) + row_mask

    for p in range(NA_WIDTH // LANES):
        cs = slice(p * LANES, (p + 1) * LANES)
        q_lo, q_hi = _split_heads(q_ref[0, :, cs])
        k = k_ref[0, pl.ds(start, nkeys), cs]
        v = v_ref[0, pl.ds(start, nkeys), cs]
        kc = kc_ref[0, 0, :, cs]
        vc = vc_ref[0, 0, :, cs]
        o_lo = _softmax_pv([_dot_nt(q_lo, k) + bias(2 * p), _dot_nt(q_lo, kc)], [v, vc])
        o_hi = _softmax_pv([_dot_nt(q_hi, k) + bias(2 * p + 1), _dot_nt(q_hi, kc)], [v, vc])
        o_ref[0, :, cs] = jnp.where(lo, o_lo, o_hi).astype(o_ref.dtype)


def _na_attn(q, k, v, kc, vc, pairs, layer, *, batch, seq):
    rows = seq // GRID_W
    nblk = rows // NA_ROWS
    tq = NA_ROWS * GRID_W
    past = kc.shape[2]
    assert GRID_W == 64 and rows >= NA_KEY_ROWS and rows % NA_ROWS == 0
    q3, k3, v3 = (a.reshape(batch, seq, NA_WIDTH) for a in (q, k, v))
    o = pl.pallas_call(
        functools.partial(_na_attn_kernel, rows=rows),
        grid=(batch, nblk),
        in_specs=[pl.BlockSpec((1, tq, NA_WIDTH), lambda b, i: (b, i, 0)),
                  pl.BlockSpec((1, seq, NA_WIDTH), lambda b, i: (b, 0, 0)),
                  pl.BlockSpec((1, seq, NA_WIDTH), lambda b, i: (b, 0, 0)),
                  pl.BlockSpec((1, 1, past, NA_WIDTH), lambda b, i: (b, layer, 0, 0)),
                  pl.BlockSpec((1, 1, past, NA_WIDTH), lambda b, i: (b, layer, 0, 0)),
                  pl.BlockSpec((1,) + pairs.shape[1:], lambda b, i: (layer, 0, 0, 0, 0))],
        out_specs=pl.BlockSpec((1, tq, NA_WIDTH), lambda b, i: (b, i, 0)),
        out_shape=jax.ShapeDtypeStruct((batch, seq, NA_WIDTH), jnp.bfloat16),
        compiler_params=_params("arbitrary", "arbitrary"),
        name="na_attn",
    )(q3, k3, v3, kc, vc, pairs)
    return o.reshape(batch * seq, NA_WIDTH)


def _na_pair_tables(na_rpb):
    cq = np.arange(GRID_W)[:, None]
    ck = np.arange(GRID_W)[None, :]
    cs = np.clip(cq - NA_KW // 2, 0, GRID_W - NA_KW)
    valid = (ck >= cs) & (ck < cs + NA_KW)
    onehot = ((ck - cq + NA_KW - 1)[None] == np.arange(2 * NA_KW - 1)[:, None, None]) & valid[None]
    t = jnp.einsum("lhdo,oqk->lhdqk", na_rpb, jnp.asarray(onehot, na_rpb.dtype), precision=lax.Precision.HIGHEST)
    t = jnp.where(jnp.asarray(valid), t, MASK_VALUE)
    t = jnp.pad(t, ((0, 0), (0, 0), (NA_PAD, NA_PAD), (0, 0), (0, 0)), constant_values=MASK_VALUE)
    return jnp.concatenate([t[:, :, :-1], t[:, :, 1:]], axis=-1)


def _dense_attn_kernel(qm_ref, qg_ref, kmc_ref, vmc_ref, kgc_ref, vgc_ref, kml_ref, vml_ref, kgl_ref, vgl_ref,
                       o_ref, s_lo_ref, s_hi_ref, *, nk, tk):
    lo = _lane_lo()
    bf = jnp.bfloat16
    past = kmc_ref.shape[2]

    def lane_fold(x, op):
        parts = [x[:, c * LANES:(c + 1) * LANES] for c in range(x.shape[1] // LANES)]
        while len(parts) > 1:
            parts = [op(parts[i], parts[i + 1]) if i + 1 < len(parts) else parts[i] for i in range(0, len(parts), 2)]
        return parts[0]

    def head(q, kc, vc, k_at, v_at, s_ref):
        spans = [(0, past)] + [(past + j * tk, tk) for j in range(nk)]
        mx = None
        for idx, (c0, width) in enumerate(spans):
            s = _dot_nt(q, kc if idx == 0 else k_at(idx - 1))
            s_ref[:, c0:c0 + width] = s
            part = lane_fold(s, jnp.maximum)
            mx = part if mx is None else jnp.maximum(mx, part)
        m = mx.max(axis=-1, keepdims=True)
        lsum = None
        acc = None
        for idx, (c0, width) in enumerate(spans):
            p = jnp.exp2(s_ref[:, c0:c0 + width] - m)
            part = lane_fold(p, jnp.add)
            pv = _dot(p.astype(bf), vc if idx == 0 else v_at(idx - 1))
            lsum = part if lsum is None else lsum + part
            acc = pv if acc is None else acc + pv
        return acc / lsum.sum(axis=-1, keepdims=True)

    def unit(q_lo, q_hi, kc_lo, kc_hi, vc, k_lo_at, k_hi_at, v_at):
        o_lo = head(q_lo, kc_lo, vc, k_lo_at, v_at, s_lo_ref)
        o_hi = head(q_hi, kc_hi, vc, k_hi_at, v_at, s_hi_ref)
        return jnp.where(lo, o_lo, o_hi)

    for p in range(MLA_HEADS // 2):
        c0 = 2 * p * LANES
        c1 = c0 + LANES
        o = unit(qm_ref[0, :, c0:c1], qm_ref[0, :, c1:c1 + LANES],
                 kmc_ref[0, 0, :, c0:c1], kmc_ref[0, 0, :, c1:c1 + LANES], vmc_ref[0, 0, :, p * LANES:(p + 1) * LANES],
                 lambda j, c0=c0, c1=c1: kml_ref[0, j * tk:(j + 1) * tk, c0:c1],
                 lambda j, c1=c1: kml_ref[0, j * tk:(j + 1) * tk, c1:c1 + LANES],
                 lambda j, p=p: vml_ref[0, j * tk:(j + 1) * tk, p * LANES:(p + 1) * LANES])
        o_ref[0, :, p * LANES:(p + 1) * LANES] = o.astype(o_ref.dtype)
    kgc = kgc_ref[0, 0]
    vgc = vgc_ref[0, 0]
    kg_at = lambda j: kgl_ref[0, j * tk:(j + 1) * tk, :]
    vg_at = lambda j: vgl_ref[0, j * tk:(j + 1) * tk, :]
    for c in range(GQA_WIDTH // LANES):
        q_lo, q_hi = _split_heads(qg_ref[0, :, c * LANES:(c + 1) * LANES])
        o = unit(q_lo, q_hi, kgc, kgc, vgc, kg_at, kg_at, vg_at)
        o_ref[0, :, MLA_WIDTH + c * LANES:MLA_WIDTH + (c + 1) * LANES] = o.astype(o_ref.dtype)


def _dense_attn(qm, qg, kmc, vmc, kgc, vgc, kml, vml, kgl, vgl, layer, *, batch, seq):
    tq, tk = DENSE_Q_TILE, DENSE_K_TILE
    r3 = lambda a: a.reshape(batch, seq, a.shape[-1])
    qm, qg, kml, vml, kgl, vgl = (r3(a) for a in (qm, qg, kml, vml, kgl, vgl))
    qspec = lambda a: pl.BlockSpec((1, tq, a.shape[-1]), lambda b, i: (b, i, 0))
    cspec = lambda a: pl.BlockSpec((1, 1) + a.shape[2:], lambda b, i: (b, layer, 0, 0))
    lspec = lambda a: pl.BlockSpec((1, seq, a.shape[-1]), lambda b, i: (b, 0, 0), pipeline_mode=pl.Buffered(1))
    o = pl.pallas_call(
        functools.partial(_dense_attn_kernel, nk=seq // tk, tk=tk),
        grid=(batch, seq // tq),
        in_specs=[qspec(qm), qspec(qg), cspec(kmc), cspec(vmc), cspec(kgc), cspec(vgc),
                  lspec(kml), lspec(vml), lspec(kgl), lspec(vgl)],
        out_specs=pl.BlockSpec((1, tq, DENSE_WIDTH), lambda b, i: (b, i, 0)),
        out_shape=jax.ShapeDtypeStruct((batch, seq, DENSE_WIDTH), jnp.bfloat16),
        scratch_shapes=[pltpu.VMEM((tq, kmc.shape[2] + seq), jnp.float32)] * 2,
        compiler_params=_params("arbitrary", "arbitrary"),
        name="dense_attn",
    )(qm, qg, kmc, vmc, kgc, vgc, kml, vml, kgl, vgl)
    return o.reshape(batch * seq, DENSE_WIDTH)


def _mla_ctx_kernel(ckv_ref, kr_ref, wuk_ref, wuv_ref, k_ref, v_ref):
    ckv = ckv_ref[0, 0]
    kn = _dot(ckv, wuk_ref[0])
    k_ref[0, 0] = (kn + jnp.concatenate([kr_ref[0, 0].astype(jnp.float32)] * MLA_HEADS, axis=1)).astype(k_ref.dtype)
    v_ref[0, 0] = _dot(ckv, wuv_ref[0]).astype(v_ref.dtype)


def _mla_ctx_expand(ckv, kr_pad, wuk, wuv):
    b, nl, past, _ = ckv.shape
    spec = lambda w: pl.BlockSpec((1, 1, past, w), lambda i, l: (i, l, 0, 0))
    wspec = lambda a: pl.BlockSpec((1,) + a.shape[1:], lambda i, l: (l, 0, 0))
    return pl.pallas_call(
        _mla_ctx_kernel,
        grid=(b, nl),
        in_specs=[spec(MLA_KV_LORA), spec(LANES), wspec(wuk), wspec(wuv)],
        out_specs=[spec(MLA_HEADS * LANES), spec(MLA_WIDTH)],
        out_shape=[jax.ShapeDtypeStruct((b, nl, past, MLA_HEADS * LANES), jnp.bfloat16),
                   jax.ShapeDtypeStruct((b, nl, past, MLA_WIDTH), jnp.bfloat16)],
        compiler_params=_params("arbitrary", "arbitrary"),
        name="mla_ctx_expand",
    )(ckv, kr_pad, wuk, wuv)


def _mix_ffn_kernel(ona_ref, odn_ref, x_ref, onap_ref, odnp_ref, xp_ref, onan_ref, odnn_ref, xn_ref, mod_ref,
                    wna_ref, wdn_ref, gpost_ref, gpre_ref, wup_ref, cw_ref, cb_ref, wd_ref, gffn_ref,
                    o_ref, hperm_ref, yperm_ref, *, seq, chunk):
    i = pl.program_id(0)
    t, d = x_ref.shape
    halo = xp_ref.shape[0]
    dff = wd_ref.shape[0]
    groups = t // 8
    mod = mod_ref[0]
    gt_m, sh_f, sc_f, gt_f = mod[2:3, :], mod[3:4, :], mod[4:5, :], mod[5:6, :]

    cat0 = lambda parts: jnp.concatenate(parts, axis=0)
    use_halo = seq > t
    if use_halo:
        ona = cat0([ona_ref[...], onap_ref[...], onan_ref[...]])
        odn = cat0([odn_ref[...], odnp_ref[...], odnn_ref[...]])
        x = cat0([x_ref[...], xp_ref[...], xn_ref[...]])
    else:
        ona, odn, x = ona_ref[...], odn_ref[...], x_ref[...]
    o = _dot(jnp.concatenate([ona, odn], axis=1), jnp.concatenate([wna_ref[...], wdn_ref[...]], axis=0))
    x1 = x + gt_m * _rms(o, gpost_ref[...])
    h2 = _rms(x1, gpre_ref[...]) * (1.0 + sc_f) + sh_f
    o_ref[...] = x1[0:t, :]

    nlb = d // LANES
    for j in range(nlb):
        for k in range(8):
            hperm_ref[j, pl.ds(k, groups, stride=8), :] = h2[k * groups:(k + 1) * groups, j * LANES:(j + 1) * LANES]
    h = jnp.concatenate([hperm_ref[j] for j in range(nlb)], axis=1)
    if use_halo:
        h = cat0([h, h2[t:, :]])
    h = h.astype(jnp.bfloat16)

    kk = lax.broadcasted_iota(jnp.int32, (8, 1), 0)
    tok = i * t + groups * kk
    prev_zero = jnp.bitwise_and(tok, seq - 1) == 0
    next_zero = jnp.bitwise_and(tok + groups, seq - 1) == 0

    def conv(c0):
        w = wup_ref[:, c0:c0 + chunk]
        u_all = _dot(h, w)
        u = u_all[0:t, :]
        b_prev = pltpu.roll(u[t - 8:t, :], 1, 0)
        b_next = pltpu.roll(u[0:8, :], 7, 0)
        if use_halo:
            b_prev = jnp.where(kk == 0, u_all[t + halo - 1:t + halo, :], b_prev)
            b_next = jnp.where(kk == 7, u_all[t + halo:t + halo + 1, :], b_next)
        b_prev = jnp.where(prev_zero, 0.0, b_prev)
        b_next = jnp.where(next_zero, 0.0, b_next)
        prev = jnp.concatenate([b_prev, u[0:t - 8, :]], axis=0)
        nxt = jnp.concatenate([u[8:t, :], b_next], axis=0)
        cw = cw_ref[:, c0:c0 + chunk]
        return prev * cw[0:1, :] + u * cw[1:2, :] + nxt * cw[2:3, :] + cb_ref[:, c0:c0 + chunk]

    acts = []
    for c in range(dff // chunk):
        gate = conv(c * chunk)
        val = conv(dff + c * chunk)
        acts.append((gate / (1.0 + jnp.exp(-gate)) * val).astype(jnp.bfloat16))
    acc = _dot(jnp.concatenate(acts, axis=1), wd_ref[...])

    y = gt_f * _rms(acc, gffn_ref[...])
    for j in range(nlb):
        yperm_ref[j] = y[:, j * LANES:(j + 1) * LANES]
    for k in range(8):
        rows = slice(k * groups, (k + 1) * groups)
        yk = jnp.concatenate([yperm_ref[j, pl.ds(k, groups, stride=8), :] for j in range(nlb)], axis=1)
        o_ref[rows, :] = o_ref[rows, :] + yk


def _mix_ffn(ona, odn, x, mods, lw, *, seq, seq_per_mod):
    ntok, d = x.shape
    t = FFN_TILE
    halo = 16
    assert seq & (seq - 1) == 0 and (t % seq == 0 or seq % t == 0) and ntok % t == 0 and t % 64 == 0
    nhalo = ntok // halo
    if seq_per_mod is None:
        mod_spec = pl.BlockSpec((1, 6, d), lambda i: (0, 0, 0))
    else:
        tps = seq_per_mod // t
        mod_spec = pl.BlockSpec((1, 6, d), lambda i: (i // tps, 0, 0))
    tok = lambda w: pl.BlockSpec((t, w), lambda i: (i, 0))
    prv = lambda w: pl.BlockSpec((halo, w), lambda i: (jnp.maximum(i * (t // halo) - 1, 0), 0))
    nxt = lambda w: pl.BlockSpec((halo, w), lambda i: (jnp.minimum((i + 1) * (t // halo), nhalo - 1), 0))
    full = lambda e: _layer_spec(e, buffered=True)
    weights = [lw["w_out_na"], lw["w_out_dn"], lw["g_post_mix"], lw["g_pre_ffn"], lw["w_up"], lw["conv_w"],
               lw["conv_b"], lw["w_down"], lw["g_post_ffn"]]
    widths = (NA_WIDTH, DENSE_WIDTH, d)
    return pl.pallas_call(
        functools.partial(_mix_ffn_kernel, seq=seq, chunk=FF_CHUNK),
        grid=(ntok // t,),
        in_specs=[tok(w) for w in widths] + [prv(w) for w in widths] + [nxt(w) for w in widths] + [mod_spec]
                 + [full(a) for a in weights],
        out_specs=tok(d),
        out_shape=jax.ShapeDtypeStruct((ntok, d), jnp.float32),
        scratch_shapes=[pltpu.VMEM((d // LANES, t, LANES), jnp.float32)] * 2,
        compiler_params=_params("arbitrary"),
        name="mix_ffn",
    )(ona, odn, x, ona, odn, x, ona, odn, x, mods, *[w[0] for w in weights])


def _prep_weights(w_in, mla_w_uq, mla_w_ukv, w_out, gqa_g_q, gqa_g_k):
    nl, d, _ = w_in.shape
    bf = jnp.bfloat16
    cat = lambda parts: jnp.concatenate(parts, axis=-1)
    pad_r = LANES - MLA_NOPE - MLA_ROPE
    z = lambda rows, n: jnp.zeros((nl, rows, n), bf)
    assert 3 * NA_WIDTH + MLA_Q_LORA + MLA_KV_LORA == C_QG and C_QG % PROJ_GROUP == 0

    uq = mla_w_uq.astype(bf)
    ukv = mla_w_ukv.astype(bf)
    ql, kl = uq.shape[1], ukv.shape[1]
    qw = MLA_NOPE + MLA_ROPE
    kvw = MLA_NOPE + MLA_V
    w_uq = cat([p for h in range(MLA_HEADS) for p in (uq[..., qw * h:qw * (h + 1)], z(ql, pad_r))])
    w_uk = cat([p for h in range(MLA_HEADS) for p in (ukv[..., kvw * h:kvw * h + MLA_NOPE], z(kl, LANES - MLA_NOPE))])
    w_uv = cat([ukv[..., kvw * h + MLA_NOPE:kvw * (h + 1)] for h in range(MLA_HEADS)])

    wo = w_out.astype(bf)
    o_gqa = NA_WIDTH + MLA_WIDTH
    w_out_dn = jnp.concatenate([wo[:, NA_WIDTH:o_gqa]]
                               + [wo[:, o_gqa + HEAD_DIM * h:o_gqa + HEAD_DIM * (h + 1)] for h in GQA_ORDER], axis=1)

    ggq = jnp.tile(gqa_g_q, (1, GQA_HEADS))[:, None, :]
    ggk = jnp.tile(gqa_g_k, (1, GQA_KV_HEADS))[:, None, :]
    ones = jnp.asarray(np.kron(np.eye(LANES // HEAD_DIM), np.full((HEAD_DIM, HEAD_DIM), 1.0 / HEAD_DIM)), bf)
    return dict(w_uq=w_uq, w_uk=w_uk, w_uv=w_uv, w_out_na=wo[:, :NA_WIDTH], w_out_dn=w_out_dn,
                ggq=ggq, ggk=ggk, ones=ones)


def _rope_tables(seq):
    f32 = np.float32
    t = np.arange(seq)
    r, c = (t // GRID_W).astype(f32), (t % GRID_W).astype(f32)

    def tables(dim):
        quarter = dim // 4
        freqs = f32(ROPE_THETA) ** (-np.arange(quarter, dtype=f32) / f32(quarter))
        ar_, ac_ = r[:, None] * freqs[None, :], c[:, None] * freqs[None, :]
        cos = np.concatenate([np.cos(ar_), np.cos(ar_), np.cos(ac_), np.cos(ac_)], axis=1)
        sin = np.concatenate([-np.sin(ar_), np.sin(ar_), -np.sin(ac_), np.sin(ac_)], axis=1)
        return cos.astype(f32), sin.astype(f32)

    c64, s64 = tables(HEAD_DIM)
    c32, s32 = tables(MLA_ROPE)
    pad = LANES - MLA_NOPE - MLA_ROPE
    cm = np.concatenate([np.ones((seq, MLA_NOPE), f32), c32, np.zeros((seq, pad), f32)], axis=1)
    sm = np.concatenate([np.zeros((seq, MLA_NOPE), f32), s32, np.zeros((seq, pad), f32)], axis=1)
    return dict(cg=jnp.asarray(np.tile(c64, (1, 2))), sg=jnp.asarray(np.tile(s64, (1, 2))),
                cm=jnp.asarray(cm), sm=jnp.asarray(sm))


def kernel(x_prompt, x_sample, c, cache_na_k, cache_na_v, cache_mla_ckv, cache_mla_krope, cache_gqa_k, cache_gqa_v, c_ctx, w_ada, b_ada, g_pre_mix, g_post_mix, g_pre_ffn, g_post_ffn, w_in, na_rpb, mla_g_q, mla_w_uq, mla_g_kv, mla_w_ukv, gqa_g_q, gqa_g_k, w_out, ffn_w_up, ffn_conv_w, ffn_conv_b, ffn_w_down):
    batch, seq, d = x_prompt.shape
    dbatch, dseq, _ = x_sample.shape
    nl = w_ada.shape[0]
    past = cache_na_k.shape[2]
    bf = jnp.bfloat16
    assert dbatch + 1 <= 8 and dseq % GRID_W == 0

    cond = jnp.zeros((8, d), jnp.float32).at[0].set(c_ctx).at[1:1 + dbatch].set(c)
    mods_all = _ada_mods(cond, w_ada, b_ada).reshape(nl, 8, 6, d)

    pw = _prep_weights(w_in, mla_w_uq, mla_w_ukv, w_out, gqa_g_q, gqa_g_k)
    rope = _rope_tables(dseq)
    vec = lambda g: g[:, None, :]
    stacked = dict(
        g_pre_mix=vec(g_pre_mix), g_post_mix=vec(g_post_mix), g_pre_ffn=vec(g_pre_ffn), g_post_ffn=vec(g_post_ffn),
        mla_g_q=vec(mla_g_q), mla_g_kv=vec(mla_g_kv), w_in=w_in,
        w_uq=pw["w_uq"], w_uk=pw["w_uk"], w_uv=pw["w_uv"], ggq=pw["ggq"], ggk=pw["ggk"],
        w_out_na=pw["w_out_na"], w_out_dn=pw["w_out_dn"],
        w_up=ffn_w_up.astype(bf), w_down=ffn_w_down.astype(bf), conv_w=ffn_conv_w, conv_b=vec(ffn_conv_b))

    kc_na = cache_na_k.reshape(dbatch, nl, past, NA_WIDTH).astype(bf)
    vc_na = cache_na_v.reshape(dbatch, nl, past, NA_WIDTH).astype(bf)
    kc_g = cache_gqa_k.reshape(dbatch, nl, past, GQA_KV_HEADS * HEAD_DIM).astype(bf)
    vc_g = cache_gqa_v.reshape(dbatch, nl, past, GQA_KV_HEADS * HEAD_DIM).astype(bf)
    kr_pad = jnp.pad(cache_mla_krope, ((0, 0), (0, 0), (0, 0), (MLA_NOPE, LANES - MLA_NOPE - MLA_ROPE))).astype(bf)
    kc_m, vc_m = _mla_ctx_expand(cache_mla_ckv.astype(bf), kr_pad, pw["w_uk"], pw["w_uv"])

    na_pairs = _na_pair_tables(na_rpb)

    xp = x_prompt.reshape(batch * seq, d)
    xs = x_sample.reshape(dbatch * dseq, d)
    cache_widths = (NA_WIDTH, NA_WIDTH, MLA_KV_LORA, MLA_ROPE, GQA_KV_HEADS * HEAD_DIM, GQA_KV_HEADS * HEAD_DIM)
    caches = [jnp.zeros((batch, nl, seq, w), jnp.float32) for w in cache_widths]
    for l in range(nl):
        lw = {k: (v, l) for k, v in stacked.items()}
        lw["ones"] = (pw["ones"], None)
        mods_ctx = mods_all[l, 0:1]
        mods_lat = mods_all[l, 1:1 + dbatch]

        qna, kna, vna, qm, ckv, kr, km, vm, qg, kg, vg = _proj(xp, mods_ctx, lw, None, latent=False, seq=seq,
                                                               caches=caches, layer=l)
        caches = [kna, vna, ckv, kr, kg, vg]
        ona, odn = _ctx_attn(qna, kna, vna, qm, km, vm, qg, kg, vg, l, batch=batch, seq=seq)
        xp = _mix_ffn(ona, odn, xp, mods_ctx, lw, seq=seq, seq_per_mod=None)

        qna, kna, vna, qm, km, vm, qg, kg, vg = _proj(xs, mods_lat, lw, rope, latent=True, seq=dseq)
        ona = _na_attn(qna, kna, vna, kc_na, vc_na, na_pairs, l, batch=dbatch, seq=dseq)
        odn = _dense_attn(qm, qg, kc_m, vc_m, kc_g, vc_g, km, vm, kg, vg, l, batch=dbatch, seq=dseq)
        xs = _mix_ffn(ona, odn, xs, mods_lat, lw, seq=dseq, seq_per_mod=dseq)

    heads = lambda a, h: a.reshape(a.shape[:3] + (h, HEAD_DIM))
    return (xp.reshape(batch, seq, d), xs.reshape(dbatch, dseq, d),
            heads(caches[0], NA_HEADS), heads(caches[1], NA_HEADS), caches[2], caches[3],
            heads(caches[4], GQA_KV_HEADS), heads(caches[5], GQA_KV_HEADS))
```

```python
import functools

import numpy as np
import jax
import jax.numpy as jnp
from jax import lax
from jax.experimental import pallas as pl
from jax.experimental.pallas import tpu as pltpu

GRID_W = 64
HEAD_DIM = 64
NA_HEADS = 6
NA_KH = 8
NA_KW = 16
MLA_HEADS = 4
MLA_Q_LORA = 256
MLA_KV_LORA = 128
MLA_NOPE = 64
MLA_ROPE = 32
MLA_V = 64
GQA_HEADS = 6
GQA_KV_HEADS = 2
GQA_GROUP = GQA_HEADS // GQA_KV_HEADS
ROPE_THETA = 10000.0
EPS = 1e-6

NA_WIDTH = NA_HEADS * HEAD_DIM
MLA_WIDTH = MLA_HEADS * MLA_V
GQA_WIDTH = GQA_HEADS * HEAD_DIM
DENSE_WIDTH = MLA_WIDTH + GQA_WIDTH

LANES = 128
VMEM_LIMIT = 52 * 1024 * 1024
MASK_VALUE = -1e30
LOG2_E = 1.4426950408889634

PROJ_TILE = 512
FFN_TILE = 512
FF_CHUNK = 256
DENSE_Q_TILE = 256
DENSE_K_TILE = 256
NA_ROWS = 4
NA_KEY_ROWS = NA_ROWS + NA_KH
NA_PAD = NA_KEY_ROWS - NA_KH

C_QNA = 0
C_KNA = C_QNA + NA_WIDTH
C_VNA = C_KNA + NA_WIDTH
C_CQ = C_VNA + NA_WIDTH
C_CKV = C_CQ + MLA_Q_LORA
C_QG = C_CKV + MLA_KV_LORA
C_KG = C_QG + GQA_WIDTH
C_VG = C_KG + LANES
C_KR = C_VG + LANES
C_END = C_KR + LANES
PROJ_GROUP = 512

GQA_ORDER = (0, 3, 1, 4, 2, 5)


def _dot(a, b):
    return jnp.dot(a, b, preferred_element_type=jnp.float32)


def _dot_nt(a, b):
    return lax.dot_general(a, b, (((1,), (1,)), ((), ())), preferred_element_type=jnp.float32)


def _params(*sem):
    return pltpu.CompilerParams(dimension_semantics=sem, vmem_limit_bytes=VMEM_LIMIT)


def _rms(x, g):
    return x * lax.rsqrt(jnp.mean(x * x, axis=-1, keepdims=True) + EPS) * g


def _head_mean_sq(x, ones_bf16):
    xx = x * x
    hi = xx.astype(jnp.bfloat16)
    lo = (xx - hi.astype(jnp.float32)).astype(jnp.bfloat16)
    return _dot(hi, ones_bf16) + _dot(lo, ones_bf16)


def _ada_kernel(cond_ref, w_ref, b_ref, o_ref):
    cnd = cond_ref[...]
    s = cnd / (1.0 + jnp.exp(-cnd))
    o_ref[0] = jnp.dot(s, w_ref[0], preferred_element_type=jnp.float32,
                       precision=lax.Precision.HIGHEST) + b_ref[0]


def _ada_mods(cond, w_ada, b_ada):
    nl, d, n6 = w_ada.shape
    tn = 1536
    return pl.pallas_call(
        _ada_kernel,
        grid=(nl, n6 // tn),
        in_specs=[pl.BlockSpec((8, d), lambda l, j: (0, 0)),
                  pl.BlockSpec((1, d, tn), lambda l, j: (l, 0, j)),
                  pl.BlockSpec((1, 1, tn), lambda l, j: (l, 0, j))],
        out_specs=pl.BlockSpec((1, 8, tn), lambda l, j: (l, 0, j)),
        out_shape=jax.ShapeDtypeStruct((nl, 8, n6), jnp.float32),
        compiler_params=_params("arbitrary", "arbitrary"),
        name="ada_mods",
    )(cond, w_ada, b_ada.reshape(nl, 1, n6))


def _swap_halves(x, half):
    lane = lax.broadcasted_iota(jnp.int32, (1, LANES), 1)
    first = jnp.bitwise_and(lane, 2 * half - 1) < half
    blocks = []
    for b in range(x.shape[1] // LANES):
        xb = x[:, b * LANES:(b + 1) * LANES]
        blocks.append(jnp.where(first, pltpu.roll(xb, LANES - half, 1), pltpu.roll(xb, half, 1)))
    return jnp.concatenate(blocks, axis=1)


def _proj_kernel(*refs, latent):
    if latent:
        (x_ref, mod_ref, gpre_ref, w_ref, wt_ref, gq_ref, wuq_ref, gkv_ref, wuk_ref, wuv_ref,
         ones_ref, ggq_ref, ggk_ref, cg_ref, sg_ref, cm_ref, sm_ref,
         qna_ref, kna_ref, vna_ref, qm_ref, km_ref, vm_ref, qg_ref, kg_ref, vg_ref, wbf_ref, wtb_ref) = refs
    else:
        (x_ref, mod_ref, gpre_ref, w_ref, wt_ref, gq_ref, wuq_ref, gkv_ref, wuk_ref, wuv_ref,
         ones_ref, ggq_ref, ggk_ref) = refs[:13]
        (ona_ref, kna_ref, vna_ref, odn_ref, ckv_ref, kr_ref, kg_ref, vg_ref,
         qna_ref, qm_ref, km_ref, vm_ref, qg_ref, wbf_ref, wtb_ref) = refs[19:]

    @pl.when(pl.program_id(0) == 0)
    def _():
        wbf_ref[...] = w_ref[...].astype(jnp.bfloat16)
        tl = wt_ref[...]
        o_qg = MLA_ROPE
        o_kg = o_qg + GQA_WIDTH
        o_vg = o_kg + GQA_KV_HEADS * HEAD_DIM
        zeros = lambda n: jnp.zeros((tl.shape[0], n), tl.dtype)
        parts = [tl[:, o_qg + HEAD_DIM * h:o_qg + HEAD_DIM * (h + 1)] for h in GQA_ORDER]
        parts += [tl[:, o_kg:o_vg], tl[:, o_vg:o_vg + GQA_KV_HEADS * HEAD_DIM],
                  zeros(MLA_NOPE), tl[:, 0:MLA_ROPE], zeros(LANES - MLA_NOPE - MLA_ROPE)]
        wtb_ref[...] = jnp.concatenate(parts, axis=1).astype(jnp.bfloat16)

    x = x_ref[...]
    mod = mod_ref[0]
    sh, sc = mod[0:1, :], mod[1:2, :]
    h = (_rms(x, gpre_ref[...]) * (1.0 + sc) + sh).astype(jnp.bfloat16)

    groups = [(c0, _dot(h, wbf_ref[:, c0:c0 + PROJ_GROUP])) for c0 in range(0, C_QG, PROJ_GROUP)]
    groups += [(c0, _dot(h, wtb_ref[:, c0 - C_QG:min(c0 + PROJ_GROUP, C_END) - C_QG])) for c0 in range(C_QG, C_END, PROJ_GROUP)]

    def piece(c0, width):
        parts = []
        for b0 in range(c0, c0 + width, LANES):
            g0, y = groups[b0 // PROJ_GROUP]
            parts.append(y[:, b0 - g0:b0 - g0 + LANES])
        return parts[0] if len(parts) == 1 else jnp.concatenate(parts, axis=1)

    na_scale = HEAD_DIM ** -0.5
    dense_unit = LOG2_E if latent else 1.0
    qna_ref[...] = (piece(C_QNA, NA_WIDTH) * na_scale).astype(qna_ref.dtype)
    def put(ref, val):
        ref[...] = val.astype(ref.dtype).reshape(ref.shape)

    put(kna_ref, piece(C_KNA, NA_WIDTH))
    put(vna_ref, piece(C_VNA, NA_WIDTH))

    mla_scale = (MLA_NOPE + MLA_ROPE) ** -0.5
    cqn = _rms(piece(C_CQ, MLA_Q_LORA), gq_ref[...]).astype(jnp.bfloat16)
    qm = _dot(cqn, wuq_ref[...])
    kr = piece(C_KR, LANES)
    if latent:
        cm = jnp.concatenate([cm_ref[...]] * MLA_HEADS, axis=1)
        sm = jnp.concatenate([sm_ref[...]] * MLA_HEADS, axis=1)
        qm = qm * cm + _swap_halves(qm, MLA_ROPE // 4) * sm
        kr = kr * cm_ref[...] + _swap_halves(kr, MLA_ROPE // 4) * sm_ref[...]
    qm_ref[...] = (qm * (mla_scale * dense_unit)).astype(qm_ref.dtype)
    ckv = _rms(piece(C_CKV, MLA_KV_LORA), gkv_ref[...])
    ckv_b = ckv.astype(jnp.bfloat16)
    kn = _dot(ckv_b, wuk_ref[...])
    km_ref[...] = (kn + jnp.concatenate([kr] * MLA_HEADS, axis=1)).astype(km_ref.dtype)
    vm_ref[...] = _dot(ckv_b, wuv_ref[...]).astype(vm_ref.dtype)
    if not latent:
        put(ckv_ref, ckv)
        put(kr_ref, kr[:, MLA_NOPE:MLA_NOPE + MLA_ROPE])

    ones = ones_ref[...]
    qg = piece(C_QG, GQA_WIDTH)
    qq = (qg * qg).astype(jnp.bfloat16)
    ms = jnp.concatenate([_dot(qq[:, b * LANES:(b + 1) * LANES], ones) for b in range(GQA_WIDTH // LANES)], axis=1)
    qg = qg * lax.rsqrt(ms + EPS) * ggq_ref[...]
    kg = piece(C_KG, LANES)
    kg = kg * lax.rsqrt(_head_mean_sq(kg, ones) + EPS) * ggk_ref[...]
    if latent:
        cg, sg = cg_ref[...], sg_ref[...]
        cg3 = jnp.concatenate([cg] * (GQA_WIDTH // LANES), axis=1)
        sg3 = jnp.concatenate([sg] * (GQA_WIDTH // LANES), axis=1)
        qg = qg * cg3 + _swap_halves(qg, HEAD_DIM // 4) * sg3
        kg = kg * cg + _swap_halves(kg, HEAD_DIM // 4) * sg
    qg_ref[...] = (qg * (na_scale * dense_unit)).astype(qg_ref.dtype)
    put(kg_ref, kg)
    put(vg_ref, piece(C_VG, LANES))
    if not latent:
        seq = kna_ref.shape[1]
        for b in range(kna_ref.shape[0]):
            rows = slice(b * seq, (b + 1) * seq)
            _ctx_attend(lambda cs: qna_ref[rows, cs], lambda cs: kna_ref[b, :, cs], lambda cs: vna_ref[b, :, cs],
                        lambda cs: qm_ref[rows, cs], lambda cs: km_ref[rows, cs], lambda cs: vm_ref[rows, cs],
                        lambda cs: qg_ref[rows, cs], kg_ref[b], vg_ref[b],
                        lambda cs, o: ona_ref.__setitem__((rows, cs), o.astype(ona_ref.dtype)),
                        lambda cs, o: odn_ref.__setitem__((rows, cs), o.astype(odn_ref.dtype)))


def _layer_spec(entry, buffered=False, cols=None):
    a, layer = entry
    kw = dict(pipeline_mode=pl.Buffered(1)) if buffered else {}
    if layer is None:
        return pl.BlockSpec(a.shape, lambda *_: (0,) * a.ndim, **kw)
    shape = a.shape[1:] if cols is None else a.shape[1:-1] + (cols,)
    return pl.BlockSpec((None,) + shape, lambda *_: (layer,) + (0,) * (a.ndim - 1), **kw)


def _proj(x, mods, lw, rope, *, latent, seq, caches=None, layer=None):
    ntok, d = x.shape
    t = PROJ_TILE
    nt = ntok // t
    tiles_per_seq = seq // t if latent else 1
    tok = lambda w: pl.BlockSpec((t, w), lambda i: (i, 0))
    full = lambda e: _layer_spec(e, buffered=True)
    if latent:
        mod_spec = pl.BlockSpec((1, 6, d), lambda i: (i // tiles_per_seq, 0, 0))
    else:
        mod_spec = pl.BlockSpec((1, 6, d), lambda i: (0, 0, 0))
    kv_dt = jnp.bfloat16 if latent else jnp.float32
    bf = jnp.bfloat16
    sds = lambda w, dt: jax.ShapeDtypeStruct((ntok, w), dt)
    common = [x, mods, lw["g_pre_mix"], lw["w_in"], lw["w_in"], lw["mla_g_q"], lw["w_uq"], lw["mla_g_kv"],
              lw["w_uk"], lw["w_uv"], lw["ones"], lw["ggq"], lw["ggk"]]
    in_specs = [tok(d), mod_spec] + [full(a) for a in common[2:]]
    in_specs[3] = _layer_spec(lw["w_in"], buffered=True, cols=C_QG)
    w_arr, w_layer = lw["w_in"]
    assert C_QG % (C_END - C_QG) == 0 and w_arr.shape[-1] <= C_END
    in_specs[4] = pl.BlockSpec((None, d, C_END - C_QG), lambda *_: (w_layer, 0, C_QG // (C_END - C_QG)),
                               pipeline_mode=pl.Buffered(1))
    if latent:
        rope_spec = pl.BlockSpec((t, LANES), lambda i: (i % tiles_per_seq, 0))
        ins = common + [rope["cg"], rope["sg"], rope["cm"], rope["sm"]]
        in_specs = in_specs + [rope_spec] * 4
        outs = [sds(NA_WIDTH, bf), sds(NA_WIDTH, kv_dt), sds(NA_WIDTH, kv_dt), sds(MLA_HEADS * LANES, bf),
                sds(MLA_HEADS * LANES, bf), sds(MLA_WIDTH, bf), sds(GQA_WIDTH, bf), sds(LANES, kv_dt), sds(LANES, kv_dt)]
        aliases = {}
        scratch = []
    else:
        ins = common + list(caches)
        in_specs = in_specs + [pl.BlockSpec(memory_space=pl.ANY)] * len(caches)
        cache_sds = [jax.ShapeDtypeStruct(a.shape, a.dtype) for a in caches]
        outs = [sds(NA_WIDTH, bf), cache_sds[0], cache_sds[1], sds(DENSE_WIDTH, bf), cache_sds[2], cache_sds[3],
                cache_sds[4], cache_sds[5]]
        cache_out = (1, 2, 4, 5, 6, 7)
        aliases = {len(common) + k: cache_out[k] for k in range(len(caches))}
        scratch = [pltpu.VMEM((t, w), bf) for w in (NA_WIDTH, MLA_HEADS * LANES, MLA_HEADS * LANES, MLA_WIDTH, GQA_WIDTH)]
    spt = t // seq
    cache_spec = lambda o: pl.BlockSpec((spt, None, seq, o.shape[-1]), lambda i: (i, layer, 0, 0))
    out_specs = [cache_spec(o) if len(o.shape) == 4 else tok(o.shape[1]) for o in outs]
    return pl.pallas_call(
        functools.partial(_proj_kernel, latent=latent),
        grid=(nt,),
        in_specs=in_specs,
        out_specs=out_specs,
        out_shape=outs,
        input_output_aliases=aliases,
        scratch_shapes=scratch + [pltpu.VMEM((d, C_QG), jnp.bfloat16), pltpu.VMEM((d, C_END - C_QG), jnp.bfloat16)],
        compiler_params=_params("arbitrary"),
        name="proj_lat" if latent else "proj_ctx",
    )(*[a[0] if isinstance(a, tuple) else a for a in ins])


def _lane_lo():
    return lax.broadcasted_iota(jnp.int32, (1, LANES), 1) < HEAD_DIM


def _split_heads(q):
    lo = _lane_lo()
    zero = jnp.zeros_like(q)
    return jnp.where(lo, q, zero), jnp.where(lo, zero, q)


def _softmax_pv(scores, values):
    m = scores[0].max(axis=-1, keepdims=True)
    for s in scores[1:]:
        m = jnp.maximum(m, s.max(axis=-1, keepdims=True))
    l = None
    acc = None
    for s, v in zip(scores, values):
        p = jnp.exp(s - m)
        ps = p.sum(axis=-1, keepdims=True)
        pv = _dot(p.astype(jnp.bfloat16), v)
        l = ps if l is None else l + ps
        acc = pv if acc is None else acc + pv
    return acc / l


def _ctx_attend(qna, kna, vna, qm, km, vm, qg, kg, vg, put_na, put_dn):
    lo = _lane_lo()
    bf = jnp.bfloat16

    def pair(q_lo, q_hi, k_lo, k_hi, v):
        o_lo = _softmax_pv([_dot_nt(q_lo, k_lo)], [v])
        o_hi = _softmax_pv([_dot_nt(q_hi, k_hi)], [v])
        return jnp.where(lo, o_lo, o_hi)

    for p in range(NA_WIDTH // LANES):
        cs = slice(p * LANES, (p + 1) * LANES)
        q_lo, q_hi = _split_heads(qna(cs))
        k = kna(cs).astype(bf)
        put_na(cs, pair(q_lo, q_hi, k, k, vna(cs).astype(bf)))
    for p in range(MLA_HEADS // 2):
        c0, c1 = slice(2 * p * LANES, (2 * p + 1) * LANES), slice((2 * p + 1) * LANES, (2 * p + 2) * LANES)
        put_dn(slice(p * LANES, (p + 1) * LANES), pair(qm(c0), qm(c1), km(c0), km(c1), vm(slice(p * LANES, (p + 1) * LANES))))
    kg = kg.astype(bf)
    vg = vg.astype(bf)
    for c in range(GQA_WIDTH // LANES):
        q_lo, q_hi = _split_heads(qg(slice(c * LANES, (c + 1) * LANES)))
        put_dn(slice(MLA_WIDTH + c * LANES, MLA_WIDTH + (c + 1) * LANES), pair(q_lo, q_hi, kg, kg, vg))


def _na_attn_kernel(q_ref, k_ref, v_ref, kc_ref, vc_ref, pair_ref, o_ref, *, rows):
    lo = _lane_lo()
    blk = pl.program_id(1)
    row0 = blk * NA_ROWS
    key_row0 = jnp.clip(row0 - NA_KH // 2, 0, rows - NA_KEY_ROWS)
    start = pl.multiple_of(key_row0 * GRID_W, GRID_W)
    tq = NA_ROWS * GRID_W
    nkeys = NA_KEY_ROWS * GRID_W

    rq = row0 + lax.shift_right_logical(lax.broadcasted_iota(jnp.int32, (tq, 1), 0), 6)
    rk = key_row0 + lax.shift_right_logical(lax.broadcasted_iota(jnp.int32, (1, nkeys), 1), 6)
    rs = jnp.clip(rq - NA_KH // 2, 0, rows - NA_KH)
    row_mask = jnp.where((rk >= rs) & (rk < rs + NA_KH), 0.0, MASK_VALUE)

    def bias(h):
        base = key_row0 - row0 + (NA_KH - 1) + NA_PAD
        blocks = [jnp.concatenate([pair_ref[0, h, base + 2 * m - dq] for m in range(NA_KEY_ROWS // 2)], axis=1)
                  for dq in range(NA_ROWS)]
        return jnp.concatenate(blocks, axis=0) + row_mask

    for p in range(NA_WIDTH // LANES):
        cs = slice(p * LANES, (p + 1) * LANES)
        q_lo, q_hi = _split_heads(q_ref[0, :, cs])
        k = k_ref[0, pl.ds(start, nkeys), cs]
        v = v_ref[0, pl.ds(start, nkeys), cs]
        kc = kc_ref[0, 0, :, cs]
        vc = vc_ref[0, 0, :, cs]
        o_lo = _softmax_pv([_dot_nt(q_lo, k) + bias(2 * p), _dot_nt(q_lo, kc)], [v, vc])
        o_hi = _softmax_pv([_dot_nt(q_hi, k) + bias(2 * p + 1), _dot_nt(q_hi, kc)], [v, vc])
        o_ref[0, :, cs] = jnp.where(lo, o_lo, o_hi).astype(o_ref.dtype)


def _na_attn(q, k, v, kc, vc, pairs, layer, *, batch, seq):
    rows = seq // GRID_W
    nblk = rows // NA_ROWS
    tq = NA_ROWS * GRID_W
    past = kc.shape[2]
    assert GRID_W == 64 and rows >= NA_KEY_ROWS and rows % NA_ROWS == 0
    q3, k3, v3 = (a.reshape(batch, seq, NA_WIDTH) for a in (q, k, v))
    o = pl.pallas_call(
        functools.partial(_na_attn_kernel, rows=rows),
        grid=(batch, nblk),
        in_specs=[pl.BlockSpec((1, tq, NA_WIDTH), lambda b, i: (b, i, 0)),
                  pl.BlockSpec((1, seq, NA_WIDTH), lambda b, i: (b, 0, 0)),
                  pl.BlockSpec((1, seq, NA_WIDTH), lambda b, i: (b, 0, 0)),
                  pl.BlockSpec((1, 1, past, NA_WIDTH), lambda b, i: (b, layer, 0, 0)),
                  pl.BlockSpec((1, 1, past, NA_WIDTH), lambda b, i: (b, layer, 0, 0)),
                  pl.BlockSpec((1,) + pairs.shape[1:], lambda b, i: (layer, 0, 0, 0, 0))],
        out_specs=pl.BlockSpec((1, tq, NA_WIDTH), lambda b, i: (b, i, 0)),
        out_shape=jax.ShapeDtypeStruct((batch, seq, NA_WIDTH), jnp.bfloat16),
        compiler_params=_params("arbitrary", "arbitrary"),
        name="na_attn",
    )(q3, k3, v3, kc, vc, pairs)
    return o.reshape(batch * seq, NA_WIDTH)


def _na_pair_tables(na_rpb):
    cq = np.arange(GRID_W)[:, None]
    ck = np.arange(GRID_W)[None, :]
    cs = np.clip(cq - NA_KW // 2, 0, GRID_W - NA_KW)
    valid = (ck >= cs) & (ck < cs + NA_KW)
    onehot = ((ck - cq + NA_KW - 1)[None] == np.arange(2 * NA_KW - 1)[:, None, None]) & valid[None]
    t = jnp.einsum("lhdo,oqk->lhdqk", na_rpb, jnp.asarray(onehot, na_rpb.dtype), precision=lax.Precision.HIGHEST)
    t = jnp.where(jnp.asarray(valid), t, MASK_VALUE)
    t = jnp.pad(t, ((0, 0), (0, 0), (NA_PAD, NA_PAD), (0, 0), (0, 0)), constant_values=MASK_VALUE)
    return jnp.concatenate([t[:, :, :-1], t[:, :, 1:]], axis=-1)


def _dense_attn_kernel(qm_ref, qg_ref, kmc_ref, vmc_ref, kgc_ref, vgc_ref, kml_ref, vml_ref, kgl_ref, vgl_ref,
                       o_ref, s_lo_ref, s_hi_ref, *, nk, tk):
    lo = _lane_lo()
    bf = jnp.bfloat16
    past = kmc_ref.shape[2]

    def lane_fold(x, op):
        parts = [x[:, c * LANES:(c + 1) * LANES] for c in range(x.shape[1] // LANES)]
        while len(parts) > 1:
            parts = [op(parts[i], parts[i + 1]) if i + 1 < len(parts) else parts[i] for i in range(0, len(parts), 2)]
        return parts[0]

    def head(q, kc, vc, k_at, v_at, s_ref):
        spans = [(0, past)] + [(past + j * tk, tk) for j in range(nk)]
        mx = None
        for idx, (c0, width) in enumerate(spans):
            s = _dot_nt(q, kc if idx == 0 else k_at(idx - 1))
            s_ref[:, c0:c0 + width] = s
            part = lane_fold(s, jnp.maximum)
            mx = part if mx is None else jnp.maximum(mx, part)
        m = mx.max(axis=-1, keepdims=True)
        lsum = None
        acc = None
        for idx, (c0, width) in enumerate(spans):
            p = jnp.exp2(s_ref[:, c0:c0 + width] - m)
            part = lane_fold(p, jnp.add)
            pv = _dot(p.astype(bf), vc if idx == 0 else v_at(idx - 1))
            lsum = part if lsum is None else lsum + part
            acc = pv if acc is None else acc + pv
        return acc / lsum.sum(axis=-1, keepdims=True)

    def unit(q_lo, q_hi, kc_lo, kc_hi, vc, k_lo_at, k_hi_at, v_at):
        o_lo = head(q_lo, kc_lo, vc, k_lo_at, v_at, s_lo_ref)
        o_hi = head(q_hi, kc_hi, vc, k_hi_at, v_at, s_hi_ref)
        return jnp.where(lo, o_lo, o_hi)

    for p in range(MLA_HEADS // 2):
        c0 = 2 * p * LANES
        c1 = c0 + LANES
        o = unit(qm_ref[0, :, c0:c1], qm_ref[0, :, c1:c1 + LANES],
                 kmc_ref[0, 0, :, c0:c1], kmc_ref[0, 0, :, c1:c1 + LANES], vmc_ref[0, 0, :, p * LANES:(p + 1) * LANES],
                 lambda j, c0=c0, c1=c1: kml_ref[0, j * tk:(j + 1) * tk, c0:c1],
                 lambda j, c1=c1: kml_ref[0, j * tk:(j + 1) * tk, c1:c1 + LANES],
                 lambda j, p=p: vml_ref[0, j * tk:(j + 1) * tk, p * LANES:(p + 1) * LANES])
        o_ref[0, :, p * LANES:(p + 1) * LANES] = o.astype(o_ref.dtype)
    kgc = kgc_ref[0, 0]
    vgc = vgc_ref[0, 0]
    kg_at = lambda j: kgl_ref[0, j * tk:(j + 1) * tk, :]
    vg_at = lambda j: vgl_ref[0, j * tk:(j + 1) * tk, :]
    for c in range(GQA_WIDTH // LANES):
        q_lo, q_hi = _split_heads(qg_ref[0, :, c * LANES:(c + 1) * LANES])
        o = unit(q_lo, q_hi, kgc, kgc, vgc, kg_at, kg_at, vg_at)
        o_ref[0, :, MLA_WIDTH + c * LANES:MLA_WIDTH + (c + 1) * LANES] = o.astype(o_ref.dtype)


def _dense_attn(qm, qg, kmc, vmc, kgc, vgc, kml, vml, kgl, vgl, layer, *, batch, seq):
    tq, tk = DENSE_Q_TILE, DENSE_K_TILE
    r3 = lambda a: a.reshape(batch, seq, a.shape[-1])
    qm, qg, kml, vml, kgl, vgl = (r3(a) for a in (qm, qg, kml, vml, kgl, vgl))
    qspec = lambda a: pl.BlockSpec((1, tq, a.shape[-1]), lambda b, i: (b, i, 0))
    cspec = lambda a: pl.BlockSpec((1, 1) + a.shape[2:], lambda b, i: (b, layer, 0, 0))
    lspec = lambda a: pl.BlockSpec((1, seq, a.shape[-1]), lambda b, i: (b, 0, 0), pipeline_mode=pl.Buffered(1))
    o = pl.pallas_call(
        functools.partial(_dense_attn_kernel, nk=seq // tk, tk=tk),
        grid=(batch, seq // tq),
        in_specs=[qspec(qm), qspec(qg), cspec(kmc), cspec(vmc), cspec(kgc), cspec(vgc),
                  lspec(kml), lspec(vml), lspec(kgl), lspec(vgl)],
        out_specs=pl.BlockSpec((1, tq, DENSE_WIDTH), lambda b, i: (b, i, 0)),
        out_shape=jax.ShapeDtypeStruct((batch, seq, DENSE_WIDTH), jnp.bfloat16),
        scratch_shapes=[pltpu.VMEM((tq, kmc.shape[2] + seq), jnp.float32)] * 2,
        compiler_params=_params("arbitrary", "arbitrary"),
        name="dense_attn",
    )(qm, qg, kmc, vmc, kgc, vgc, kml, vml, kgl, vgl)
    return o.reshape(batch * seq, DENSE_WIDTH)


def _mla_ctx_kernel(ckv_ref, kr_ref, wuk_ref, wuv_ref, k_ref, v_ref):
    ckv = ckv_ref[0, 0]
    kn = _dot(ckv, wuk_ref[0])
    k_ref[0, 0] = (kn + jnp.concatenate([kr_ref[0, 0].astype(jnp.float32)] * MLA_HEADS, axis=1)).astype(k_ref.dtype)
    v_ref[0, 0] = _dot(ckv, wuv_ref[0]).astype(v_ref.dtype)


def _mla_ctx_expand(ckv, kr_pad, wuk, wuv):
    b, nl, past, _ = ckv.shape
    spec = lambda w: pl.BlockSpec((1, 1, past, w), lambda i, l: (i, l, 0, 0))
    wspec = lambda a: pl.BlockSpec((1,) + a.shape[1:], lambda i, l: (l, 0, 0))
    return pl.pallas_call(
        _mla_ctx_kernel,
        grid=(b, nl),
        in_specs=[spec(MLA_KV_LORA), spec(LANES), wspec(wuk), wspec(wuv)],
        out_specs=[spec(MLA_HEADS * LANES), spec(MLA_WIDTH)],
        out_shape=[jax.ShapeDtypeStruct((b, nl, past, MLA_HEADS * LANES), jnp.bfloat16),
                   jax.ShapeDtypeStruct((b, nl, past, MLA_WIDTH), jnp.bfloat16)],
        compiler_params=_params("arbitrary", "arbitrary"),
        name="mla_ctx_expand",
    )(ckv, kr_pad, wuk, wuv)


def _mix_ffn_kernel(ona_ref, odn_ref, x_ref, onap_ref, odnp_ref, xp_ref, onan_ref, odnn_ref, xn_ref, mod_ref,
                    wna_ref, wdn_ref, gpost_ref, gpre_ref, wup_ref, cw_ref, cb_ref, wd_ref, gffn_ref,
                    o_ref, hperm_ref, yperm_ref, *, seq, chunk):
    i = pl.program_id(0)
    t, d = x_ref.shape
    halo = xp_ref.shape[0]
    dff = wd_ref.shape[0]
    groups = t // 8
    mod = mod_ref[0]
    gt_m, sh_f, sc_f, gt_f = mod[2:3, :], mod[3:4, :], mod[4:5, :], mod[5:6, :]

    cat0 = lambda parts: jnp.concatenate(parts, axis=0)
    use_halo = seq > t
    if use_halo:
        ona = cat0([ona_ref[...], onap_ref[...], onan_ref[...]])
        odn = cat0([odn_ref[...], odnp_ref[...], odnn_ref[...]])
        x = cat0([x_ref[...], xp_ref[...], xn_ref[...]])
    else:
        ona, odn, x = ona_ref[...], odn_ref[...], x_ref[...]
    o = _dot(jnp.concatenate([ona, odn], axis=1), jnp.concatenate([wna_ref[...], wdn_ref[...]], axis=0))
    x1 = x + gt_m * _rms(o, gpost_ref[...])
    h2 = _rms(x1, gpre_ref[...]) * (1.0 + sc_f) + sh_f
    o_ref[...] = x1[0:t, :]

    nlb = d // LANES
    for j in range(nlb):
        for k in range(8):
            hperm_ref[j, pl.ds(k, groups, stride=8), :] = h2[k * groups:(k + 1) * groups, j * LANES:(j + 1) * LANES]
    h = jnp.concatenate([hperm_ref[j] for j in range(nlb)], axis=1)
    if use_halo:
        h = cat0([h, h2[t:, :]])
    h = h.astype(jnp.bfloat16)

    kk = lax.broadcasted_iota(jnp.int32, (8, 1), 0)
    tok = i * t + groups * kk
    prev_zero = jnp.bitwise_and(tok, seq - 1) == 0
    next_zero = jnp.bitwise_and(tok + groups, seq - 1) == 0

    def conv(c0):
        w = wup_ref[:, c0:c0 + chunk]
        u_all = _dot(h, w)
        u = u_all[0:t, :]
        b_prev = pltpu.roll(u[t - 8:t, :], 1, 0)
        b_next = pltpu.roll(u[0:8, :], 7, 0)
        if use_halo:
            b_prev = jnp.where(kk == 0, u_all[t + halo - 1:t + halo, :], b_prev)
            b_next = jnp.where(kk == 7, u_all[t + halo:t + halo + 1, :], b_next)
        b_prev = jnp.where(prev_zero, 0.0, b_prev)
        b_next = jnp.where(next_zero, 0.0, b_next)
        prev = jnp.concatenate([b_prev, u[0:t - 8, :]], axis=0)
        nxt = jnp.concatenate([u[8:t, :], b_next], axis=0)
        cw = cw_ref[:, c0:c0 + chunk]
        return prev * cw[0:1, :] + u * cw[1:2, :] + nxt * cw[2:3, :] + cb_ref[:, c0:c0 + chunk]

    acts = []
    for c in range(dff // chunk):
        gate = conv(c * chunk)
        val = conv(dff + c * chunk)
        acts.append((gate / (1.0 + jnp.exp(-gate)) * val).astype(jnp.bfloat16))
    acc = _dot(jnp.concatenate(acts, axis=1), wd_ref[...])

    y = gt_f * _rms(acc, gffn_ref[...])
    for j in range(nlb):
        yperm_ref[j] = y[:, j * LANES:(j + 1) * LANES]
    for k in range(8):
        rows = slice(k * groups, (k + 1) * groups)
        yk = jnp.concatenate([yperm_ref[j, pl.ds(k, groups, stride=8), :] for j in range(nlb)], axis=1)
        o_ref[rows, :] = o_ref[rows, :] + yk


def _mix_ffn(ona, odn, x, mods, lw, *, seq, seq_per_mod):
    ntok, d = x.shape
    t = FFN_TILE
    halo = 16
    assert seq & (seq - 1) == 0 and (t % seq == 0 or seq % t == 0) and ntok % t == 0 and t % 64 == 0
    nhalo = ntok // halo
    if seq_per_mod is None:
        mod_spec = pl.BlockSpec((1, 6, d), lambda i: (0, 0, 0))
    else:
        tps = seq_per_mod // t
        mod_spec = pl.BlockSpec((1, 6, d), lambda i: (i // tps, 0, 0))
    tok = lambda w: pl.BlockSpec((t, w), lambda i: (i, 0))
    prv = lambda w: pl.BlockSpec((halo, w), lambda i: (jnp.maximum(i * (t // halo) - 1, 0), 0))
    nxt = lambda w: pl.BlockSpec((halo, w), lambda i: (jnp.minimum((i + 1) * (t // halo), nhalo - 1), 0))
    full = lambda e: _layer_spec(e, buffered=True)
    weights = [lw["w_out_na"], lw["w_out_dn"], lw["g_post_mix"], lw["g_pre_ffn"], lw["w_up"], lw["conv_w"],
               lw["conv_b"], lw["w_down"], lw["g_post_ffn"]]
    widths = (NA_WIDTH, DENSE_WIDTH, d)
    return pl.pallas_call(
        functools.partial(_mix_ffn_kernel, seq=seq, chunk=FF_CHUNK),
        grid=(ntok // t,),
        in_specs=[tok(w) for w in widths] + [prv(w) for w in widths] + [nxt(w) for w in widths] + [mod_spec]
                 + [full(a) for a in weights],
        out_specs=tok(d),
        out_shape=jax.ShapeDtypeStruct((ntok, d), jnp.float32),
        scratch_shapes=[pltpu.VMEM((d // LANES, t, LANES), jnp.float32)] * 2,
        compiler_params=_params("arbitrary"),
        name="mix_ffn",
    )(ona, odn, x, ona, odn, x, ona, odn, x, mods, *[w[0] for w in weights])


def _prep_weights(w_in, mla_w_uq, mla_w_ukv, w_out, gqa_g_q, gqa_g_k):
    nl, d, _ = w_in.shape
    bf = jnp.bfloat16
    cat = lambda parts: jnp.concatenate(parts, axis=-1)
    pad_r = LANES - MLA_NOPE - MLA_ROPE
    z = lambda rows, n: jnp.zeros((nl, rows, n), bf)
    assert 3 * NA_WIDTH + MLA_Q_LORA + MLA_KV_LORA == C_QG and C_QG % PROJ_GROUP == 0

    uq = mla_w_uq.astype(bf)
    ukv = mla_w_ukv.astype(bf)
    ql, kl = uq.shape[1], ukv.shape[1]
    qw = MLA_NOPE + MLA_ROPE
    kvw = MLA_NOPE + MLA_V
    w_uq = cat([p for h in range(MLA_HEADS) for p in (uq[..., qw * h:qw * (h + 1)], z(ql, pad_r))])
    w_uk = cat([p for h in range(MLA_HEADS) for p in (ukv[..., kvw * h:kvw * h + MLA_NOPE], z(kl, LANES - MLA_NOPE))])
    w_uv = cat([ukv[..., kvw * h + MLA_NOPE:kvw * (h + 1)] for h in range(MLA_HEADS)])

    wo = w_out.astype(bf)
    o_gqa = NA_WIDTH + MLA_WIDTH
    w_out_dn = jnp.concatenate([wo[:, NA_WIDTH:o_gqa]]
                               + [wo[:, o_gqa + HEAD_DIM * h:o_gqa + HEAD_DIM * (h + 1)] for h in GQA_ORDER], axis=1)

    ggq = jnp.tile(gqa_g_q, (1, GQA_HEADS))[:, None, :]
    ggk = jnp.tile(gqa_g_k, (1, GQA_KV_HEADS))[:, None, :]
    ones = jnp.asarray(np.kron(np.eye(LANES // HEAD_DIM), np.full((HEAD_DIM, HEAD_DIM), 1.0 / HEAD_DIM)), bf)
    return dict(w_uq=w_uq, w_uk=w_uk, w_uv=w_uv, w_out_na=wo[:, :NA_WIDTH], w_out_dn=w_out_dn,
                ggq=ggq, ggk=ggk, ones=ones)


def _rope_tables(seq):
    f32 = np.float32
    t = np.arange(seq)
    r, c = (t // GRID_W).astype(f32), (t % GRID_W).astype(f32)

    def tables(dim):
        quarter = dim // 4
        freqs = f32(ROPE_THETA) ** (-np.arange(quarter, dtype=f32) / f32(quarter))
        ar_, ac_ = r[:, None] * freqs[None, :], c[:, None] * freqs[None, :]
        cos = np.concatenate([np.cos(ar_), np.cos(ar_), np.cos(ac_), np.cos(ac_)], axis=1)
        sin = np.concatenate([-np.sin(ar_), np.sin(ar_), -np.sin(ac_), np.sin(ac_)], axis=1)
        return cos.astype(f32), sin.astype(f32)

    c64, s64 = tables(HEAD_DIM)
    c32, s32 = tables(MLA_ROPE)
    pad = LANES - MLA_NOPE - MLA_ROPE
    cm = np.concatenate([np.ones((seq, MLA_NOPE), f32), c32, np.zeros((seq, pad), f32)], axis=1)
    sm = np.concatenate([np.zeros((seq, MLA_NOPE), f32), s32, np.zeros((seq, pad), f32)], axis=1)
    return dict(cg=jnp.asarray(np.tile(c64, (1, 2))), sg=jnp.asarray(np.tile(s64, (1, 2))),
                cm=jnp.asarray(cm), sm=jnp.asarray(sm))


def kernel(x_prompt, x_sample, c, cache_na_k, cache_na_v, cache_mla_ckv, cache_mla_krope, cache_gqa_k, cache_gqa_v, c_ctx, w_ada, b_ada, g_pre_mix, g_post_mix, g_pre_ffn, g_post_ffn, w_in, na_rpb, mla_g_q, mla_w_uq, mla_g_kv, mla_w_ukv, gqa_g_q, gqa_g_k, w_out, ffn_w_up, ffn_conv_w, ffn_conv_b, ffn_w_down):
    batch, seq, d = x_prompt.shape
    dbatch, dseq, _ = x_sample.shape
    nl = w_ada.shape[0]
    past = cache_na_k.shape[2]
    bf = jnp.bfloat16
    assert dbatch + 1 <= 8 and dseq % GRID_W == 0

    cond = jnp.zeros((8, d), jnp.float32).at[0].set(c_ctx).at[1:1 + dbatch].set(c)
    mods_all = _ada_mods(cond, w_ada, b_ada).reshape(nl, 8, 6, d)

    pw = _prep_weights(w_in, mla_w_uq, mla_w_ukv, w_out, gqa_g_q, gqa_g_k)
    rope = _rope_tables(dseq)
    vec = lambda g: g[:, None, :]
    stacked = dict(
        g_pre_mix=vec(g_pre_mix), g_post_mix=vec(g_post_mix), g_pre_ffn=vec(g_pre_ffn), g_post_ffn=vec(g_post_ffn),
        mla_g_q=vec(mla_g_q), mla_g_kv=vec(mla_g_kv), w_in=w_in,
        w_uq=pw["w_uq"], w_uk=pw["w_uk"], w_uv=pw["w_uv"], ggq=pw["ggq"], ggk=pw["ggk"],
        w_out_na=pw["w_out_na"], w_out_dn=pw["w_out_dn"],
        w_up=ffn_w_up.astype(bf), w_down=ffn_w_down.astype(bf), conv_w=ffn_conv_w, conv_b=vec(ffn_conv_b))

    kc_na = cache_na_k.reshape(dbatch, nl, past, NA_WIDTH).astype(bf)
    vc_na = cache_na_v.reshape(dbatch, nl, past, NA_WIDTH).astype(bf)
    kc_g = cache_gqa_k.reshape(dbatch, nl, past, GQA_KV_HEADS * HEAD_DIM).astype(bf)
    vc_g = cache_gqa_v.reshape(dbatch, nl, past, GQA_KV_HEADS * HEAD_DIM).astype(bf)
    kr_pad = jnp.pad(cache_mla_krope, ((0, 0), (0, 0), (0, 0), (MLA_NOPE, LANES - MLA_NOPE - MLA_ROPE))).astype(bf)
    kc_m, vc_m = _mla_ctx_expand(cache_mla_ckv.astype(bf), kr_pad, pw["w_uk"], pw["w_uv"])

    na_pairs = _na_pair_tables(na_rpb)

    xp = x_prompt.reshape(batch * seq, d)
    xs = x_sample.reshape(dbatch * dseq, d)
    cache_widths = (NA_WIDTH, NA_WIDTH, MLA_KV_LORA, MLA_ROPE, GQA_KV_HEADS * HEAD_DIM, GQA_KV_HEADS * HEAD_DIM)
    caches = [jnp.zeros((batch, nl, seq, w), jnp.float32) for w in cache_widths]
    for l in range(nl):
        lw = {k: (v, l) for k, v in stacked.items()}
        lw["ones"] = (pw["ones"], None)
        mods_ctx = mods_all[l, 0:1]
        mods_lat = mods_all[l, 1:1 + dbatch]

        ona, kna, vna, odn, ckv, kr, kg, vg = _proj(xp, mods_ctx, lw, None, latent=False, seq=seq, caches=caches, layer=l)
        caches = [kna, vna, ckv, kr, kg, vg]
        xp = _mix_ffn(ona, odn, xp, mods_ctx, lw, seq=seq, seq_per_mod=None)

        qna, kna, vna, qm, km, vm, qg, kg, vg = _proj(xs, mods_lat, lw, rope, latent=True, seq=dseq)
        ona = _na_attn(qna, kna, vna, kc_na, vc_na, na_pairs, l, batch=dbatch, seq=dseq)
        odn = _dense_attn(qm, qg, kc_m, vc_m, kc_g, vc_g, km, vm, kg, vg, l, batch=dbatch, seq=dseq)
        xs = _mix_ffn(ona, odn, xs, mods_lat, lw, seq=dseq, seq_per_mod=dseq)

    heads = lambda a, h: a.reshape(a.shape[:3] + (h, HEAD_DIM))
    return (xp.reshape(batch, seq, d), xs.reshape(dbatch, dseq, d),
            heads(caches[0], NA_HEADS), heads(caches[1], NA_HEADS), caches[2], caches[3],
            heads(caches[4], GQA_KV_HEADS), heads(caches[5], GQA_KV_HEADS))
```
